```python
import math
import jax
import jax.numpy as jnp
from jax import lax
import numpy as np

D_MODEL = 1024
BATCH = 4
SEQ = 4096
DEPTH = 4
DEC_BATCH = 32
DEC_SEQ = 4
PAST_LEN = 8192
PAGE_SIZE = 128

N_MIXERS = 2
N_ATTN_LAYERS = (DEPTH + 1) // 2
N_MLSTM_LAYERS = DEPTH // 2
NORM_EPS = 1e-6
D_FF = ((8 * D_MODEL // 3 + 127) // 128) * 128
N_HEADS = 16
HEAD_DIM = D_MODEL // N_HEADS
KV_HEADS = 4
GROUP = N_HEADS // KV_HEADS
ROT_DIM = HEAD_DIM // 4
ROPE_THETA = 500000.0
CMP_BLOCK = 32
CMP_STRIDE = 16
CMP_HIDDEN = 2 * HEAD_DIM
SEL_BLOCK = 64
SEL_TOPK = 16
WINDOW = 512
Q_BLOCK = 64
NSA_Q = N_HEADS * HEAD_DIM
NSA_KV = 3 * 2 * KV_HEADS * HEAD_DIM
NSA_PROJ = NSA_Q + NSA_KV + 3 * N_HEADS
NEG = -1e30
BIG = 1e30
ML_INNER = 2 * D_MODEL
ML_HEADS = 4
ML_HEAD_DIM = ML_INNER // ML_HEADS
QKV_BLOCK = 4
CONV_W = 4
ML_CHUNK = 64
ML_PROJ = 2 * ML_INNER + 2 * ML_HEADS

kernel_name = 'nsa_mlstm_macaron_adaln_decode_step'


def _rmsnorm(x, g):
    xf = x.astype(jnp.float32)
    y = xf * lax.rsqrt(jnp.mean(xf * xf, axis=-1, keepdims=True) + NORM_EPS)
    return (y * g.astype(jnp.float32)).astype(x.dtype)


def _modulate(x, g, shift, scale):
    return _rmsnorm(x, g) * (1.0 + scale[:, None, :]) + shift[:, None, :]


def _adaln(c, w, b):
    return (jax.nn.silu(c) @ w + b).reshape(c.shape[0], 9, D_MODEL)


def _swiglu(u, w_in, w_out):
    gate, up = jnp.split(u @ w_in, 2, axis=-1)
    return (jax.nn.silu(gate) * up) @ w_out


def _half_ffn(x, mod, slot, g, w_in, w_out):
    shift, scale, gate = mod[:, 3 * slot], mod[:, 3 * slot + 1], mod[:, 3 * slot + 2]
    return x + 0.5 * (1.0 + gate[:, None, :]) * _swiglu(_modulate(x, g, shift, scale), w_in, w_out)


def _rope_partial(x, pos):
    half = ROT_DIM // 2
    inv = ROPE_THETA ** (-jnp.arange(0, ROT_DIM, 2, dtype=jnp.float32) / ROT_DIM)
    ang = pos.astype(jnp.float32)[:, None] * inv[None, :]
    shp = (pos.shape[0],) + (1,) * (x.ndim - 3) + (half,)
    cos = jnp.cos(ang).reshape(shp)
    sin = jnp.sin(ang).reshape(shp)
    xf = x.astype(jnp.float32)
    x1, x2 = xf[..., :half], xf[..., half:ROT_DIM]
    out = jnp.concatenate([x1 * cos - x2 * sin, x1 * sin + x2 * cos, xf[..., ROT_DIM:]], axis=-1)
    return out.astype(x.dtype)


def _nsa_project(u, w_in, pos):
    b, t, _ = u.shape
    proj = u @ w_in
    q = _rope_partial(proj[..., :NSA_Q].reshape(b, t, N_HEADS, HEAD_DIM), pos)
    kv = proj[..., NSA_Q:NSA_Q + NSA_KV].reshape(b, t, 3, 2, KV_HEADS, HEAD_DIM)
    kv = jnp.stack([_rope_partial(kv[:, :, :, 0], pos), kv[:, :, :, 1]], axis=3)
    gates = jax.nn.sigmoid(proj[..., NSA_Q + NSA_KV:].astype(jnp.float32)).reshape(b, t, 3, N_HEADS)
    return q, kv[:, :, 0], kv[:, :, 1], kv[:, :, 2], gates


def _compress(kv, pos_emb, w1, w2):
    b, l = kv.shape[:2]
    nseg = l // CMP_STRIDE
    seg = kv[:, :nseg * CMP_STRIDE].reshape(b, nseg, CMP_STRIDE, 2, KV_HEADS, HEAD_DIM)
    pe = pos_emb[:, :, None, :]
    first = jnp.einsum('bnlsgd,sldf->bnsgf', seg + pe[:CMP_STRIDE], w1[:, :CMP_STRIDE])
    second = jnp.einsum('bnlsgd,sldf->bnsgf', seg + pe[CMP_STRIDE:], w1[:, CMP_STRIDE:])
    hid = jax.nn.gelu(first[:, :-1] + second[:, 1:])
    out = jnp.einsum('bcsgf,sfd->bcsgd', hid, w2)
    return out[:, :, 0], out[:, :, 1]


def _cmp_branch(q, kc, vc, q_pos):
    b, tq = q.shape[:2]
    nc = kc.shape[1]
    qg = q.reshape(b, tq, KV_HEADS, GROUP, HEAD_DIM)
    s = jnp.einsum('bqgrd,bcgd->bgrqc', qg, kc).astype(jnp.float32) * (HEAD_DIM ** -0.5)
    blk_last = jnp.arange(nc) * CMP_STRIDE + CMP_BLOCK - 1
    valid = blk_last[None, :] <= q_pos[:, None]
    p = jax.nn.softmax(jnp.where(valid, s, NEG), axis=-1) * valid
    o = jnp.einsum('bgrqc,bcgd->bqgrd', p, vc.astype(jnp.float32)).reshape(b, tq, N_HEADS, HEAD_DIM)
    return o, p.sum(axis=2)


def _select_blocks(p_grp, q_pos, seq_len):
    nc = p_grp.shape[-1]
    n_sel = -(-seq_len // SEL_BLOCK)
    i = np.arange(nc)[:, None]
    j = np.arange(n_sel)[None, :]
    overlap = np.maximum(np.minimum(i * CMP_STRIDE + CMP_BLOCK, (j + 1) * SEL_BLOCK) - np.maximum(i * CMP_STRIDE, j * SEL_BLOCK), 0)
    m = jnp.asarray(overlap / CMP_STRIDE, dtype=jnp.float32)
    imp = jnp.einsum('bgqc,cj->bgqj', p_grp, m)
    cur = (q_pos // SEL_BLOCK)[:, None]
    jj = jnp.arange(n_sel)[None, :]
    forced = (jj == 0) | (jj == cur) | (jj == cur - 1)
    score = jnp.where(forced, BIG, jnp.where(jj <= cur, imp, NEG))
    _, idx = lax.top_k(score, min(SEL_TOPK, n_sel))
    return idx


def _attend_gathered(qg, k, v, mask):
    s = jnp.einsum('bgrqd,bgqnd->bgrqn', qg, k).astype(jnp.float32) * (HEAD_DIM ** -0.5)
    p = jax.nn.softmax(jnp.where(mask[:, :, None], s, NEG), axis=-1)
    return jnp.einsum('bgrqn,bgqnd->bgrqd', p, v.astype(jnp.float32))


def _attend_shared(qg, k, v, mask):
    s = jnp.einsum('bgrqd,bngd->bgrqn', qg, k).astype(jnp.float32) * (HEAD_DIM ** -0.5)
    p = jax.nn.softmax(jnp.where(mask, s, NEG), axis=-1)
    return jnp.einsum('bgrqn,bngd->bgrqd', p, v.astype(jnp.float32))


def _heads_to_seq(o):
    b, _, _, t, _ = o.shape
    return o.transpose(0, 3, 1, 2, 4).reshape(b, t, N_HEADS, HEAD_DIM)


def _nsa_combine(u, gates, o_cmp, o_sel, o_win, w_out):
    b, t = u.shape[:2]
    o = gates[:, :, 0, :, None] * o_cmp + gates[:, :, 1, :, None] * o_sel + gates[:, :, 2, :, None] * o_win
    return o.reshape(b, t, NSA_Q).astype(u.dtype) @ w_out


def _nsa_prompt(u, w_in, w_out, pos_emb, w1, w2):
    b, t, _ = u.shape
    pos = jnp.arange(t)
    q, cmp_kv, sel_kv, win_kv, gates = _nsa_project(u, w_in, pos)
    kc, vc = _compress(cmp_kv, pos_emb, w1, w2)
    o_cmp, p_grp = _cmp_branch(q, kc, vc, pos)
    idx = _select_blocks(p_grp, pos, t)
    n_sel = -(-t // SEL_BLOCK)
    sel_pad = jnp.pad(sel_kv, ((0, 0), (0, n_sel * SEL_BLOCK - t), (0, 0), (0, 0), (0, 0)))
    sel_blocks = sel_pad.reshape(b, n_sel, SEL_BLOCK, 2, KV_HEADS, HEAD_DIM).transpose(0, 4, 1, 2, 3, 5)
    win_pad = jnp.pad(win_kv, ((0, 0), (WINDOW, 0), (0, 0), (0, 0), (0, 0)))
    qg = q.reshape(b, t, KV_HEADS, GROUP, HEAD_DIM).transpose(0, 2, 3, 1, 4)
    qb_len = math.gcd(t, Q_BLOCK)
    n_top = idx.shape[-1]
    bi = jnp.arange(b)[:, None, None, None]
    gi = jnp.arange(KV_HEADS)[None, :, None, None]

    def block(start):
        qp = start + jnp.arange(qb_len)
        qb = lax.dynamic_slice_in_dim(qg, start, qb_len, axis=3)
        ib = lax.dynamic_slice_in_dim(idx, start, qb_len, axis=2)
        kvg = sel_blocks[bi, gi, ib].reshape(b, KV_HEADS, qb_len, n_top * SEL_BLOCK, 2, HEAD_DIM)
        kp = (ib[..., None] * SEL_BLOCK + jnp.arange(SEL_BLOCK)).reshape(b, KV_HEADS, qb_len, n_top * SEL_BLOCK)
        o_sel = _attend_gathered(qb, kvg[..., 0, :], kvg[..., 1, :], kp <= qp[:, None])
        wkv = lax.dynamic_slice_in_dim(win_pad, start, qb_len + WINDOW, axis=1)
        kpw = start - WINDOW + jnp.arange(qb_len + WINDOW)
        d = qp[:, None] - kpw[None, :]
        o_win = _attend_shared(qb, wkv[:, :, 0], wkv[:, :, 1], (kpw[None, :] >= 0) & (d >= 0) & (d < WINDOW))
        return o_sel, o_win

    o_sel, o_win = lax.map(block, jnp.arange(0, t, qb_len))
    o_sel = _heads_to_seq(o_sel.transpose(1, 2, 3, 0, 4, 5).reshape(b, KV_HEADS, GROUP, t, HEAD_DIM))
    o_win = _heads_to_seq(o_win.transpose(1, 2, 3, 0, 4, 5).reshape(b, KV_HEADS, GROUP, t, HEAD_DIM))
    out = _nsa_combine(u, gates, o_cmp, o_sel, o_win, w_out)
    return out, cmp_kv, sel_kv, win_kv[:, t - min(WINDOW, t):]


def _nsa_sample(u, w_in, w_out, pos_emb, w1, w2, cache_cmp_l, cache_sel_l, win_buf, page_table):
    b, t, _ = u.shape
    n_pages = page_table.shape[1]
    past = n_pages * PAGE_SIZE
    pos = past + jnp.arange(t)
    q, cmp_kv, sel_kv, win_kv, gates = _nsa_project(u, w_in, pos)
    past_cmp = cache_cmp_l[page_table].reshape(b, past, 2, KV_HEADS, HEAD_DIM)
    all_cmp = jnp.concatenate([past_cmp, cmp_kv.astype(past_cmp.dtype)], axis=1)
    kc, vc = _compress(all_cmp, pos_emb, w1, w2)
    o_cmp, p_grp = _cmp_branch(q, kc, vc, pos)
    idx = _select_blocks(p_grp, pos, past + t)
    bpp = PAGE_SIZE // SEL_BLOCK
    n_past_blk = past // SEL_BLOCK
    n_tail = -(-t // SEL_BLOCK)
    pool_blk = cache_sel_l.reshape(-1, bpp, SEL_BLOCK, 2, KV_HEADS, HEAD_DIM)
    tail = jnp.pad(sel_kv.astype(cache_sel_l.dtype), ((0, 0), (0, n_tail * SEL_BLOCK - t), (0, 0), (0, 0), (0, 0)))
    tail = tail.reshape(b, n_tail, SEL_BLOCK, 2, KV_HEADS, HEAD_DIM)
    bi = jnp.arange(b)[:, None, None, None]
    gi = jnp.arange(KV_HEADS)[None, :, None, None]
    jc = jnp.minimum(idx, n_past_blk - 1)
    phys = page_table[bi, jc // bpp]
    from_pool = pool_blk[phys, jc % bpp, :, :, gi]
    jt = jnp.clip(idx - n_past_blk, 0, n_tail - 1)
    from_tail = tail[bi, jt, :, :, gi]
    kvg = jnp.where((idx < n_past_blk)[..., None, None, None], from_pool, from_tail)
    n_top = idx.shape[-1]
    kvg = kvg.reshape(b, KV_HEADS, t, n_top * SEL_BLOCK, 2, HEAD_DIM)
    kp = (idx[..., None] * SEL_BLOCK + jnp.arange(SEL_BLOCK)).reshape(b, KV_HEADS, t, n_top * SEL_BLOCK)
    qg = q.reshape(b, t, KV_HEADS, GROUP, HEAD_DIM).transpose(0, 2, 3, 1, 4)
    o_sel = _heads_to_seq(_attend_gathered(qg, kvg[..., 0, :], kvg[..., 1, :], kp <= pos[:, None]))
    wb = win_buf.shape[1]
    wkv = jnp.concatenate([win_buf, win_kv.astype(win_buf.dtype)], axis=1)
    kpw = past - wb + jnp.arange(wb + t)
    d = pos[:, None] - kpw[None, :]
    o_win = _heads_to_seq(_attend_shared(qg, wkv[:, :, 0], wkv[:, :, 1], (d >= 0) & (d < WINDOW)))
    out = _nsa_combine(u, gates, o_cmp, o_sel, o_win, w_out)
    return out, cmp_kv, sel_kv, wkv[:, wb + t - min(WINDOW, wb + t):]


def _headwise(x, w):
    b, t, _ = x.shape
    return jnp.einsum('btnc,ncd->btnd', x.reshape(b, t, -1, QKV_BLOCK), w).reshape(b, t, -1)


def _mlstm_chunkwise(q, k, v, ig, lf, c0, n0, m0):
    b, h, t, d = q.shape
    lc = math.gcd(t, ML_CHUNK)
    nck = t // lc
    tri = jnp.tril(jnp.ones((lc, lc), dtype=bool))

    def chunks(a):
        return jnp.moveaxis(a.reshape(a.shape[:2] + (nck, lc) + a.shape[3:]), 2, 0)

    def step(carry, inp):
        c, n, m = carry
        qc, kc, vc, ic, fc = inp
        bcum = jnp.cumsum(fc, axis=-1)
        dmat = jnp.where(tri, bcum[..., :, None] - bcum[..., None, :] + ic[..., None, :], -jnp.inf)
        inter = bcum + m[..., None]
        m_t = jnp.maximum(inter, dmat.max(axis=-1))
        w_int = jnp.exp(inter - m_t)
        s = jnp.einsum('bhtd,bhsd->bhts', qc, kc) * jnp.exp(dmat - m_t[..., None])
        num = w_int[..., None] * jnp.einsum('bhtd,bhde->bhte', qc, c) + jnp.einsum('bhts,bhse->bhte', s, vc)
        den = w_int * jnp.einsum('bhtd,bhd->bht', qc, n) + s.sum(axis=-1)
        hc = num / jnp.maximum(jnp.abs(den), jnp.exp(-m_t))[..., None]
        m_new = m_t[..., -1]
        w_dec = jnp.exp(bcum[..., -1] + m - m_new)
        w_s = jnp.exp(bcum[..., -1:] - bcum + ic - m_new[..., None])
        c_new = w_dec[..., None, None] * c + jnp.einsum('bhs,bhsd,bhse->bhde', w_s, kc, vc)
        n_new = w_dec[..., None] * n + jnp.einsum('bhs,bhsd->bhd', w_s, kc)
        return (c_new, n_new, m_new), hc

    (c, n, m), hs = lax.scan(step, (c0, n0, m0), (chunks(q), chunks(k), chunks(v), chunks(ig), chunks(lf)))
    return jnp.moveaxis(hs, 0, 2).reshape(b, h, t, d), c, n, m


def _mlstm_mixer(u, w_in, b_gate, conv_w, conv_b, wq, wk, wv, skip, gn_g, w_out, conv_buf, c0, n0, m0):
    b, t, _ = u.shape
    proj = u @ w_in
    xm = proj[..., :ML_INNER]
    z = proj[..., ML_INNER:2 * ML_INNER]
    gl = proj[..., 2 * ML_INNER:].astype(jnp.float32).reshape(b, t, 2, ML_HEADS) + b_gate.astype(jnp.float32)
    ig = gl[:, :, 0].transpose(0, 2, 1)
    lf = jax.nn.log_sigmoid(gl[:, :, 1]).transpose(0, 2, 1)
    xpad = jnp.concatenate([conv_buf.astype(xm.dtype), xm], axis=1)
    xc = conv_b
    for w in range(CONV_W):
        xc = xc + xpad[:, w:w + t] * conv_w[w]
    xc = jax.nn.silu(xc)

    def heads(a):
        return a.reshape(b, t, ML_HEADS, ML_HEAD_DIM).transpose(0, 2, 1, 3).astype(jnp.float32)

    q = heads(_headwise(xc, wq))
    k = heads(_headwise(xc, wk)) * (ML_HEAD_DIM ** -0.5)
    v = heads(_headwise(xm, wv))
    hcell, c, n, m = _mlstm_chunkwise(q, k, v, ig, lf, c0.astype(jnp.float32), n0.astype(jnp.float32), m0.astype(jnp.float32))
    hcell = hcell.transpose(0, 2, 1, 3) * jax.nn.sigmoid(z.astype(jnp.float32)).reshape(b, t, ML_HEADS, ML_HEAD_DIM)
    hn = hcell * lax.rsqrt(jnp.mean(hcell * hcell, axis=-1, keepdims=True) + NORM_EPS)
    hn = hn * gn_g.astype(jnp.float32).reshape(ML_HEADS, ML_HEAD_DIM)
    hn = hn.reshape(b, t, ML_INNER) + skip.astype(jnp.float32) * xc.astype(jnp.float32)
    return hn.astype(u.dtype) @ w_out, c, n, m, xpad[:, t:]


def setup_inputs(seed: int = 0) -> dict:
    key = jax.random.key(seed)
    ks = iter(jax.random.split(key, 40))

    def nrm(shape, scale):
        return jax.random.normal(next(ks), shape, jnp.float32) * scale

    n_pages = PAST_LEN // PAGE_SIZE
    n_used = DEC_BATCH * n_pages
    n_pool = n_used + (n_used + 3) // 4
    win_len = min(WINDOW, PAST_LEN)
    page_table = jax.random.permutation(next(ks), n_pool)[:n_used].reshape(DEC_BATCH, n_pages).astype(jnp.int32)
    b_gate = jnp.stack([nrm((N_MLSTM_LAYERS, ML_HEADS), 0.1), 3.0 + nrm((N_MLSTM_LAYERS, ML_HEADS), 0.5)], axis=1)
    return {
        'x_prompt': nrm((BATCH, SEQ, D_MODEL), 1.0),
        'x_sample': nrm((DEC_BATCH, DEC_SEQ, D_MODEL), 1.0),
        'cache_cmp': nrm((N_ATTN_LAYERS, n_pool, PAGE_SIZE, 2, KV_HEADS, HEAD_DIM), 1.0),
        'cache_sel': nrm((N_ATTN_LAYERS, n_pool, PAGE_SIZE, 2, KV_HEADS, HEAD_DIM), 1.0),
        'cache_win': nrm((N_ATTN_LAYERS, DEC_BATCH, win_len, 2, KV_HEADS, HEAD_DIM), 1.0),
        'state_C': nrm((N_MLSTM_LAYERS, DEC_BATCH, ML_HEADS, ML_HEAD_DIM, ML_HEAD_DIM), 0.05),
        'state_n': nrm((N_MLSTM_LAYERS, DEC_BATCH, ML_HEADS, ML_HEAD_DIM), 0.5),
        'state_m': nrm((N_MLSTM_LAYERS, DEC_BATCH, ML_HEADS), 0.5),
        'state_conv': nrm((N_MLSTM_LAYERS, DEC_BATCH, CONV_W - 1, ML_INNER), 1.0),
        'page_table': page_table,
        'c_prompt': nrm((BATCH, D_MODEL), 1.0),
        'c_sample': nrm((DEC_BATCH, D_MODEL), 1.0),
        'ada_w': nrm((DEPTH, D_MODEL, 9 * D_MODEL), 0.5 * D_MODEL ** -0.5),
        'ada_b': nrm((DEPTH, 9 * D_MODEL), 0.02),
        'norm_g': 1.0 + nrm((DEPTH, 3, D_MODEL), 0.05),
        'ffn_w_in': nrm((DEPTH, 2, D_MODEL, 2 * D_FF), D_MODEL ** -0.5),
        'ffn_w_out': nrm((DEPTH, 2, D_FF, D_MODEL), D_FF ** -0.5),
        'final_norm_g': 1.0 + nrm((D_MODEL,), 0.05),
        'attn_w_in': nrm((N_ATTN_LAYERS, D_MODEL, NSA_PROJ), D_MODEL ** -0.5),
        'attn_w_out': nrm((N_ATTN_LAYERS, NSA_Q, D_MODEL), NSA_Q ** -0.5),
        'cmp_pos': nrm((N_ATTN_LAYERS, CMP_BLOCK, 2, HEAD_DIM), 0.1),
        'cmp_w1': nrm((N_ATTN_LAYERS, 2, CMP_BLOCK, HEAD_DIM, CMP_HIDDEN), (CMP_BLOCK * HEAD_DIM) ** -0.5),
        'cmp_w2': nrm((N_ATTN_LAYERS, 2, CMP_HIDDEN, HEAD_DIM), CMP_HIDDEN ** -0.5),
        'ml_w_in': nrm((N_MLSTM_LAYERS, D_MODEL, ML_PROJ), D_MODEL ** -0.5),
        'ml_b_gate': b_gate,
        'ml_conv_w': nrm((N_MLSTM_LAYERS, CONV_W, ML_INNER), CONV_W ** -0.5),
        'ml_conv_b': nrm((N_MLSTM_LAYERS, ML_INNER), 0.02),
        'ml_wq': nrm((N_MLSTM_LAYERS, ML_INNER // QKV_BLOCK, QKV_BLOCK, QKV_BLOCK), QKV_BLOCK ** -0.5),
        'ml_wk': nrm((N_MLSTM_LAYERS, ML_INNER // QKV_BLOCK, QKV_BLOCK, QKV_BLOCK), QKV_BLOCK ** -0.5),
        'ml_wv': nrm((N_MLSTM_LAYERS, ML_INNER // QKV_BLOCK, QKV_BLOCK, QKV_BLOCK), QKV_BLOCK ** -0.5),
        'ml_skip': 1.0 + nrm((N_MLSTM_LAYERS, ML_INNER), 0.1),
        'ml_gn_g': 1.0 + nrm((N_MLSTM_LAYERS, ML_INNER), 0.05),
        'ml_w_out': nrm((N_MLSTM_LAYERS, ML_INNER, D_MODEL), ML_INNER ** -0.5),
    }


def reference(x_prompt, x_sample, cache_cmp, cache_sel, cache_win, state_C, state_n, state_m, state_conv, page_table,
              c_prompt, c_sample, ada_w, ada_b, norm_g, ffn_w_in, ffn_w_out, final_norm_g,
              attn_w_in, attn_w_out, cmp_pos, cmp_w1, cmp_w2,
              ml_w_in, ml_b_gate, ml_conv_w, ml_conv_b, ml_wq, ml_wk, ml_wv, ml_skip, ml_gn_g, ml_w_out):
    xp, xs = x_prompt, x_sample
    cmp_p, sel_p, win_p, cst_p, nst_p, mst_p, conv_p = [], [], [], [], [], [], []
    cmp_s, sel_s, win_s, cst_s, nst_s, mst_s, conv_s = [], [], [], [], [], [], []
    for i in range(DEPTH):
        mp = _adaln(c_prompt, ada_w[i], ada_b[i])
        ms = _adaln(c_sample, ada_w[i], ada_b[i])
        xp = _half_ffn(xp, mp, 0, norm_g[i, 0], ffn_w_in[i, 0], ffn_w_out[i, 0])
        xs = _half_ffn(xs, ms, 0, norm_g[i, 0], ffn_w_in[i, 0], ffn_w_out[i, 0])
        up = _modulate(xp, norm_g[i, 1], mp[:, 3], mp[:, 4])
        us = _modulate(xs, norm_g[i, 1], ms[:, 3], ms[:, 4])
        j = i // N_MIXERS
        if i % N_MIXERS == 0:
            hp, ckv, skv, wkv = _nsa_prompt(up, attn_w_in[j], attn_w_out[j], cmp_pos[j], cmp_w1[j], cmp_w2[j])
            hs, ckv_s, skv_s, wkv_s = _nsa_sample(us, attn_w_in[j], attn_w_out[j], cmp_pos[j], cmp_w1[j], cmp_w2[j],
                                                  cache_cmp[j], cache_sel[j], cache_win[j], page_table)
            cmp_p.append(ckv)
            sel_p.append(skv)
            win_p.append(wkv)
            cmp_s.append(ckv_s)
            sel_s.append(skv_s)
            win_s.append(wkv_s)
        else:
            bp = up.shape[0]
            hp, c_new, n_new, m_new, buf_new = _mlstm_mixer(
                up, ml_w_in[j], ml_b_gate[j], ml_conv_w[j], ml_conv_b[j], ml_wq[j], ml_wk[j], ml_wv[j],
                ml_skip[j], ml_gn_g[j], ml_w_out[j],
                jnp.zeros((bp, CONV_W - 1, ML_INNER), up.dtype),
                jnp.zeros((bp, ML_HEADS, ML_HEAD_DIM, ML_HEAD_DIM), jnp.float32),
                jnp.zeros((bp, ML_HEADS, ML_HEAD_DIM), jnp.float32),
                jnp.zeros((bp, ML_HEADS), jnp.float32))
            hs, c_s, n_s, m_s, buf_s = _mlstm_mixer(
                us, ml_w_in[j], ml_b_gate[j], ml_conv_w[j], ml_conv_b[j], ml_wq[j], ml_wk[j], ml_wv[j],
                ml_skip[j], ml_gn_g[j], ml_w_out[j], state_conv[j], state_C[j], state_n[j], state_m[j])
            cst_p.append(c_new)
            nst_p.append(n_new)
            mst_p.append(m_new)
            conv_p.append(buf_new)
            cst_s.append(c_s)
            nst_s.append(n_s)
            mst_s.append(m_s)
            conv_s.append(buf_s)
        xp = xp + (1.0 + mp[:, 5])[:, None, :] * hp
        xs = xs + (1.0 + ms[:, 5])[:, None, :] * hs
        xp = _half_ffn(xp, mp, 2, norm_g[i, 2], ffn_w_in[i, 1], ffn_w_out[i, 1])
        xs = _half_ffn(xs, ms, 2, norm_g[i, 2], ffn_w_in[i, 1], ffn_w_out[i, 1])
    y_prompt = _rmsnorm(xp, final_norm_g)
    y_sample = _rmsnorm(xs, final_norm_g)
    return (y_prompt, y_sample,
            jnp.stack(cmp_p), jnp.stack(sel_p), jnp.stack(win_p),
            jnp.stack(cst_p), jnp.stack(nst_p), jnp.stack(mst_p), jnp.stack(conv_p),
            jnp.stack(cmp_s), jnp.stack(sel_s), jnp.stack(win_s),
            jnp.stack(cst_s), jnp.stack(nst_s), jnp.stack(mst_s), jnp.stack(conv_s))
```

```python
import functools
import math

import numpy as np
import jax
import jax.numpy as jnp
from jax import lax
from jax.experimental import pallas as pl
from jax.experimental.pallas import tpu as pltpu

D_MODEL = 1024
D_FF = 2816
N_HEADS = 16
HEAD_DIM = 64
KV_HEADS = 4
GROUP = 4
ROT_DIM = 16
ROPE_THETA = 500000.0
CMP_BLOCK = 32
CMP_STRIDE = 16
CMP_HIDDEN = 128
SEL_BLOCK = 64
SEL_TOPK = 16
WINDOW = 512
PAGE_SIZE = 128
KV_LANES = 2 * KV_HEADS * HEAD_DIM
NSA_Q = N_HEADS * HEAD_DIM
NSA_N = 2688
ML_INNER = 2048
ML_HEADS = 4
ML_HEAD_DIM = 512
ML_N = 4224
CONV_W = 4
NORM_EPS = 1e-6
NEG = -1e30
BIG = 1e30
SCALE = HEAD_DIM ** -0.5

F32 = jnp.float32
BF16 = jnp.bfloat16
LANE = 128
VMEM_LIMIT_BYTES = 56 * 1024 * 1024


def _params(*sem):
    return pltpu.CompilerParams(dimension_semantics=sem, vmem_limit_bytes=VMEM_LIMIT_BYTES)


def _resident(shape):
    nd = len(shape)
    return pl.BlockSpec(shape, lambda *_: (0,) * nd, pipeline_mode=pl.Buffered(1))


def _dot(a, b):
    return jnp.dot(a, b, preferred_element_type=F32)


def _dot_nt(a, b):
    return lax.dot_general(a, b, (((1,), (1,)), ((), ())), preferred_element_type=F32)


def _dot_tn(a, b):
    return lax.dot_general(a, b, (((0,), (0,)), ((), ())), preferred_element_type=F32)


def _dot_exact(a, b):
    return jnp.dot(a, b, preferred_element_type=F32, precision=lax.Precision.HIGHEST)


def _dot_nt_exact(a, b):
    return lax.dot_general(a, b, (((1,), (1,)), ((), ())), preferred_element_type=F32,
                           precision=lax.Precision.HIGHEST)


def _modulated_norm(x, g, shift, scale):
    y = x * lax.rsqrt(jnp.mean(x * x, axis=-1, keepdims=True) + NORM_EPS)
    return (y * g) * (1.0 + scale) + shift


def _mod_spec(mod, tm, tiles_per_seq):
    if mod.ndim == 4:
        return pl.BlockSpec((None, 3, 1, D_MODEL), lambda i: (i // tiles_per_seq, 0, 0, 0))
    return pl.BlockSpec((3, tm, D_MODEL), lambda i: (0, i, 0))


def _adaln_kernel(c_ref, w_ref, b_ref, o_ref):
    a = jax.nn.silu(c_ref[...]).astype(BF16)
    o_ref[0] = _dot(a, w_ref[0].astype(BF16)) + b_ref[0]


def _adaln(c_all, ada_w, ada_b):
    depth, _, n = ada_w.shape
    rows = c_all.shape[0]
    tn = 2304
    return pl.pallas_call(
        _adaln_kernel,
        grid=(depth, n // tn),
        in_specs=[pl.BlockSpec((rows, D_MODEL), lambda l, j: (0, 0)),
                  pl.BlockSpec((1, D_MODEL, tn), lambda l, j: (l, 0, j)),
                  pl.BlockSpec((1, 1, tn), lambda l, j: (l, 0, j))],
        out_specs=pl.BlockSpec((1, rows, tn), lambda l, j: (l, 0, j)),
        out_shape=jax.ShapeDtypeStruct((depth, rows, n), F32),
        compiler_params=_params("arbitrary", "arbitrary"),
        name="adaln",
    )(c_all, ada_w, ada_b.reshape(depth, 1, n))


FF_CHUNK = 1408


def _ffn_kernel(x_ref, g_ref, mod_ref, win_ref, wout_ref, o_ref):
    x = x_ref[...]
    h = _modulated_norm(x, g_ref[...], mod_ref[0], mod_ref[1]).astype(BF16)
    acc = jnp.zeros(x.shape, F32)
    for c in range(D_FF // FF_CHUNK):
        lo = c * FF_CHUNK
        gate = _dot(h, win_ref[:, lo:lo + FF_CHUNK])
        up = _dot(h, win_ref[:, D_FF + lo:D_FF + lo + FF_CHUNK])
        act = (jax.nn.silu(gate) * up).astype(BF16)
        acc = acc + _dot(act, wout_ref[lo:lo + FF_CHUNK, :])
    o_ref[...] = x + 0.5 * (1.0 + mod_ref[2]) * acc


def _ffn(x, g, mod, w_in, w_out, tm, tiles_per_seq=1):
    rows = x.shape[0]
    return pl.pallas_call(
        _ffn_kernel,
        grid=(rows // tm,),
        in_specs=[pl.BlockSpec((tm, D_MODEL), lambda i: (i, 0)),
                  _resident((1, D_MODEL)),
                  _mod_spec(mod, tm, tiles_per_seq),
                  _resident((D_MODEL, 2 * D_FF)),
                  _resident((D_FF, D_MODEL))],
        out_specs=pl.BlockSpec((tm, D_MODEL), lambda i: (i, 0)),
        out_shape=jax.ShapeDtypeStruct((rows, D_MODEL), F32),
        compiler_params=_params("arbitrary"),
        name="ffn_half",
    )(x, g, mod, w_in, w_out)


def _final_norm_kernel(x_ref, g_ref, o_ref):
    x = x_ref[...]
    o_ref[...] = x * lax.rsqrt(jnp.mean(x * x, axis=-1, keepdims=True) + NORM_EPS) * g_ref[...]


def _final_norm(x, g, tm):
    rows = x.shape[0]
    return pl.pallas_call(
        _final_norm_kernel,
        grid=(rows // tm,),
        in_specs=[pl.BlockSpec((tm, D_MODEL), lambda i: (i, 0)), _resident((1, D_MODEL))],
        out_specs=pl.BlockSpec((tm, D_MODEL), lambda i: (i, 0)),
        out_shape=jax.ShapeDtypeStruct((rows, D_MODEL), F32),
        compiler_params=_params("arbitrary"),
        name="final_norm",
    )(x, g)


def _rope_tables(pos):
    half = ROT_DIM // 2
    inv = ROPE_THETA ** (-jnp.arange(0, ROT_DIM, 2, dtype=F32) / ROT_DIM)
    ang = pos.astype(F32)[:, None] * inv[None, :]
    cos, sin = jnp.cos(ang), jnp.sin(ang)
    n = pos.shape[0]
    pad = jnp.zeros((n, HEAD_DIM - ROT_DIM), F32)
    zeros = jnp.zeros((n, half), F32)
    c = jnp.concatenate([cos, cos, pad + 1.0], axis=1)
    s1 = jnp.concatenate([-sin, zeros, pad], axis=1)
    s2 = jnp.concatenate([zeros, sin, pad], axis=1)
    return tuple(jnp.tile(t, (1, LANE // HEAD_DIM)) for t in (c, s1, s2))


def _nsa_proj_kernel(x_ref, g_ref, mod_ref, w_ref, c_ref, s1_ref, s2_ref,
                     q_ref, cmp_ref, sel_ref, win_ref, gate_ref):
    h = _modulated_norm(x_ref[...], g_ref[...], mod_ref[0], mod_ref[1]).astype(BF16)
    proj = _dot(h, w_ref[...])
    c, s1, s2 = c_ref[...], s1_ref[...], s2_ref[...]

    def rope(v):
        return v * c + pltpu.roll(v, LANE - ROT_DIM // 2, 1) * s1 + pltpu.roll(v, ROT_DIM // 2, 1) * s2

    for j in range(NSA_Q // LANE):
        q_ref[:, j * LANE:(j + 1) * LANE] = rope(proj[:, j * LANE:(j + 1) * LANE])
    half = KV_LANES // 2
    for br, ref in enumerate((cmp_ref, sel_ref, win_ref)):
        base = NSA_Q + br * KV_LANES
        for j in range(half // LANE):
            ref[:, j * LANE:(j + 1) * LANE] = rope(proj[:, base + j * LANE:base + (j + 1) * LANE])
        ref[:, half:] = proj[:, base + half:base + KV_LANES]
    gate_ref[...] = jax.nn.sigmoid(proj[:, NSA_Q + 3 * KV_LANES:])


def _nsa_proj(x, g, mod, w, tables, tm, tiles_per_seq=1):
    rows = x.shape[0]
    trows = tables[0].shape[0]
    tspec = pl.BlockSpec((tm, LANE), lambda i: (i % (trows // tm), 0))
    row = lambda n: pl.BlockSpec((tm, n), lambda i: (i, 0))
    return pl.pallas_call(
        _nsa_proj_kernel,
        grid=(rows // tm,),
        in_specs=[row(D_MODEL), _resident((1, D_MODEL)), _mod_spec(mod, tm, tiles_per_seq),
                  _resident((D_MODEL, NSA_N)), tspec, tspec, tspec],
        out_specs=[row(NSA_Q), row(KV_LANES), row(KV_LANES), row(KV_LANES), row(LANE)],
        out_shape=[jax.ShapeDtypeStruct((rows, n), F32) for n in (NSA_Q, KV_LANES, KV_LANES, KV_LANES, LANE)],
        compiler_params=_params("arbitrary"),
        name="nsa_proj",
    )(x, g, mod, w, *tables)


def _compress_body(x_ref, pe_ref, w1_ref, w2_ref, out_ref, acc_ref, nseg):
    n_sg = 2 * KV_HEADS
    for l in range(CMP_STRIDE):
        xl = x_ref[:, l * KV_LANES:(l + 1) * KV_LANES].astype(BF16)
        for sg in range(n_sg):
            d = _dot(xl[:, sg * HEAD_DIM:(sg + 1) * HEAD_DIM], w1_ref[sg // KV_HEADS, l])
            if l == 0:
                acc_ref[sg] = d
            else:
                acc_ref[sg] += d
    row = lax.broadcasted_iota(jnp.int32, (nseg, 1), 0)
    for s in range(2):
        bias = jnp.zeros((8, 2 * CMP_HIDDEN), F32)
        lane = lax.broadcasted_iota(jnp.int32, (8, 2 * CMP_HIDDEN), 1)
        for l in range(CMP_STRIDE):
            pa = jnp.broadcast_to(pe_ref[s, l:l + 1, :], (8, HEAD_DIM)).astype(BF16)
            pb = jnp.broadcast_to(pe_ref[s, CMP_STRIDE + l:CMP_STRIDE + l + 1, :], (8, HEAD_DIM)).astype(BF16)
            bias = bias + jnp.where(lane < CMP_HIDDEN, _dot(pa, w1_ref[s, l]), _dot(pb, w1_ref[s, l]))
        bias = bias[0:1]
        for g in range(KV_HEADS):
            a = acc_ref[s * KV_HEADS + g] + bias
            first, second = a[:, :CMP_HIDDEN], a[:, CMP_HIDDEN:]
            hid = jax.nn.gelu(first + pltpu.roll(second, nseg - 1, 0))
            o = _dot(hid.astype(BF16), w2_ref[s])
            o = jnp.where(row < nseg - 1, o, 0.0)
            out_ref[s, :, g * HEAD_DIM:(g + 1) * HEAD_DIM] = o.astype(out_ref.dtype)


def _compress_prompt_kernel(x_ref, pe_ref, w1_ref, w2_ref, out_ref, acc_ref, *, nseg):
    _compress_body(x_ref.at[0], pe_ref, w1_ref, w2_ref, out_ref.at[0], acc_ref, nseg)


def _compress_weights(pos_emb, w1, w2):
    pe = jnp.transpose(pos_emb, (1, 0, 2))
    w1c = jnp.concatenate([w1[:, :CMP_STRIDE], w1[:, CMP_STRIDE:]], axis=-1).astype(BF16)
    return pe, w1c, w2.astype(BF16)


def _compress_prompt(cmp_kv, pe, w1c, w2b):
    b, nseg, seg_lanes = cmp_kv.shape
    return pl.pallas_call(
        functools.partial(_compress_prompt_kernel, nseg=nseg),
        grid=(b,),
        in_specs=[pl.BlockSpec((1, nseg, seg_lanes), lambda i: (i, 0, 0)),
                  _resident(pe.shape), _resident(w1c.shape), _resident(w2b.shape)],
        out_specs=pl.BlockSpec((1, 2, nseg, KV_LANES // 2), lambda i: (i, 0, 0, 0)),
        out_shape=jax.ShapeDtypeStruct((b, 2, nseg, KV_LANES // 2), BF16),
        scratch_shapes=[pltpu.VMEM((2 * KV_HEADS, nseg, 2 * CMP_HIDDEN), F32)],
        compiler_params=_params("arbitrary"),
        name="compress_prompt",
    )(cmp_kv, pe, w1c, w2b)


PAGES_PER_STEP = 8
SEGS_PER_PAGE = PAGE_SIZE // CMP_STRIDE


def _compress_paged_kernel(pt_ref, *refs, nseg, n_steps):
    del pt_ref
    pages = refs[:PAGES_PER_STEP]
    pe_ref, w1_ref, w2_ref, out_ref, stash_ref, acc_ref = refs[PAGES_PER_STEP:]
    step = pl.program_id(1)
    for k, page in enumerate(pages):
        r0 = pl.multiple_of((step * PAGES_PER_STEP + k) * SEGS_PER_PAGE, SEGS_PER_PAGE)
        stash_ref[pl.ds(r0, SEGS_PER_PAGE), :] = page[0]

    @pl.when(step == n_steps - 1)
    def _():
        _compress_body(stash_ref, pe_ref, w1_ref, w2_ref, out_ref.at[0], acc_ref, nseg)


def _compress_paged(cache, page_table, pe, w1c, w2b):
    b, n_pages = page_table.shape
    n_steps = n_pages // PAGES_PER_STEP
    nseg = n_pages * SEGS_PER_PAGE
    seg_lanes = CMP_STRIDE * KV_LANES

    def page_spec(k):
        return pl.BlockSpec((1, SEGS_PER_PAGE, seg_lanes), lambda i, s, pt: (pt[i, s * PAGES_PER_STEP + k], 0, 0))

    const = lambda shape: pl.BlockSpec(shape, lambda i, s, pt: (0,) * len(shape))
    grid_spec = pltpu.PrefetchScalarGridSpec(
        num_scalar_prefetch=1,
        grid=(b, n_steps),
        in_specs=[page_spec(k) for k in range(PAGES_PER_STEP)] + [const(pe.shape), const(w1c.shape), const(w2b.shape)],
        out_specs=pl.BlockSpec((1, 2, nseg, KV_LANES // 2), lambda i, s, pt: (i, 0, 0, 0)),
        scratch_shapes=[pltpu.VMEM((nseg, seg_lanes), F32),
                        pltpu.VMEM((2 * KV_HEADS, nseg, 2 * CMP_HIDDEN), F32)],
    )
    return pl.pallas_call(
        functools.partial(_compress_paged_kernel, nseg=nseg, n_steps=n_steps),
        grid_spec=grid_spec,
        out_shape=jax.ShapeDtypeStruct((b, 2, nseg, KV_LANES // 2), BF16),
        compiler_params=_params("arbitrary", "arbitrary"),
        name="compress_paged",
    )(page_table, *([cache] * PAGES_PER_STEP), pe, w1c, w2b)


def _importance_matrix(nc_pad, nc, n_sel_pad, n_sel):
    i = np.arange(nc_pad)[:, None]
    j = np.arange(n_sel_pad)[None, :]
    overlap = np.maximum(np.minimum(i * CMP_STRIDE + CMP_BLOCK, (j + 1) * SEL_BLOCK)
                         - np.maximum(i * CMP_STRIDE, j * SEL_BLOCK), 0)
    m = overlap / CMP_STRIDE * (i < nc) * (j < n_sel)
    return jnp.asarray(m, dtype=BF16)


def _topk_mask(score, k):
    n = score.shape[-1]
    lane = lax.broadcasted_iota(jnp.int32, score.shape, 1).astype(F32)

    def body(_, carry):
        sc, sel = carry
        m = jnp.max(sc, axis=-1, keepdims=True)
        first = jnp.min(jnp.where(sc == m, lane, float(n)), axis=-1, keepdims=True)
        hit = lane == first
        return jnp.where(hit, -jnp.inf, sc), jnp.where(hit, 1.0, sel)

    _, sel = lax.fori_loop(0, k, body, (score, jnp.zeros(score.shape, F32)))
    return sel


def _block_scores(imp, qpos, n_valid):
    jj = lax.broadcasted_iota(jnp.int32, imp.shape, 1)
    cur = qpos // SEL_BLOCK
    forced = (jj == 0) | (jj == cur) | (jj == cur - 1)
    score = jnp.where(forced, BIG, jnp.where(jj <= cur, imp, NEG))
    return jnp.where(jj < n_valid, score, -jnp.inf)


def _cmp_select_kernel(q_ref, kc_ref, vc_ref, imp_ref, gate_ref, o_ref, qa_ref, *, tq, nc, n_sel):
    q0 = pl.program_id(1) * tq
    rows = GROUP * tq
    qpos = q0 + lax.broadcasted_iota(jnp.int32, (rows, 1), 0) % tq
    cidx = lax.broadcasted_iota(jnp.int32, (1, kc_ref.shape[2]), 1)
    valid = (cidx * CMP_STRIDE + CMP_BLOCK - 1 <= qpos) & (cidx < nc)
    tpos = q0 + lax.broadcasted_iota(jnp.int32, (tq, 1), 0)
    for g in range(KV_HEADS):
        qg = q_ref[0, g * GROUP:(g + 1) * GROUP].reshape(rows, HEAD_DIM)
        s = jnp.where(valid, _dot_nt(qg, kc_ref[0, g]) * SCALE, NEG)
        e = jnp.exp(s - jnp.max(s, axis=-1, keepdims=True))
        p = jnp.where(valid, e / jnp.sum(e, axis=-1, keepdims=True), 0.0)
        o = _dot(p.astype(BF16), vc_ref[0, g])
        p_grp = p[0:tq]
        for r in range(1, GROUP):
            p_grp = p_grp + p[r * tq:(r + 1) * tq]
        imp = _dot(p_grp.astype(BF16), imp_ref[...])
        sel = _topk_mask(_block_scores(imp, tpos, n_sel), min(SEL_TOPK, n_sel))
        bias = jnp.where(sel > 0.0, 0.0, NEG).astype(BF16)
        for r in range(GROUP):
            h = g * GROUP + r
            o_ref[0, :, h * HEAD_DIM:(h + 1) * HEAD_DIM] = o[r * tq:(r + 1) * tq] * gate_ref[:, h:h + 1]
            qa_ref[0, h, :, 0:HEAD_DIM] = (q_ref[0, h].astype(F32) * SCALE).astype(BF16)
            qa_ref[0, h, :, HEAD_DIM:] = bias


def _cmp_select(qh, kch, vch, gates, tq):
    b, _, t, _ = qh.shape
    nseg = kch.shape[2]
    n_sel = t // SEL_BLOCK
    assert n_sel <= HEAD_DIM
    imp_m = _importance_matrix(nseg, nseg - 1, HEAD_DIM, n_sel)
    return pl.pallas_call(
        functools.partial(_cmp_select_kernel, tq=tq, nc=nseg - 1, n_sel=n_sel),
        grid=(b, t // tq),
        in_specs=[pl.BlockSpec((1, N_HEADS, tq, HEAD_DIM), lambda i, j: (i, 0, j, 0)),
                  pl.BlockSpec((1, KV_HEADS, nseg, HEAD_DIM), lambda i, j: (i, 0, 0, 0)),
                  pl.BlockSpec((1, KV_HEADS, nseg, HEAD_DIM), lambda i, j: (i, 0, 0, 0)),
                  pl.BlockSpec((nseg, HEAD_DIM), lambda i, j: (0, 0)),
                  pl.BlockSpec((tq, LANE), lambda i, j: (i * (t // tq) + j, 0))],
        out_specs=[pl.BlockSpec((1, tq, NSA_Q), lambda i, j: (i, j, 0)),
                   pl.BlockSpec((1, N_HEADS, tq, 2 * HEAD_DIM), lambda i, j: (i, 0, j, 0))],
        out_shape=[jax.ShapeDtypeStruct((b, t, NSA_Q), F32),
                   jax.ShapeDtypeStruct((b, N_HEADS, t, 2 * HEAD_DIM), BF16)],
        compiler_params=_params("arbitrary", "arbitrary"),
        name="cmp_select",
    )(qh, kch, vch, imp_m, gates)


def _flash_kernel(qa_ref, ks_ref, vs_ref, kw_ref, vw_ref, oc_ref, gate_ref, o_ref, *, tq, tk):
    g = pl.program_id(1)
    q0 = pl.program_id(2) * tq
    rows = GROUP * tq
    qa = qa_ref[0].reshape(rows, 2 * HEAD_DIM)
    qpos = q0 + lax.broadcasted_iota(jnp.int32, (rows, 1), 0) % tq

    def body(j, carry):
        m, l, acc = carry
        k0 = pl.multiple_of(j * tk, tk)
        s = _dot_nt(qa, ks_ref[0, 0, pl.ds(k0, tk), :])
        kpos = k0 + lax.broadcasted_iota(jnp.int32, (1, tk), 1)
        s = jnp.where(kpos <= qpos, s, NEG)
        m_new = jnp.maximum(m, jnp.max(s, axis=-1, keepdims=True))
        a = jnp.exp(m - m_new)
        p = jnp.exp(s - m_new)
        l = a * l + jnp.sum(p, axis=-1, keepdims=True)
        acc = a * acc + _dot(p.astype(BF16), vs_ref[0, 0, pl.ds(k0, tk), :])
        return m_new, l, acc

    n_kt = (q0 + tq + tk - 1) // tk
    init = (jnp.full((rows, 1), -jnp.inf, F32), jnp.zeros((rows, 1), F32), jnp.zeros((rows, HEAD_DIM), F32))
    _, l, acc = lax.fori_loop(0, n_kt, body, init)
    o_sel = acc / l

    wk = WINDOW + tq
    k0 = pl.multiple_of(jnp.maximum(q0 - WINDOW, 0), tq)
    s = _dot_nt(qa[:, :HEAD_DIM], kw_ref[0, 0, pl.ds(k0, wk), :])
    kpos = k0 + lax.broadcasted_iota(jnp.int32, (1, wk), 1)
    d = qpos - kpos
    s = jnp.where((d >= 0) & (d < WINDOW), s, NEG)
    p = jnp.exp(s - jnp.max(s, axis=-1, keepdims=True))
    o_win = _dot(p.astype(BF16), vw_ref[0, 0, pl.ds(k0, wk), :]) / jnp.sum(p, axis=-1, keepdims=True)

    gates = gate_ref[...]
    lane = lax.broadcasted_iota(jnp.int32, gates.shape, 1)
    for r in range(GROUP):
        g_sel = jnp.sum(jnp.where(lane == N_HEADS + g * GROUP + r, gates, 0.0), axis=-1, keepdims=True)
        g_win = jnp.sum(jnp.where(lane == 2 * N_HEADS + g * GROUP + r, gates, 0.0), axis=-1, keepdims=True)
        sl = slice(r * tq, (r + 1) * tq)
        o = oc_ref[0, :, r * HEAD_DIM:(r + 1) * HEAD_DIM] + g_sel * o_sel[sl] + g_win * o_win[sl]
        o_ref[0, :, r * HEAD_DIM:(r + 1) * HEAD_DIM] = o.astype(o_ref.dtype)


def _flash(qa, ks, vs, kw, vw, oc, gates, tq, tk):
    b, _, t, _ = qa.shape
    grp_lanes = GROUP * HEAD_DIM
    kv = lambda n: pl.BlockSpec((1, 1, t, n), lambda i, g, j: (i, g, 0, 0))
    return pl.pallas_call(
        functools.partial(_flash_kernel, tq=tq, tk=tk),
        grid=(b, KV_HEADS, t // tq),
        in_specs=[pl.BlockSpec((1, GROUP, tq, 2 * HEAD_DIM), lambda i, g, j: (i, g, j, 0)),
                  kv(2 * HEAD_DIM), kv(HEAD_DIM), kv(HEAD_DIM), kv(HEAD_DIM),
                  pl.BlockSpec((1, tq, grp_lanes), lambda i, g, j: (i, j, g)),
                  pl.BlockSpec((tq, LANE), lambda i, g, j: (i * (t // tq) + j, 0))],
        out_specs=pl.BlockSpec((1, tq, grp_lanes), lambda i, g, j: (i, j, g)),
        out_shape=jax.ShapeDtypeStruct((b, t, NSA_Q), BF16),
        compiler_params=_params("arbitrary", "arbitrary", "arbitrary"),
        name="nsa_flash",
    )(qa, ks, vs, kw, vw, oc, gates)


def _out_proj_kernel(x_ref, o_ref, w_ref, mod_ref, y_ref):
    y = _dot(o_ref[...].astype(BF16), w_ref[...])
    y_ref[...] = x_ref[...] + (1.0 + mod_ref[2]) * y


def _out_proj(x, o, w, mod, tm, tiles_per_seq=1):
    rows, k = o.shape
    return pl.pallas_call(
        _out_proj_kernel,
        grid=(rows // tm,),
        in_specs=[pl.BlockSpec((tm, D_MODEL), lambda i: (i, 0)),
                  pl.BlockSpec((tm, k), lambda i: (i, 0)),
                  _resident((k, D_MODEL)),
                  _mod_spec(mod, tm, tiles_per_seq)],
        out_specs=pl.BlockSpec((tm, D_MODEL), lambda i: (i, 0)),
        out_shape=jax.ShapeDtypeStruct((rows, D_MODEL), F32),
        compiler_params=_params("arbitrary"),
        name="out_proj",
    )(x, o, w, mod)


def _diag_fold(acc):
    rows = acc.shape[0]
    t = rows // (GROUP * KV_HEADS)
    row_g = (lax.broadcasted_iota(jnp.int32, (rows, 1), 0) // t) % KV_HEADS
    out = jnp.zeros((rows, HEAD_DIM), F32)
    for g in range(KV_HEADS):
        out = out + jnp.where(row_g == g, acc[:, g * HEAD_DIM:(g + 1) * HEAD_DIM], 0.0)
    return out


def _dec_cmp_select_kernel(q_ref, kc_ref, vc_ref, imp_ref, o_ref, bias_ref, *, t, nc, n_sel, past):
    rows = q_ref.shape[1]
    qt = KV_HEADS * t
    qpos = past + lax.broadcasted_iota(jnp.int32, (rows, 1), 0) % t
    cidx = lax.broadcasted_iota(jnp.int32, (1, kc_ref.shape[1]), 1)
    valid = (cidx * CMP_STRIDE + CMP_BLOCK - 1 <= qpos) & (cidx < nc)
    s = jnp.where(valid, _dot_nt(q_ref[0], kc_ref[0]), NEG)
    e = jnp.exp(s - jnp.max(s, axis=-1, keepdims=True))
    p = jnp.where(valid, e / jnp.sum(e, axis=-1, keepdims=True), 0.0)
    o_ref[0] = _diag_fold(_dot(p.astype(BF16), vc_ref[0]))
    p_grp = p[0:qt]
    for r in range(1, GROUP):
        p_grp = p_grp + p[r * qt:(r + 1) * qt]
    imp = _dot(p_grp.astype(BF16), imp_ref[...])
    sel = _topk_mask(_block_scores(imp, qpos[0:qt], n_sel), min(SEL_TOPK, n_sel))
    bias = jnp.where(sel > 0.0, 0.0, NEG).astype(BF16)
    for r in range(GROUP):
        bias_ref[0, r * qt:(r + 1) * qt, :] = bias


def _dec_cmp_select(qbd, kc, vc, t, past):
    b, rows, lanes = qbd.shape
    nseg = kc.shape[1]
    n_sel = -(-(past + t) // SEL_BLOCK)
    n_sel_pad = -(-n_sel // LANE) * LANE
    imp_m = _importance_matrix(nseg, nseg - 1, n_sel_pad, n_sel)
    return pl.pallas_call(
        functools.partial(_dec_cmp_select_kernel, t=t, nc=nseg - 1, n_sel=n_sel, past=past),
        grid=(b,),
        in_specs=[pl.BlockSpec((1, rows, lanes), lambda i: (i, 0, 0)),
                  pl.BlockSpec((1, nseg, lanes), lambda i: (i, 0, 0)),
                  pl.BlockSpec((1, nseg, lanes), lambda i: (i, 0, 0)),
                  pl.BlockSpec((nseg, n_sel_pad), lambda i: (0, 0))],
        out_specs=[pl.BlockSpec((1, rows, HEAD_DIM), lambda i: (i, 0, 0)),
                   pl.BlockSpec((1, rows, n_sel_pad), lambda i: (i, 0, 0))],
        out_shape=[jax.ShapeDtypeStruct((b, rows, HEAD_DIM), F32),
                   jax.ShapeDtypeStruct((b, rows, n_sel_pad), BF16)],
        compiler_params=_params("arbitrary"),
        name="dec_cmp_select",
    )(qbd, kc, vc, imp_m)


def _dec_attend_kernel(pt_ref, *refs, t, n_steps, past):
    del pt_ref
    pages = refs[:PAGES_PER_STEP]
    (q_ref, bias_ref, e_ref, tail_ref, wbuf_ref, wtail_ref,
     osel_ref, owin_ref, m_ref, l_ref, acc_ref) = refs[PAGES_PER_STEP:]
    step = pl.program_id(1)
    half = KV_LANES // 2
    q = q_ref[0]
    rows = q.shape[0]
    tq = lax.broadcasted_iota(jnp.int32, (rows, 1), 0) % t

    @pl.when(step == 0)
    def _():
        m_ref[...] = jnp.full(m_ref.shape, -jnp.inf, F32)
        l_ref[...] = jnp.zeros(l_ref.shape, F32)
        acc_ref[...] = jnp.zeros(acc_ref.shape, F32)

    def update(s, v):
        m = m_ref[...]
        m_new = jnp.maximum(m, jnp.max(s, axis=-1, keepdims=True))
        a = jnp.exp(m - m_new)
        p = jnp.exp(s - m_new)
        l_ref[...] = a * l_ref[...] + jnp.sum(p, axis=-1, keepdims=True)
        acc_ref[...] = a * acc_ref[...] + _dot(p.astype(BF16), v)
        m_ref[...] = m_new

    bias = bias_ref[0]
    for k, page in enumerate(pages):
        kv = page[0]
        s = _dot_nt(q, kv[:, :half].astype(BF16)) + _dot_nt(bias, e_ref[k * PAGE_SIZE:(k + 1) * PAGE_SIZE, :])
        update(s, kv[:, half:].astype(BF16))

    @pl.when(step == n_steps - 1)
    def _():
        tail = tail_ref[0]
        tpos = lax.broadcasted_iota(jnp.int32, (1, tail.shape[0]), 1)
        blk = past // SEL_BLOCK
        lane = lax.broadcasted_iota(jnp.int32, bias.shape, 1)
        b_tail = jnp.sum(jnp.where(lane == blk, bias.astype(F32), 0.0), axis=-1, keepdims=True)
        s = _dot_nt(q, tail[:, :half].astype(BF16)) + b_tail
        update(jnp.where(tpos <= tq, s, NEG), tail[:, half:].astype(BF16))
        osel_ref[0] = _diag_fold(acc_ref[...] / l_ref[...])

        wbuf = wbuf_ref[0]
        wb = wbuf.shape[0]
        wtail = wtail_ref[0]
        ipos = lax.broadcasted_iota(jnp.int32, (1, wb), 1)
        d = (past + tq) - (past - wb + ipos)
        s1 = jnp.where((d >= 0) & (d < WINDOW), _dot_nt(q, wbuf[:, :half].astype(BF16)), NEG)
        s2 = jnp.where(tpos <= tq, _dot_nt(q, wtail[:, :half].astype(BF16)), NEG)
        m = jnp.maximum(jnp.max(s1, axis=-1, keepdims=True), jnp.max(s2, axis=-1, keepdims=True))
        p1, p2 = jnp.exp(s1 - m), jnp.exp(s2 - m)
        den = jnp.sum(p1, axis=-1, keepdims=True) + jnp.sum(p2, axis=-1, keepdims=True)
        o = _dot(p1.astype(BF16), wbuf[:, half:].astype(BF16)) + _dot(p2.astype(BF16), wtail[:, half:].astype(BF16))
        owin_ref[0] = _diag_fold(o / den)


def _dec_attend(cache, page_table, qbd, bias, onehot, sel_tail, win_buf, win_tail, t, past):
    b, n_pages = page_table.shape
    n_steps = n_pages // PAGES_PER_STEP
    rows, lanes = qbd.shape[1:]
    n_sel_pad = bias.shape[2]
    keys_per_step = PAGES_PER_STEP * PAGE_SIZE

    def page_spec(k):
        return pl.BlockSpec((1, PAGE_SIZE, KV_LANES), lambda i, s, pt: (pt[i, s * PAGES_PER_STEP + k], 0, 0))

    per_seq = lambda arr: pl.BlockSpec((1,) + arr.shape[1:], lambda i, s, pt: (i,) + (0,) * (arr.ndim - 1))
    grid_spec = pltpu.PrefetchScalarGridSpec(
        num_scalar_prefetch=1,
        grid=(b, n_steps),
        in_specs=[page_spec(k) for k in range(PAGES_PER_STEP)]
        + [per_seq(qbd), per_seq(bias),
           pl.BlockSpec((keys_per_step, n_sel_pad), lambda i, s, pt: (s, 0)),
           per_seq(sel_tail), per_seq(win_buf), per_seq(win_tail)],
        out_specs=[pl.BlockSpec((1, rows, HEAD_DIM), lambda i, s, pt: (i, 0, 0)),
                   pl.BlockSpec((1, rows, HEAD_DIM), lambda i, s, pt: (i, 0, 0))],
        scratch_shapes=[pltpu.VMEM((rows, 1), F32), pltpu.VMEM((rows, 1), F32), pltpu.VMEM((rows, lanes), F32)],
    )
    return pl.pallas_call(
        functools.partial(_dec_attend_kernel, t=t, n_steps=n_steps, past=past),
        grid_spec=grid_spec,
        out_shape=[jax.ShapeDtypeStruct((b, rows, HEAD_DIM), F32)] * 2,
        compiler_params=_params("arbitrary", "arbitrary"),
        name="dec_attend",
    )(page_table, *([cache] * PAGES_PER_STEP), qbd, bias, onehot, sel_tail, win_buf, win_tail)


def _gated_out_proj_kernel(x_ref, oc_ref, os_ref, ow_ref, gate_ref, ex_ref, w_ref, mod_ref, y_ref):
    gates = gate_ref[...]
    o = jnp.zeros(oc_ref.shape, F32)
    for k, ref in enumerate((oc_ref, os_ref, ow_ref)):
        o = o + _dot_exact(gates, ex_ref[k]) * ref[...]
    y_ref[...] = x_ref[...] + (1.0 + mod_ref[2]) * _dot(o.astype(BF16), w_ref[...])


def _gate_expansion():
    ex = np.zeros((3, LANE, NSA_Q), np.float32)
    for k in range(3):
        for h in range(N_HEADS):
            ex[k, k * N_HEADS + h, h * HEAD_DIM:(h + 1) * HEAD_DIM] = 1.0
    return jnp.asarray(ex)


def _gated_out_proj(x, oc, osel, owin, gates, w, mod):
    rows = x.shape[0]
    full = lambda n: pl.BlockSpec((rows, n), lambda i: (0, 0))
    return pl.pallas_call(
        _gated_out_proj_kernel,
        grid=(1,),
        in_specs=[full(D_MODEL), full(NSA_Q), full(NSA_Q), full(NSA_Q), full(LANE),
                  pl.BlockSpec((3, LANE, NSA_Q), lambda i: (0, 0, 0)),
                  pl.BlockSpec((NSA_Q, D_MODEL), lambda i: (0, 0)),
                  pl.BlockSpec((3, rows, D_MODEL), lambda i: (0, 0, 0))],
        out_specs=full(D_MODEL),
        out_shape=jax.ShapeDtypeStruct((rows, D_MODEL), F32),
        compiler_params=_params("arbitrary"),
        name="gated_out_proj",
    )(x, oc, osel, owin, gates, _gate_expansion(), w, mod)


def _ml_proj_kernel(x_ref, g_ref, mod_ref, w_ref, bg_ref, xm_ref, z_ref, gl_ref):
    h = _modulated_norm(x_ref[...], g_ref[...], mod_ref[0], mod_ref[1]).astype(BF16)
    proj = _dot(h, w_ref[...])
    xm_ref[...] = proj[:, :ML_INNER]
    z_ref[...] = proj[:, ML_INNER:2 * ML_INNER]
    gl = proj[:, 2 * ML_INNER:] + bg_ref[...]
    lane = lax.broadcasted_iota(jnp.int32, gl.shape, 1)
    gl_ref[...] = jnp.where(lane < ML_HEADS, gl, jax.nn.log_sigmoid(gl))


def _ml_proj(x, g, mod, w, b_gate, tm, tiles_per_seq=1):
    rows = x.shape[0]
    row = lambda n: pl.BlockSpec((tm, n), lambda i: (i, 0))
    return pl.pallas_call(
        _ml_proj_kernel,
        grid=(rows // tm,),
        in_specs=[row(D_MODEL), _resident((1, D_MODEL)), _mod_spec(mod, tm, tiles_per_seq),
                  _resident((D_MODEL, ML_N)), _resident((1, LANE))],
        out_specs=[row(ML_INNER), row(ML_INNER), row(LANE)],
        out_shape=[jax.ShapeDtypeStruct((rows, n), F32) for n in (ML_INNER, ML_INNER, LANE)],
        compiler_params=_params("arbitrary"),
        name="ml_proj",
    )(x, g, mod, w, b_gate)


def _conv_qkv(taps, cw_ref, cb_ref, wqk_ref, wv_ref, xc_ref, q_ref, k_ref, v_ref):
    xc = cb_ref[...]
    for w in range(CONV_W):
        xc = xc + taps[w] * cw_ref[w:w + 1, :]
    xc = jax.nn.silu(xc)
    xc_ref[...] = xc
    xcb = xc.astype(BF16)
    xmb = taps[CONV_W - 1].astype(BF16)
    for j in range(ML_INNER // LANE):
        sl = slice(j * LANE, (j + 1) * LANE)
        qk = _dot(xcb[:, sl], wqk_ref[j])
        q_ref[:, sl] = qk[:, :LANE].astype(q_ref.dtype)
        k_ref[:, sl] = (qk[:, LANE:] * (ML_HEAD_DIM ** -0.5)).astype(k_ref.dtype)
        v_ref[:, sl] = _dot(xmb[:, sl], wv_ref[j]).astype(v_ref.dtype)


def _ml_conv_prompt_kernel(xm_ref, halo_ref, cw_ref, cb_ref, wqk_ref, wv_ref,
                           xc_ref, q_ref, k_ref, v_ref, pad_ref, *, tm, tiles_per_seq):
    first = pl.program_id(0) % tiles_per_seq == 0
    pad_ref[0:8, :] = jnp.where(first, 0.0, halo_ref[...])
    pad_ref[8:, :] = xm_ref[...]
    taps = [pad_ref[pl.ds(8 - (CONV_W - 1) + w, tm), :] for w in range(CONV_W)]
    _conv_qkv(taps, cw_ref, cb_ref, wqk_ref, wv_ref, xc_ref, q_ref, k_ref, v_ref)


def _ml_conv_sample_kernel(t0_ref, t1_ref, t2_ref, t3_ref, cw_ref, cb_ref, wqk_ref, wv_ref,
                           xc_ref, q_ref, k_ref, v_ref):
    taps = [r[...] for r in (t0_ref, t1_ref, t2_ref, t3_ref)]
    _conv_qkv(taps, cw_ref, cb_ref, wqk_ref, wv_ref, xc_ref, q_ref, k_ref, v_ref)


def _headwise_weights(wq, wk, wv):
    def blockdiag(w):
        n_chunks = ML_INNER // LANE
        per = LANE // 4
        wc = w.reshape(n_chunks, per, 4, 4)
        eye = jnp.eye(per, dtype=w.dtype)
        return jnp.einsum('jncd,nm->jncmd', wc, eye).reshape(n_chunks, LANE, LANE)
    return jnp.concatenate([blockdiag(wq), blockdiag(wk)], axis=-1).astype(BF16), blockdiag(wv).astype(BF16)


def _ml_conv_prompt(xm, cw, cb, wqk, wv, tm, tiles_per_seq):
    rows = xm.shape[0]
    row = pl.BlockSpec((tm, ML_INNER), lambda i: (i, 0))
    out = jax.ShapeDtypeStruct((rows, ML_INNER), BF16)
    return pl.pallas_call(
        functools.partial(_ml_conv_prompt_kernel, tm=tm, tiles_per_seq=tiles_per_seq),
        grid=(rows // tm,),
        in_specs=[row, pl.BlockSpec((8, ML_INNER), lambda i: (jnp.maximum(i * (tm // 8) - 1, 0), 0)),
                  _resident(cw.shape), _resident(cb.shape), _resident(wqk.shape), _resident(wv.shape)],
        out_specs=[row] * 4,
        out_shape=[jax.ShapeDtypeStruct((rows, ML_INNER), F32), out, out, out],
        scratch_shapes=[pltpu.VMEM((tm + 8, ML_INNER), F32)],
        compiler_params=_params("arbitrary"),
        name="ml_conv_prompt",
    )(xm, xm, cw, cb, wqk, wv)


def _ml_conv_sample(taps, cw, cb, wqk, wv):
    rows = taps[0].shape[0]
    full = pl.BlockSpec((rows, ML_INNER), lambda i: (0, 0))
    const = lambda a: pl.BlockSpec(a.shape, lambda i: (0,) * a.ndim)
    out = jax.ShapeDtypeStruct((rows, ML_INNER), F32)
    return pl.pallas_call(
        _ml_conv_sample_kernel,
        grid=(1,),
        in_specs=[full] * 4 + [const(cw), const(cb), const(wqk), const(wv)],
        out_specs=[full] * 4,
        out_shape=[out] * 4,
        compiler_params=_params("arbitrary"),
        name="ml_conv_sample",
    )(*taps, cw, cb, wqk, wv)


def _ml_cell_kernel(q_ref, k_ref, v_ref, gl_ref, c0_ref, n0_ref, m0_ref,
                    h_ref, c_ref, n_ref, m_ref, *, chunk, n_chunks):
    step = pl.program_id(1)

    @pl.when(step == 0)
    def _():
        c_ref[...] = c0_ref[...]
        n_ref[...] = n0_ref[...]
        m_ref[...] = m0_ref[...]

    gl = gl_ref[0]
    ti = lax.broadcasted_iota(jnp.int32, (chunk, chunk), 0)
    si = lax.broadcasted_iota(jnp.int32, (chunk, chunk), 1)
    tri = si <= ti
    cum = _dot_exact(tri.astype(F32), gl)
    sel = (lax.broadcasted_iota(jnp.int32, (8, LANE), 0) == lax.broadcasted_iota(jnp.int32, (8, LANE), 1)).astype(F32)
    gl_rows = _dot_nt_exact(sel, gl)
    cum_rows = _dot_nt_exact(sel, cum)
    for h in range(ML_HEADS):
        hs = slice(h * ML_HEAD_DIM, (h + 1) * ML_HEAD_DIM)
        q = q_ref[0, :, hs].astype(BF16)
        k = k_ref[0, :, hs].astype(F32)
        v = v_ref[0, :, hs].astype(BF16)
        ig_col = gl[:, h:h + 1]
        ig_row = gl_rows[h:h + 1, :]
        b_col = cum[:, ML_HEADS + h:ML_HEADS + h + 1]
        b_row = cum_rows[ML_HEADS + h:ML_HEADS + h + 1, :]
        b_last = b_col[chunk - 1:chunk, :]
        m_prev = m_ref[0, 0:1, h:h + 1]
        c_prev = c_ref[0, h]
        n_prev = n_ref[0, h:h + 1, :]

        dmat = jnp.where(tri, b_col - b_row + ig_row, -jnp.inf)
        inter = b_col + m_prev
        m_t = jnp.maximum(inter, jnp.max(dmat, axis=-1, keepdims=True))
        w_int = jnp.exp(inter - m_t)
        s = _dot_nt(q, k.astype(BF16)) * jnp.exp(dmat - m_t)
        num = w_int * _dot(q, c_prev.astype(BF16)) + _dot(s.astype(BF16), v)
        den = w_int * jnp.sum(q.astype(F32) * n_prev, axis=-1, keepdims=True) + jnp.sum(s, axis=-1, keepdims=True)
        h_ref[0, :, hs] = num / jnp.maximum(jnp.abs(den), jnp.exp(-m_t))

        m_new = m_t[chunk - 1:chunk, :]
        w_dec = jnp.exp(b_last + m_prev - m_new)
        kw = k * jnp.exp(b_last - b_col + ig_col - m_new)
        c_ref[0, h] = w_dec * c_prev + _dot_tn(kw.astype(BF16), v)
        n_ref[0, h:h + 1, :] = w_dec * n_prev + jnp.sum(kw, axis=0, keepdims=True)
        m_ref[0, :, h:h + 1] = jnp.broadcast_to(m_new, (8, 1))


def _ml_cell(q, k, v, gl, c0, n0, m0, chunk):
    b, t, _ = q.shape
    n_chunks = t // chunk
    seq = lambda n: pl.BlockSpec((1, chunk, n), lambda i, c: (i, c, 0))
    c_spec = pl.BlockSpec((1, ML_HEADS, ML_HEAD_DIM, ML_HEAD_DIM), lambda i, c: (i, 0, 0, 0))
    n_spec = pl.BlockSpec((1, ML_HEADS, ML_HEAD_DIM), lambda i, c: (i, 0, 0))
    m_spec = pl.BlockSpec((1, 8, LANE), lambda i, c: (i, 0, 0))
    return pl.pallas_call(
        functools.partial(_ml_cell_kernel, chunk=chunk, n_chunks=n_chunks),
        grid=(b, n_chunks),
        in_specs=[seq(ML_INNER), seq(ML_INNER), seq(ML_INNER), seq(LANE), c_spec, n_spec, m_spec],
        out_specs=[seq(ML_INNER), c_spec, n_spec, m_spec],
        out_shape=[jax.ShapeDtypeStruct((b, t, ML_INNER), F32),
                   jax.ShapeDtypeStruct(c0.shape, F32),
                   jax.ShapeDtypeStruct(n0.shape, F32),
                   jax.ShapeDtypeStruct(m0.shape, F32)],
        compiler_params=_params("arbitrary", "arbitrary"),
        name="ml_cell",
    )(q, k, v, gl, c0, n0, m0)


def _ml_out_kernel(x_ref, h_ref, z_ref, xc_ref, gn_ref, skip_ref, w_ref, mod_ref, y_ref):
    hg = h_ref[...] * jax.nn.sigmoid(z_ref[...])
    parts = []
    for h in range(ML_HEADS):
        hh = hg[:, h * ML_HEAD_DIM:(h + 1) * ML_HEAD_DIM]
        parts.append(hh * lax.rsqrt(jnp.mean(hh * hh, axis=-1, keepdims=True) + NORM_EPS))
    hn = jnp.concatenate(parts, axis=-1) * gn_ref[...] + skip_ref[...] * xc_ref[...]
    y_ref[...] = x_ref[...] + (1.0 + mod_ref[2]) * _dot(hn.astype(BF16), w_ref[...])


def _ml_out(x, hcell, z, xc, gn, skip, w, mod, tm, tiles_per_seq=1):
    rows = x.shape[0]
    row = lambda n: pl.BlockSpec((tm, n), lambda i: (i, 0))
    return pl.pallas_call(
        _ml_out_kernel,
        grid=(rows // tm,),
        in_specs=[row(D_MODEL), row(ML_INNER), row(ML_INNER), row(ML_INNER),
                  _resident((1, ML_INNER)), _resident((1, ML_INNER)), _resident((ML_INNER, D_MODEL)),
                  _mod_spec(mod, tm, tiles_per_seq)],
        out_specs=row(D_MODEL),
        out_shape=jax.ShapeDtypeStruct((rows, D_MODEL), F32),
        compiler_params=_params("arbitrary"),
        name="ml_out",
    )(x, hcell, z, xc, gn, skip, w, mod)


TM_PROMPT = 512
ML_CHUNK_PROMPT = 256
ML_CHUNK_SAMPLE = 8


def _pad_cols(w, n):
    return jnp.pad(w, ((0, 0), (0, n - w.shape[1])))


def _heads_major(x, b, t, n_heads):
    return x.reshape(b, t, n_heads, HEAD_DIM).transpose(0, 2, 1, 3)


def _nsa_prompt_layer(x, g, mod, w_in, w_out, cmp_w, tables, b, t):
    tiles = t // TM_PROMPT
    q, cmp_kv, sel_kv, win_kv, gates = _nsa_proj(x, g, mod, w_in, tables, TM_PROMPT, tiles)
    cmp = _compress_prompt(cmp_kv.reshape(b, t // CMP_STRIDE, CMP_STRIDE * KV_LANES), *cmp_w)
    nseg = cmp.shape[2]
    cmp_h = cmp.reshape(b, 2, nseg, KV_HEADS, HEAD_DIM).transpose(0, 1, 3, 2, 4)
    qh = _heads_major(q, b, t, N_HEADS).astype(BF16)
    o_cmp, q_aug = _cmp_select(qh, cmp_h[:, 0], cmp_h[:, 1], gates, tq=256)
    sel_h = _heads_major(sel_kv, b, t, 2 * KV_HEADS).astype(BF16)
    win_h = _heads_major(win_kv, b, t, 2 * KV_HEADS).astype(BF16)
    onehot = (jnp.arange(t)[:, None] // SEL_BLOCK == jnp.arange(HEAD_DIM)[None, :]).astype(BF16)
    ks_aug = jnp.concatenate([sel_h[:, :KV_HEADS], jnp.broadcast_to(onehot, (b, KV_HEADS, t, HEAD_DIM))], axis=-1)
    o = _flash(q_aug, ks_aug, sel_h[:, KV_HEADS:], win_h[:, :KV_HEADS], win_h[:, KV_HEADS:], o_cmp, gates,
               tq=128, tk=256)
    x = _out_proj(x, o.reshape(b * t, NSA_Q), w_out, mod, TM_PROMPT, tiles)
    kv6 = lambda a: a.reshape(b, t, 2, KV_HEADS, HEAD_DIM)
    return x, kv6(cmp_kv), kv6(sel_kv), kv6(win_kv)[:, t - min(WINDOW, t):]


def _nsa_sample_layer(x, g, mod, w_in, w_out, cmp_w, tables, cache_cmp, cache_sel, win_buf, page_table, b, t):
    rows = b * t
    past = page_table.shape[1] * PAGE_SIZE
    q, cmp_kv, sel_kv, win_kv, gates = _nsa_proj(x, g, mod, w_in, tables, rows)
    cmp = _compress_paged(cache_cmp.reshape(-1, SEGS_PER_PAGE, CMP_STRIDE * KV_LANES), page_table, *cmp_w)
    q5 = (q * SCALE).reshape(b, t, KV_HEADS, GROUP, HEAD_DIM).transpose(0, 3, 2, 1, 4)
    qbd = (q5[:, :, :, :, None, :] * jnp.eye(KV_HEADS, dtype=F32)[None, None, :, None, :, None])
    qbd = qbd.reshape(b, GROUP * KV_HEADS * t, KV_HEADS * HEAD_DIM).astype(BF16)
    o_cmp, bias = _dec_cmp_select(qbd, cmp[:, 0], cmp[:, 1], t, past)
    n_sel_pad = bias.shape[2]
    onehot = (jnp.arange(past)[:, None] // SEL_BLOCK == jnp.arange(n_sel_pad)[None, :]).astype(BF16)
    pad8 = lambda a: jnp.pad(a.reshape(b, t, KV_LANES), ((0, 0), (0, 8 - t), (0, 0)))
    wb = win_buf.shape[1]
    o_sel, o_win = _dec_attend(cache_sel.reshape(-1, PAGE_SIZE, KV_LANES), page_table, qbd, bias, onehot,
                               pad8(sel_kv), win_buf.reshape(b, wb, KV_LANES), pad8(win_kv), t, past)
    lane_dense = lambda o: o.reshape(b, GROUP, KV_HEADS, t, HEAD_DIM).transpose(0, 3, 2, 1, 4).reshape(rows, NSA_Q)
    x = _gated_out_proj(x, lane_dense(o_cmp), lane_dense(o_sel), lane_dense(o_win), gates, w_out, mod)
    kv6 = lambda a: a.reshape(b, t, 2, KV_HEADS, HEAD_DIM)
    wkv = jnp.concatenate([win_buf, kv6(win_kv)], axis=1)
    return x, kv6(cmp_kv), kv6(sel_kv), wkv[:, wb + t - min(WINDOW, wb + t):]


def _ml_weights(w_in, b_gate, conv_w, conv_b, wq, wk, wv, skip, gn_g, w_out):
    wqk, wvb = _headwise_weights(wq, wk, wv)
    bg = jnp.pad(b_gate.reshape(1, 2 * ML_HEADS), ((0, 0), (0, LANE - 2 * ML_HEADS)))
    return dict(w_in=_pad_cols(w_in, ML_N).astype(BF16), bg=bg, cw=conv_w, cb=conv_b.reshape(1, ML_INNER),
                wqk=wqk, wv=wvb, skip=skip.reshape(1, ML_INNER), gn=gn_g.reshape(1, ML_INNER),
                w_out=w_out.astype(BF16))


def _ml_prompt_layer(x, g, mod, w, b, t):
    tiles = t // TM_PROMPT
    xm, z, gl = _ml_proj(x, g, mod, w['w_in'], w['bg'], TM_PROMPT, tiles)
    xc, q, k, v = _ml_conv_prompt(xm, w['cw'], w['cb'], w['wqk'], w['wv'], TM_PROMPT, tiles)
    seq = lambda a: a.reshape(b, t, a.shape[-1])
    c0 = jnp.zeros((b, ML_HEADS, ML_HEAD_DIM, ML_HEAD_DIM), F32)
    n0 = jnp.zeros((b, ML_HEADS, ML_HEAD_DIM), F32)
    m0 = jnp.zeros((b, 8, LANE), F32)
    hcell, c, n, m = _ml_cell(seq(q), seq(k), seq(v), seq(gl), c0, n0, m0, ML_CHUNK_PROMPT)
    x = _ml_out(x, hcell.reshape(b * t, ML_INNER), z, xc, w['gn'], w['skip'], w['w_out'], mod, TM_PROMPT, tiles)
    return x, c, n, m[:, 0, :ML_HEADS], seq(xm)[:, t - (CONV_W - 1):]


def _ml_sample_layer(x, g, mod, w, conv_buf, c0, n0, m0, b, t):
    rows = b * t
    xm, z, gl = _ml_proj(x, g, mod, w['w_in'], w['bg'], rows)
    xpad = jnp.concatenate([conv_buf, xm.reshape(b, t, ML_INNER)], axis=1)
    taps = [xpad[:, k:k + t].reshape(rows, ML_INNER) for k in range(CONV_W)]
    xc, q, k, v = _ml_conv_sample(taps, w['cw'], w['cb'], w['wqk'], w['wv'])
    tp = ML_CHUNK_SAMPLE
    pad = lambda a: jnp.pad(a.reshape(b, t, a.shape[-1]), ((0, 0), (0, tp - t), (0, 0)))
    lane = jnp.arange(LANE)[None, None, :]
    step = jnp.arange(tp)[None, :, None]
    glp = jnp.where(step < t, pad(gl), jnp.where(lane < ML_HEADS, NEG, 0.0))
    m0p = jnp.broadcast_to(jnp.pad(m0, ((0, 0), (0, LANE - ML_HEADS)))[:, None, :], (b, 8, LANE))
    hcell, c, n, m = _ml_cell(pad(q), pad(k), pad(v), glp, c0, n0, m0p, tp)
    x = _ml_out(x, hcell[:, :t].reshape(rows, ML_INNER), z, xc, w['gn'], w['skip'], w['w_out'], mod, rows)
    return x, c, n, m[:, 0, :ML_HEADS], xpad[:, t:]


def kernel(x_prompt, x_sample, cache_cmp, cache_sel, cache_win, state_C, state_n, state_m, state_conv, page_table,
           c_prompt, c_sample, ada_w, ada_b, norm_g, ffn_w_in, ffn_w_out, final_norm_g,
           attn_w_in, attn_w_out, cmp_pos, cmp_w1, cmp_w2,
           ml_w_in, ml_b_gate, ml_conv_w, ml_conv_b, ml_wq, ml_wk, ml_wv, ml_skip, ml_gn_g, ml_w_out):
    bp, tp, _ = x_prompt.shape
    bs, ts, _ = x_sample.shape
    depth = ada_w.shape[0]
    rows_s = bs * ts
    past = page_table.shape[1] * PAGE_SIZE

    n_c = bp + bs
    c_all = jnp.pad(jnp.concatenate([c_prompt, c_sample], axis=0), ((0, -n_c % 8), (0, 0)))
    mods = _adaln(c_all, ada_w, ada_b)

    xp = x_prompt.reshape(bp * tp, D_MODEL)
    xs = x_sample.reshape(rows_s, D_MODEL)
    tables_p = _rope_tables(jnp.arange(tp))
    tables_s = tuple(jnp.tile(tb, (bs, 1)) for tb in _rope_tables(past + jnp.arange(ts)))
    tiles = tp // TM_PROMPT

    outs_p = {k: [] for k in ('cmp', 'sel', 'win', 'c', 'n', 'm', 'conv')}
    outs_s = {k: [] for k in ('cmp', 'sel', 'win', 'c', 'n', 'm', 'conv')}
    for i in range(depth):
        mod_i = mods[i, :n_c].reshape(n_c, 9, D_MODEL)
        mp = [mod_i[:bp, 3 * s:3 * s + 3].reshape(bp, 3, 1, D_MODEL) for s in range(3)]
        ms = [jnp.repeat(mod_i[bp:, 3 * s:3 * s + 3].transpose(1, 0, 2), ts, axis=1) for s in range(3)]
        g = norm_g[i].reshape(3, 1, D_MODEL)
        w_in = ffn_w_in[i].astype(BF16)
        w_out = ffn_w_out[i].astype(BF16)

        xp = _ffn(xp, g[0], mp[0], w_in[0], w_out[0], TM_PROMPT, tiles)
        xs = _ffn(xs, g[0], ms[0], w_in[0], w_out[0], rows_s)
        j = i // 2
        if i % 2 == 0:
            a_in = _pad_cols(attn_w_in[j], NSA_N).astype(BF16)
            a_out = attn_w_out[j].astype(BF16)
            cmp_w = _compress_weights(cmp_pos[j], cmp_w1[j], cmp_w2[j])
            xp, ckv, skv, wkv = _nsa_prompt_layer(xp, g[1], mp[1], a_in, a_out, cmp_w, tables_p, bp, tp)
            xs, ckv_s, skv_s, wkv_s = _nsa_sample_layer(xs, g[1], ms[1], a_in, a_out, cmp_w, tables_s,
                                                        cache_cmp[j], cache_sel[j], cache_win[j], page_table, bs, ts)
            for d, vals in ((outs_p, (ckv, skv, wkv)), (outs_s, (ckv_s, skv_s, wkv_s))):
                for key, val in zip(('cmp', 'sel', 'win'), vals):
                    d[key].append(val)
        else:
            w = _ml_weights(ml_w_in[j], ml_b_gate[j], ml_conv_w[j], ml_conv_b[j], ml_wq[j], ml_wk[j], ml_wv[j],
                            ml_skip[j], ml_gn_g[j], ml_w_out[j])
            xp, c_p, n_p, m_p, buf_p = _ml_prompt_layer(xp, g[1], mp[1], w, bp, tp)
            xs, c_s, n_s, m_s, buf_s = _ml_sample_layer(xs, g[1], ms[1], w, state_conv[j], state_C[j], state_n[j],
                                                        state_m[j], bs, ts)
            for d, vals in ((outs_p, (c_p, n_p, m_p, buf_p)), (outs_s, (c_s, n_s, m_s, buf_s))):
                for key, val in zip(('c', 'n', 'm', 'conv'), vals):
                    d[key].append(val)
        xp = _ffn(xp, g[2], mp[2], w_in[1], w_out[1], TM_PROMPT, tiles)
        xs = _ffn(xs, g[2], ms[2], w_in[1], w_out[1], rows_s)

    fg = final_norm_g.reshape(1, D_MODEL)
    y_prompt = _final_norm(xp, fg, TM_PROMPT).reshape(bp, tp, D_MODEL)
    y_sample = _final_norm(xs, fg, rows_s).reshape(bs, ts, D_MODEL)
    st = lambda d, key: jnp.stack(d[key])
    return (y_prompt, y_sample,
            st(outs_p, 'cmp'), st(outs_p, 'sel'), st(outs_p, 'win'),
            st(outs_p, 'c'), st(outs_p, 'n'), st(outs_p, 'm'), st(outs_p, 'conv'),
            st(outs_s, 'cmp'), st(outs_s, 'sel'), st(outs_s, 'win'),
            st(outs_s, 'c'), st(outs_s, 'n'), st(outs_s, 'm'), st(outs_s, 'conv'))
```

```python
import functools
import math

import numpy as np
import jax
import jax.numpy as jnp
from jax import lax
from jax.experimental import pallas as pl
from jax.experimental.pallas import tpu as pltpu

D_MODEL = 1024
D_FF = 2816
N_HEADS = 16
HEAD_DIM = 64
KV_HEADS = 4
GROUP = 4
ROT_DIM = 16
ROPE_THETA = 500000.0
CMP_BLOCK = 32
CMP_STRIDE = 16
CMP_HIDDEN = 128
SEL_BLOCK = 64
SEL_TOPK = 16
WINDOW = 512
PAGE_SIZE = 128
KV_LANES = 2 * KV_HEADS * HEAD_DIM
NSA_Q = N_HEADS * HEAD_DIM
NSA_N = 2688
ML_INNER = 2048
ML_HEADS = 4
ML_HEAD_DIM = 512
ML_N = 4224
CONV_W = 4
NORM_EPS = 1e-6
NEG = -1e30
BIG = 1e30
SCALE = HEAD_DIM ** -0.5

F32 = jnp.float32
BF16 = jnp.bfloat16
LANE = 128
VMEM_LIMIT_BYTES = 56 * 1024 * 1024


def _params(*sem):
    return pltpu.CompilerParams(dimension_semantics=sem, vmem_limit_bytes=VMEM_LIMIT_BYTES)


def _resident(shape):
    nd = len(shape)
    return pl.BlockSpec(shape, lambda *_: (0,) * nd, pipeline_mode=pl.Buffered(1))


def _dot(a, b):
    return jnp.dot(a, b, preferred_element_type=F32)


def _dot_nt(a, b):
    return lax.dot_general(a, b, (((1,), (1,)), ((), ())), preferred_element_type=F32)


def _dot_tn(a, b):
    return lax.dot_general(a, b, (((0,), (0,)), ((), ())), preferred_element_type=F32)


def _dot_exact(a, b):
    return jnp.dot(a, b, preferred_element_type=F32, precision=lax.Precision.HIGHEST)


def _dot_nt_exact(a, b):
    return lax.dot_general(a, b, (((1,), (1,)), ((), ())), preferred_element_type=F32,
                           precision=lax.Precision.HIGHEST)


def _modulated_norm(x, g, shift, scale):
    y = x * lax.rsqrt(jnp.mean(x * x, axis=-1, keepdims=True) + NORM_EPS)
    return (y * g) * (1.0 + scale) + shift


def _mod_spec(mod, tm, tiles_per_seq):
    if mod.ndim == 4:
        return pl.BlockSpec((None, 3, 1, D_MODEL), lambda i: (i // tiles_per_seq, 0, 0, 0))
    return pl.BlockSpec((3, tm, D_MODEL), lambda i: (0, i, 0))


def _adaln_kernel(c_ref, w_ref, b_ref, o_ref):
    a = jax.nn.silu(c_ref[...]).astype(BF16)
    o_ref[0] = _dot(a, w_ref[0].astype(BF16)) + b_ref[0]


def _adaln(c_all, ada_w, ada_b):
    depth, _, n = ada_w.shape
    rows = c_all.shape[0]
    tn = 2304
    return pl.pallas_call(
        _adaln_kernel,
        grid=(depth, n // tn),
        in_specs=[pl.BlockSpec((rows, D_MODEL), lambda l, j: (0, 0)),
                  pl.BlockSpec((1, D_MODEL, tn), lambda l, j: (l, 0, j)),
                  pl.BlockSpec((1, 1, tn), lambda l, j: (l, 0, j))],
        out_specs=pl.BlockSpec((1, rows, tn), lambda l, j: (l, 0, j)),
        out_shape=jax.ShapeDtypeStruct((depth, rows, n), F32),
        compiler_params=_params("arbitrary", "arbitrary"),
        name="adaln",
    )(c_all, ada_w, ada_b.reshape(depth, 1, n))


FF_CHUNK = 1408


def _ffn_kernel(x_ref, g_ref, mod_ref, win_ref, wout_ref, o_ref):
    x = x_ref[...]
    h = _modulated_norm(x, g_ref[...], mod_ref[0], mod_ref[1]).astype(BF16)
    acc = jnp.zeros(x.shape, F32)
    for c in range(D_FF // FF_CHUNK):
        lo = c * FF_CHUNK
        gate = _dot(h, win_ref[:, lo:lo + FF_CHUNK])
        up = _dot(h, win_ref[:, D_FF + lo:D_FF + lo + FF_CHUNK])
        act = (jax.nn.silu(gate) * up).astype(BF16)
        acc = acc + _dot(act, wout_ref[lo:lo + FF_CHUNK, :])
    o_ref[...] = x + 0.5 * (1.0 + mod_ref[2]) * acc


def _ffn(x, g, mod, w_in, w_out, tm, tiles_per_seq=1):
    rows = x.shape[0]
    return pl.pallas_call(
        _ffn_kernel,
        grid=(rows // tm,),
        in_specs=[pl.BlockSpec((tm, D_MODEL), lambda i: (i, 0)),
                  _resident((1, D_MODEL)),
                  _mod_spec(mod, tm, tiles_per_seq),
                  _resident((D_MODEL, 2 * D_FF)),
                  _resident((D_FF, D_MODEL))],
        out_specs=pl.BlockSpec((tm, D_MODEL), lambda i: (i, 0)),
        out_shape=jax.ShapeDtypeStruct((rows, D_MODEL), F32),
        compiler_params=_params("arbitrary"),
        name="ffn_half",
    )(x, g, mod, w_in, w_out)


def _final_norm_kernel(x_ref, g_ref, o_ref):
    x = x_ref[...]
    o_ref[...] = x * lax.rsqrt(jnp.mean(x * x, axis=-1, keepdims=True) + NORM_EPS) * g_ref[...]


def _final_norm(x, g, tm):
    rows = x.shape[0]
    return pl.pallas_call(
        _final_norm_kernel,
        grid=(rows // tm,),
        in_specs=[pl.BlockSpec((tm, D_MODEL), lambda i: (i, 0)), _resident((1, D_MODEL))],
        out_specs=pl.BlockSpec((tm, D_MODEL), lambda i: (i, 0)),
        out_shape=jax.ShapeDtypeStruct((rows, D_MODEL), F32),
        compiler_params=_params("arbitrary"),
        name="final_norm",
    )(x, g)


def _rope_tables(pos):
    half = ROT_DIM // 2
    inv = ROPE_THETA ** (-jnp.arange(0, ROT_DIM, 2, dtype=F32) / ROT_DIM)
    ang = pos.astype(F32)[:, None] * inv[None, :]
    cos, sin = jnp.cos(ang), jnp.sin(ang)
    n = pos.shape[0]
    pad = jnp.zeros((n, HEAD_DIM - ROT_DIM), F32)
    zeros = jnp.zeros((n, half), F32)
    c = jnp.concatenate([cos, cos, pad + 1.0], axis=1)
    s1 = jnp.concatenate([-sin, zeros, pad], axis=1)
    s2 = jnp.concatenate([zeros, sin, pad], axis=1)
    return tuple(jnp.tile(t, (1, LANE // HEAD_DIM)) for t in (c, s1, s2))


def _nsa_proj_kernel(x_ref, g_ref, mod_ref, w_ref, c_ref, s1_ref, s2_ref,
                     q_ref, cmp_ref, sel_ref, win_ref, gate_ref):
    h = _modulated_norm(x_ref[...], g_ref[...], mod_ref[0], mod_ref[1]).astype(BF16)
    proj = _dot(h, w_ref[...])
    c, s1, s2 = c_ref[...], s1_ref[...], s2_ref[...]

    def rope(v):
        return v * c + pltpu.roll(v, LANE - ROT_DIM // 2, 1) * s1 + pltpu.roll(v, ROT_DIM // 2, 1) * s2

    for j in range(NSA_Q // LANE):
        q_ref[:, j * LANE:(j + 1) * LANE] = rope(proj[:, j * LANE:(j + 1) * LANE])
    half = KV_LANES // 2
    for br, ref in enumerate((cmp_ref, sel_ref, win_ref)):
        base = NSA_Q + br * KV_LANES
        for j in range(half // LANE):
            ref[:, j * LANE:(j + 1) * LANE] = rope(proj[:, base + j * LANE:base + (j + 1) * LANE])
        ref[:, half:] = proj[:, base + half:base + KV_LANES]
    gate_ref[...] = jax.nn.sigmoid(proj[:, NSA_Q + 3 * KV_LANES:])


def _nsa_proj(x, g, mod, w, tables, tm, tiles_per_seq=1):
    rows = x.shape[0]
    trows = tables[0].shape[0]
    tspec = pl.BlockSpec((tm, LANE), lambda i: (i % (trows // tm), 0))
    row = lambda n: pl.BlockSpec((tm, n), lambda i: (i, 0))
    return pl.pallas_call(
        _nsa_proj_kernel,
        grid=(rows // tm,),
        in_specs=[row(D_MODEL), _resident((1, D_MODEL)), _mod_spec(mod, tm, tiles_per_seq),
                  _resident((D_MODEL, NSA_N)), tspec, tspec, tspec],
        out_specs=[row(NSA_Q), row(KV_LANES), row(KV_LANES), row(KV_LANES), row(LANE)],
        out_shape=[jax.ShapeDtypeStruct((rows, n), F32) for n in (NSA_Q, KV_LANES, KV_LANES, KV_LANES, LANE)],
        compiler_params=_params("arbitrary"),
        name="nsa_proj",
    )(x, g, mod, w, *tables)


def _compress_finish(seg_proj, pe_ref, w1_ref, w2_ref, out_ref, nseg):
    row = lax.broadcasted_iota(jnp.int32, (nseg, 1), 0)
    for s in range(2):
        bias = jnp.zeros((8, 2 * CMP_HIDDEN), F32)
        lane = lax.broadcasted_iota(jnp.int32, (8, 2 * CMP_HIDDEN), 1)
        for l in range(CMP_STRIDE):
            pa = jnp.broadcast_to(pe_ref[s, l:l + 1, :], (8, HEAD_DIM)).astype(BF16)
            pb = jnp.broadcast_to(pe_ref[s, CMP_STRIDE + l:CMP_STRIDE + l + 1, :], (8, HEAD_DIM)).astype(BF16)
            bias = bias + jnp.where(lane < CMP_HIDDEN, _dot(pa, w1_ref[s, l]), _dot(pb, w1_ref[s, l]))
        bias = bias[0:1]
        for g in range(KV_HEADS):
            a = seg_proj(s, g) + bias
            first, second = a[:, :CMP_HIDDEN], a[:, CMP_HIDDEN:]
            hid = jax.nn.gelu(first + pltpu.roll(second, nseg - 1, 0))
            o = _dot(hid.astype(BF16), w2_ref[s])
            o = jnp.where(row < nseg - 1, o, 0.0)
            out_ref[s, :, g * HEAD_DIM:(g + 1) * HEAD_DIM] = o.astype(out_ref.dtype)


def _compress_prompt_kernel(x_ref, pe_ref, w1_ref, w2_ref, out_ref, acc_ref, *, nseg):
    for l in range(CMP_STRIDE):
        xl = x_ref[0, :, l * KV_LANES:(l + 1) * KV_LANES].astype(BF16)
        for sg in range(2 * KV_HEADS):
            d = _dot(xl[:, sg * HEAD_DIM:(sg + 1) * HEAD_DIM], w1_ref[sg // KV_HEADS, l])
            if l == 0:
                acc_ref[sg] = d
            else:
                acc_ref[sg] += d
    _compress_finish(lambda s, g: acc_ref[s * KV_HEADS + g], pe_ref, w1_ref, w2_ref, out_ref.at[0], nseg)


def _compress_weights(pos_emb, w1, w2):
    pe = jnp.transpose(pos_emb, (1, 0, 2))
    w1c = jnp.concatenate([w1[:, :CMP_STRIDE], w1[:, CMP_STRIDE:]], axis=-1).astype(BF16)
    zero = jnp.zeros_like(w1c)
    w1bd = jnp.concatenate([jnp.concatenate([w1c, zero], axis=-1), jnp.concatenate([zero, w1c], axis=-1)], axis=-2)
    w1bd = w1bd.reshape(2, CMP_STRIDE // 2, 4 * HEAD_DIM, 4 * CMP_HIDDEN)
    return pe, w1c, w1bd, w2.astype(BF16)


def _compress_prompt(cmp_kv, pe, w1c, w2b):
    b, nseg, seg_lanes = cmp_kv.shape
    return pl.pallas_call(
        functools.partial(_compress_prompt_kernel, nseg=nseg),
        grid=(b,),
        in_specs=[pl.BlockSpec((1, nseg, seg_lanes), lambda i: (i, 0, 0)),
                  _resident(pe.shape), _resident(w1c.shape), _resident(w2b.shape)],
        out_specs=pl.BlockSpec((1, 2, nseg, KV_LANES // 2), lambda i: (i, 0, 0, 0)),
        out_shape=jax.ShapeDtypeStruct((b, 2, nseg, KV_LANES // 2), BF16),
        scratch_shapes=[pltpu.VMEM((2 * KV_HEADS, nseg, 2 * CMP_HIDDEN), F32)],
        compiler_params=_params("arbitrary"),
        name="compress_prompt",
    )(cmp_kv, pe, w1c, w2b)


PAGES_PER_STEP = 8
SEGS_PER_PAGE = PAGE_SIZE // CMP_STRIDE


def _page_spec(layer, k):
    return pl.BlockSpec((1, 1, 2, KV_HEADS, HEAD_DIM, PAGE_SIZE),
                        lambda i, s, pt: (layer, pt[i, s * PAGES_PER_STEP + k], 0, 0, 0, 0))


def _compress_paged_kernel(pt_ref, *refs, nseg, n_steps):
    del pt_ref
    pages = refs[:PAGES_PER_STEP]
    pe_ref, w1_ref, w1bd_ref, w2_ref, out_ref, rows_ref, acc_ref = refs[PAGES_PER_STEP:]
    step = pl.program_id(1)
    pair = 2 * HEAD_DIM
    for k, page in enumerate(pages):
        r0 = pl.multiple_of((step * PAGES_PER_STEP + k) * PAGE_SIZE, PAGE_SIZE)
        for s in range(2):
            for gp in range(KV_HEADS // 2):
                tile = page[0, 0, s, 2 * gp:2 * gp + 2].reshape(pair, PAGE_SIZE)
                rows_ref[s * 2 + gp, pl.ds(r0, PAGE_SIZE), :] = tile.T

    @pl.when(step == n_steps - 1)
    def _():
        for sgp in range(KV_HEADS):
            for l in range(0, CMP_STRIDE, 2):
                xl = jnp.concatenate([rows_ref[sgp, pl.ds(l + i, nseg, stride=CMP_STRIDE), :] for i in range(2)],
                                     axis=1).astype(BF16)
                d = _dot(xl, w1bd_ref[sgp // 2, l // 2])
                if l == 0:
                    acc_ref[sgp] = d
                else:
                    acc_ref[sgp] += d
        width = 2 * CMP_HIDDEN

        def seg_proj(s, g):
            return acc_ref[s * 2 + g // 2, :, (g % 2) * width:(g % 2 + 1) * width]

        _compress_finish(seg_proj, pe_ref, w1_ref, w2_ref, out_ref.at[0], nseg)


def _compress_paged(cache_t, layer, page_table, pe, w1c, w1bd, w2b):
    b, n_pages = page_table.shape
    n_steps = n_pages // PAGES_PER_STEP
    nseg = n_pages * SEGS_PER_PAGE
    const = lambda a: pl.BlockSpec(a.shape, lambda i, s, pt: (0,) * a.ndim)
    grid_spec = pltpu.PrefetchScalarGridSpec(
        num_scalar_prefetch=1,
        grid=(b, n_steps),
        in_specs=[_page_spec(layer, k) for k in range(PAGES_PER_STEP)] + [const(pe), const(w1c), const(w1bd), const(w2b)],
        out_specs=pl.BlockSpec((1, 2, nseg, KV_LANES // 2), lambda i, s, pt: (i, 0, 0, 0)),
        scratch_shapes=[pltpu.VMEM((KV_HEADS, n_pages * PAGE_SIZE, 2 * HEAD_DIM), F32),
                        pltpu.VMEM((KV_HEADS, nseg, 4 * CMP_HIDDEN), F32)],
    )
    return pl.pallas_call(
        functools.partial(_compress_paged_kernel, nseg=nseg, n_steps=n_steps),
        grid_spec=grid_spec,
        out_shape=jax.ShapeDtypeStruct((b, 2, nseg, KV_LANES // 2), BF16),
        compiler_params=_params("arbitrary", "arbitrary"),
        name="compress_paged",
    )(page_table, *([cache_t] * PAGES_PER_STEP), pe, w1c, w1bd, w2b)


def _importance_matrix(nc_pad, nc, n_sel_pad, n_sel):
    i = np.arange(nc_pad)[:, None]
    j = np.arange(n_sel_pad)[None, :]
    overlap = np.maximum(np.minimum(i * CMP_STRIDE + CMP_BLOCK, (j + 1) * SEL_BLOCK)
                         - np.maximum(i * CMP_STRIDE, j * SEL_BLOCK), 0)
    m = overlap / CMP_STRIDE * (i < nc) * (j < n_sel)
    return jnp.asarray(m, dtype=BF16)


def _topk_mask(score, k, axis):
    n = score.shape[axis]
    idx = lax.broadcasted_iota(jnp.int32, score.shape, axis).astype(F32)

    def body(_, carry):
        sc, sel = carry
        m = jnp.max(sc, axis=axis, keepdims=True)
        first = jnp.min(jnp.where(sc == m, idx, float(n)), axis=axis, keepdims=True)
        hit = idx == first
        return jnp.where(hit, -jnp.inf, sc), jnp.where(hit, 1.0, sel)

    _, sel = lax.fori_loop(0, k, body, (score, jnp.zeros(score.shape, F32)))
    return sel


def _block_scores(imp, qpos, n_valid, axis):
    jj = lax.broadcasted_iota(jnp.int32, imp.shape, axis)
    cur = qpos // SEL_BLOCK
    forced = (jj == 0) | (jj == cur) | (jj == cur - 1)
    score = jnp.where(forced, BIG, jnp.where(jj <= cur, imp, NEG))
    return jnp.where(jj < n_valid, score, -jnp.inf)


def _cmp_select_kernel(q_ref, kc_ref, vct_ref, impt_ref, gate_ref, o_ref, qa_ref, *, tq, nc, n_sel):
    q0 = pl.program_id(1) * tq
    rows = GROUP * tq
    qpos = q0 + lax.broadcasted_iota(jnp.int32, (1, rows), 1) % tq
    cidx = lax.broadcasted_iota(jnp.int32, (kc_ref.shape[2], 1), 0)
    valid = (cidx * CMP_STRIDE + CMP_BLOCK - 1 <= qpos) & (cidx < nc)
    tpos = q0 + lax.broadcasted_iota(jnp.int32, (1, tq), 1)
    gates_t = gate_ref[...].T
    o_t, bias_t = [], []
    for g in range(KV_HEADS):
        qg = q_ref[0, g * GROUP:(g + 1) * GROUP].reshape(rows, HEAD_DIM)
        s = jnp.where(valid, _dot_nt(kc_ref[0, g], qg) * SCALE, NEG)
        e = jnp.exp(s - jnp.max(s, axis=0, keepdims=True))
        p = jnp.where(valid, e / jnp.sum(e, axis=0, keepdims=True), 0.0)
        o = _dot(vct_ref[0, g], p.astype(BF16))
        p_grp = p[:, 0:tq]
        for r in range(1, GROUP):
            p_grp = p_grp + p[:, r * tq:(r + 1) * tq]
        imp = _dot(impt_ref[...], p_grp.astype(BF16))
        sel = _topk_mask(_block_scores(imp, tpos, n_sel, 0), min(SEL_TOPK, n_sel), 0)
        bias_t.append(jnp.where(sel > 0.0, 0.0, NEG))
        for r in range(GROUP):
            h = g * GROUP + r
            o_t.append(o[:, r * tq:(r + 1) * tq] * gates_t[h:h + 1, :])
    o_ref[0] = jnp.concatenate(o_t, axis=0).T
    bias = jnp.concatenate(bias_t, axis=0).T.astype(BF16)
    for h in range(N_HEADS):
        g = h // GROUP
        qa_ref[0, h, :, 0:HEAD_DIM] = (q_ref[0, h].astype(F32) * SCALE).astype(BF16)
        qa_ref[0, h, :, HEAD_DIM:] = bias[:, g * HEAD_DIM:(g + 1) * HEAD_DIM]


def _cmp_select(qh, kch, vct, gates, tq):
    b, _, t, _ = qh.shape
    nseg = kch.shape[2]
    n_sel = t // SEL_BLOCK
    assert n_sel <= HEAD_DIM
    imp_t = _importance_matrix(nseg, nseg - 1, HEAD_DIM, n_sel).T
    return pl.pallas_call(
        functools.partial(_cmp_select_kernel, tq=tq, nc=nseg - 1, n_sel=n_sel),
        grid=(b, t // tq),
        in_specs=[pl.BlockSpec((1, N_HEADS, tq, HEAD_DIM), lambda i, j: (i, 0, j, 0)),
                  pl.BlockSpec((1, KV_HEADS, nseg, HEAD_DIM), lambda i, j: (i, 0, 0, 0)),
                  pl.BlockSpec((1, KV_HEADS, HEAD_DIM, nseg), lambda i, j: (i, 0, 0, 0)),
                  pl.BlockSpec((HEAD_DIM, nseg), lambda i, j: (0, 0)),
                  pl.BlockSpec((tq, LANE), lambda i, j: (i * (t // tq) + j, 0))],
        out_specs=[pl.BlockSpec((1, tq, NSA_Q), lambda i, j: (i, j, 0)),
                   pl.BlockSpec((1, N_HEADS, tq, 2 * HEAD_DIM), lambda i, j: (i, 0, j, 0))],
        out_shape=[jax.ShapeDtypeStruct((b, t, NSA_Q), F32),
                   jax.ShapeDtypeStruct((b, N_HEADS, t, 2 * HEAD_DIM), BF16)],
        compiler_params=_params("arbitrary", "arbitrary"),
        name="cmp_select",
    )(qh, kch, vct, imp_t, gates)


def _flash_kernel(qa_ref, ks_ref, vst_ref, kw_ref, vwt_ref, oc_ref, gate_ref, o_ref, *, tq, tk):
    g = pl.program_id(1)
    q0 = pl.program_id(2) * tq
    qpos = q0 + lax.broadcasted_iota(jnp.int32, (1, tq), 1)
    heads = [qa_ref[0, r] for r in range(GROUP)]

    def softmax_step(m, l, acc, s, v_t):
        m_new = jnp.maximum(m, jnp.max(s, axis=0, keepdims=True))
        a = jnp.exp(m - m_new)
        p = jnp.exp(s - m_new)
        return m_new, a * l + jnp.sum(p, axis=0, keepdims=True), a * acc + _dot(v_t, p.astype(BF16))

    def tile_pair(j, carry, causal):
        carry = list(carry)
        k0 = pl.multiple_of(j * (2 * tk), 2 * tk)
        starts = [k0, k0 + tk]
        scores = [[_dot_nt(ks_ref[0, 0, pl.ds(st, tk), :], heads[r]) for r in range(GROUP)] for st in starts]
        for st, s_tile in zip(starts, scores):
            v_t = vst_ref[0, 0, :, pl.ds(st, tk)]
            for r in range(GROUP):
                s = s_tile[r]
                if causal:
                    s = jnp.where(st + lax.broadcasted_iota(jnp.int32, (tk, 1), 0) <= qpos, s, NEG)
                carry[r] = softmax_step(*carry[r], s, v_t)
        return tuple(carry)

    n_full = q0 // (2 * tk)
    init = tuple((jnp.full((1, tq), -jnp.inf, F32), jnp.zeros((1, tq), F32), jnp.zeros((HEAD_DIM, tq), F32))
                 for _ in range(GROUP))
    carry = lax.fori_loop(0, n_full, lambda j, c: tile_pair(j, c, False), init)
    sel = tile_pair(n_full, carry, True)

    wk = WINDOW + tq
    w0 = pl.multiple_of(jnp.maximum(q0 - WINDOW, 0), tq)
    kw_tile = kw_ref[0, 0, pl.ds(w0, wk), :]
    vw_t = vwt_ref[0, 0, :, pl.ds(w0, wk)]
    d = qpos - (w0 + lax.broadcasted_iota(jnp.int32, (wk, 1), 0))
    in_window = (d >= 0) & (d < WINDOW)

    gates_t = gate_ref[...].T
    sub = lax.broadcasted_iota(jnp.int32, gates_t.shape, 0)
    parts = []
    win_scores = [_dot_nt(kw_tile, heads[r]) for r in range(GROUP)]
    for r in range(GROUP):
        _, l, acc = sel[r]
        s = jnp.where(in_window, win_scores[r], NEG)
        p = jnp.exp(s - jnp.max(s, axis=0, keepdims=True))
        o_win = _dot(vw_t, p.astype(BF16)) / jnp.sum(p, axis=0, keepdims=True)
        g_sel = jnp.sum(jnp.where(sub == N_HEADS + g * GROUP + r, gates_t, 0.0), axis=0, keepdims=True)
        g_win = jnp.sum(jnp.where(sub == 2 * N_HEADS + g * GROUP + r, gates_t, 0.0), axis=0, keepdims=True)
        parts.append(g_sel * (acc / l) + g_win * o_win)
    o = oc_ref[0] + jnp.concatenate(parts, axis=0).T
    o_ref[0] = o.astype(o_ref.dtype)


def _flash(qa, ks, vst, kw, vwt, oc, gates, tq, tk):
    b, _, t, _ = qa.shape
    grp_lanes = GROUP * HEAD_DIM
    keys = pl.BlockSpec((1, 1, t, 2 * HEAD_DIM), lambda i, g, j: (i, g, 0, 0))
    vals = pl.BlockSpec((1, 1, HEAD_DIM, t), lambda i, g, j: (i, g, 0, 0))
    return pl.pallas_call(
        functools.partial(_flash_kernel, tq=tq, tk=tk),
        grid=(b, KV_HEADS, t // tq),
        in_specs=[pl.BlockSpec((1, GROUP, tq, 2 * HEAD_DIM), lambda i, g, j: (i, g, j, 0)),
                  keys, vals, keys, vals,
                  pl.BlockSpec((1, tq, grp_lanes), lambda i, g, j: (i, j, g)),
                  pl.BlockSpec((tq, LANE), lambda i, g, j: (i * (t // tq) + j, 0))],
        out_specs=pl.BlockSpec((1, tq, grp_lanes), lambda i, g, j: (i, j, g)),
        out_shape=jax.ShapeDtypeStruct((b, t, NSA_Q), BF16),
        compiler_params=_params("arbitrary", "arbitrary", "arbitrary"),
        name="nsa_flash",
    )(qa, ks, vst, kw, vwt, oc, gates)


def _out_proj_kernel(x_ref, o_ref, w_ref, mod_ref, y_ref):
    y = _dot(o_ref[...].astype(BF16), w_ref[...])
    y_ref[...] = x_ref[...] + (1.0 + mod_ref[2]) * y


def _out_proj(x, o, w, mod, tm, tiles_per_seq=1):
    rows, k = o.shape
    return pl.pallas_call(
        _out_proj_kernel,
        grid=(rows // tm,),
        in_specs=[pl.BlockSpec((tm, D_MODEL), lambda i: (i, 0)),
                  pl.BlockSpec((tm, k), lambda i: (i, 0)),
                  _resident((k, D_MODEL)),
                  _mod_spec(mod, tm, tiles_per_seq)],
        out_specs=pl.BlockSpec((tm, D_MODEL), lambda i: (i, 0)),
        out_shape=jax.ShapeDtypeStruct((rows, D_MODEL), F32),
        compiler_params=_params("arbitrary"),
        name="out_proj",
    )(x, o, w, mod)


def _diag_fold(acc):
    rows = acc.shape[0]
    t = rows // (GROUP * KV_HEADS)
    row_g = (lax.broadcasted_iota(jnp.int32, (rows, 1), 0) // t) % KV_HEADS
    out = jnp.zeros((rows, HEAD_DIM), F32)
    for g in range(KV_HEADS):
        out = out + jnp.where(row_g == g, acc[:, g * HEAD_DIM:(g + 1) * HEAD_DIM], 0.0)
    return out


def _dec_cmp_select_kernel(q_ref, kc_ref, vc_ref, imp_ref, o_ref, bias_ref, *, t, nc, n_sel, past):
    rows = q_ref.shape[1]
    qt = KV_HEADS * t
    qpos = past + lax.broadcasted_iota(jnp.int32, (rows, 1), 0) % t
    cidx = lax.broadcasted_iota(jnp.int32, (1, kc_ref.shape[1]), 1)
    valid = (cidx * CMP_STRIDE + CMP_BLOCK - 1 <= qpos) & (cidx < nc)
    s = jnp.where(valid, _dot_nt(q_ref[0], kc_ref[0]), NEG)
    e = jnp.exp(s - jnp.max(s, axis=-1, keepdims=True))
    p = jnp.where(valid, e / jnp.sum(e, axis=-1, keepdims=True), 0.0)
    o_ref[0] = _diag_fold(_dot(p.astype(BF16), vc_ref[0]))
    p_grp = p[0:qt]
    for r in range(1, GROUP):
        p_grp = p_grp + p[r * qt:(r + 1) * qt]
    imp = _dot(p_grp.astype(BF16), imp_ref[...])
    sel = _topk_mask(_block_scores(imp, qpos[0:qt], n_sel, 1), min(SEL_TOPK, n_sel), 1)
    bias = jnp.where(sel > 0.0, 0.0, NEG).astype(BF16)
    for r in range(GROUP):
        bias_ref[0, r * qt:(r + 1) * qt, :] = bias


def _dec_cmp_select(qbd, kc, vc, t, past):
    b, rows, lanes = qbd.shape
    nseg = kc.shape[1]
    n_sel = -(-(past + t) // SEL_BLOCK)
    n_sel_pad = -(-n_sel // LANE) * LANE
    imp_m = _importance_matrix(nseg, nseg - 1, n_sel_pad, n_sel)
    return pl.pallas_call(
        functools.partial(_dec_cmp_select_kernel, t=t, nc=nseg - 1, n_sel=n_sel, past=past),
        grid=(b,),
        in_specs=[pl.BlockSpec((1, rows, lanes), lambda i: (i, 0, 0)),
                  pl.BlockSpec((1, nseg, lanes), lambda i: (i, 0, 0)),
                  pl.BlockSpec((1, nseg, lanes), lambda i: (i, 0, 0)),
                  pl.BlockSpec((nseg, n_sel_pad), lambda i: (0, 0))],
        out_specs=[pl.BlockSpec((1, rows, HEAD_DIM), lambda i: (i, 0, 0)),
                   pl.BlockSpec((1, rows, n_sel_pad), lambda i: (i, 0, 0))],
        out_shape=[jax.ShapeDtypeStruct((b, rows, HEAD_DIM), F32),
                   jax.ShapeDtypeStruct((b, rows, n_sel_pad), BF16)],
        compiler_params=_params("arbitrary"),
        name="dec_cmp_select",
    )(qbd, kc, vc, imp_m)


def _dec_attend_kernel(pt_ref, *refs, t, n_steps, past):
    del pt_ref
    pages = refs[:PAGES_PER_STEP]
    (q_ref, bias_ref, e_ref, tail_ref, wbuf_ref, wtail_ref,
     osel_ref, owin_ref, m_ref, l_ref, acc_ref) = refs[PAGES_PER_STEP:]
    step = pl.program_id(1)
    half = KV_LANES // 2
    q = q_ref[0]
    rows = q.shape[0]
    tq = lax.broadcasted_iota(jnp.int32, (rows, 1), 0) % t

    @pl.when(step == 0)
    def _():
        m_ref[...] = jnp.full(m_ref.shape, -jnp.inf, F32)
        l_ref[...] = jnp.zeros(l_ref.shape, F32)
        acc_ref[...] = jnp.zeros(acc_ref.shape, F32)

    def update(s, weighted_values):
        m = m_ref[...]
        m_new = jnp.maximum(m, jnp.max(s, axis=-1, keepdims=True))
        a = jnp.exp(m - m_new)
        p = jnp.exp(s - m_new)
        l_ref[...] = a * l_ref[...] + jnp.sum(p, axis=-1, keepdims=True)
        acc_ref[...] = a * acc_ref[...] + weighted_values(p.astype(BF16))
        m_ref[...] = m_new

    bias = bias_ref[0]
    k_t = jnp.concatenate([page[0, 0, 0].reshape(half, PAGE_SIZE) for page in pages], axis=1).astype(BF16)
    v_t = jnp.concatenate([page[0, 0, 1].reshape(half, PAGE_SIZE) for page in pages], axis=1).astype(BF16)
    update(_dot(q, k_t) + _dot_nt(bias, e_ref[...]), lambda p: _dot_nt(p, v_t))

    @pl.when(step == n_steps - 1)
    def _():
        tail = tail_ref[0]
        tpos = lax.broadcasted_iota(jnp.int32, (1, tail.shape[0]), 1)
        blk = past // SEL_BLOCK
        lane = lax.broadcasted_iota(jnp.int32, bias.shape, 1)
        b_tail = jnp.sum(jnp.where(lane == blk, bias.astype(F32), 0.0), axis=-1, keepdims=True)
        s = _dot_nt(q, tail[:, :half].astype(BF16)) + b_tail
        update(jnp.where(tpos <= tq, s, NEG), lambda p: _dot(p, tail[:, half:].astype(BF16)))
        osel_ref[0] = _diag_fold(acc_ref[...] / l_ref[...])

        wb = wbuf_ref.shape[-1]
        wk_t = wbuf_ref[0, 0, 0].reshape(half, wb).astype(BF16)
        wv_t = wbuf_ref[0, 0, 1].reshape(half, wb).astype(BF16)
        wtail = wtail_ref[0]
        ipos = lax.broadcasted_iota(jnp.int32, (1, wb), 1)
        d = (past + tq) - (past - wb + ipos)
        s1 = jnp.where((d >= 0) & (d < WINDOW), _dot(q, wk_t), NEG)
        s2 = jnp.where(tpos <= tq, _dot_nt(q, wtail[:, :half].astype(BF16)), NEG)
        m = jnp.maximum(jnp.max(s1, axis=-1, keepdims=True), jnp.max(s2, axis=-1, keepdims=True))
        p1, p2 = jnp.exp(s1 - m), jnp.exp(s2 - m)
        den = jnp.sum(p1, axis=-1, keepdims=True) + jnp.sum(p2, axis=-1, keepdims=True)
        o = _dot_nt(p1.astype(BF16), wv_t) + _dot(p2.astype(BF16), wtail[:, half:].astype(BF16))
        owin_ref[0] = _diag_fold(o / den)


def _dec_attend(cache_t, win_t, layer, page_table, qbd, bias, onehot, sel_tail, win_tail, t, past):
    b, n_pages = page_table.shape
    n_steps = n_pages // PAGES_PER_STEP
    rows, lanes = qbd.shape[1:]
    n_sel_pad = bias.shape[2]
    keys_per_step = PAGES_PER_STEP * PAGE_SIZE
    per_seq = lambda arr: pl.BlockSpec((1,) + arr.shape[1:], lambda i, s, pt: (i,) + (0,) * (arr.ndim - 1))
    grid_spec = pltpu.PrefetchScalarGridSpec(
        num_scalar_prefetch=1,
        grid=(b, n_steps),
        in_specs=[_page_spec(layer, k) for k in range(PAGES_PER_STEP)]
        + [per_seq(qbd), per_seq(bias),
           pl.BlockSpec((keys_per_step, n_sel_pad), lambda i, s, pt: (s, 0)),
           per_seq(sel_tail),
           pl.BlockSpec((1, 1) + win_t.shape[2:], lambda i, s, pt: (layer, i, 0, 0, 0, 0)),
           per_seq(win_tail)],
        out_specs=[pl.BlockSpec((1, rows, HEAD_DIM), lambda i, s, pt: (i, 0, 0)),
                   pl.BlockSpec((1, rows, HEAD_DIM), lambda i, s, pt: (i, 0, 0))],
        scratch_shapes=[pltpu.VMEM((rows, 1), F32), pltpu.VMEM((rows, 1), F32), pltpu.VMEM((rows, lanes), F32)],
    )
    return pl.pallas_call(
        functools.partial(_dec_attend_kernel, t=t, n_steps=n_steps, past=past),
        grid_spec=grid_spec,
        out_shape=[jax.ShapeDtypeStruct((b, rows, HEAD_DIM), F32)] * 2,
        compiler_params=_params("arbitrary", "arbitrary"),
        name="dec_attend",
    )(page_table, *([cache_t] * PAGES_PER_STEP), qbd, bias, onehot, sel_tail, win_t, win_tail)


def _gated_out_proj_kernel(x_ref, oc_ref, os_ref, ow_ref, gate_ref, ex_ref, w_ref, mod_ref, y_ref):
    gates = gate_ref[...]
    o = jnp.zeros(oc_ref.shape, F32)
    for k, ref in enumerate((oc_ref, os_ref, ow_ref)):
        o = o + _dot_exact(gates, ex_ref[k]) * ref[...]
    y_ref[...] = x_ref[...] + (1.0 + mod_ref[2]) * _dot(o.astype(BF16), w_ref[...])


def _gate_expansion():
    ex = np.zeros((3, LANE, NSA_Q), np.float32)
    for k in range(3):
        for h in range(N_HEADS):
            ex[k, k * N_HEADS + h, h * HEAD_DIM:(h + 1) * HEAD_DIM] = 1.0
    return jnp.asarray(ex)


def _gated_out_proj(x, oc, osel, owin, gates, w, mod):
    rows = x.shape[0]
    full = lambda n: pl.BlockSpec((rows, n), lambda i: (0, 0))
    return pl.pallas_call(
        _gated_out_proj_kernel,
        grid=(1,),
        in_specs=[full(D_MODEL), full(NSA_Q), full(NSA_Q), full(NSA_Q), full(LANE),
                  pl.BlockSpec((3, LANE, NSA_Q), lambda i: (0, 0, 0)),
                  pl.BlockSpec((NSA_Q, D_MODEL), lambda i: (0, 0)),
                  pl.BlockSpec((3, rows, D_MODEL), lambda i: (0, 0, 0))],
        out_specs=full(D_MODEL),
        out_shape=jax.ShapeDtypeStruct((rows, D_MODEL), F32),
        compiler_params=_params("arbitrary"),
        name="gated_out_proj",
    )(x, oc, osel, owin, gates, _gate_expansion(), w, mod)


def _ml_proj_kernel(x_ref, g_ref, mod_ref, w_ref, bg_ref, xm_ref, z_ref, gl_ref):
    h = _modulated_norm(x_ref[...], g_ref[...], mod_ref[0], mod_ref[1]).astype(BF16)
    proj = _dot(h, w_ref[...])
    xm_ref[...] = proj[:, :ML_INNER]
    z_ref[...] = proj[:, ML_INNER:2 * ML_INNER]
    gl = proj[:, 2 * ML_INNER:] + bg_ref[...]
    lane = lax.broadcasted_iota(jnp.int32, gl.shape, 1)
    gl_ref[...] = jnp.where(lane < ML_HEADS, gl, jax.nn.log_sigmoid(gl))


def _ml_proj(x, g, mod, w, b_gate, tm, tiles_per_seq=1):
    rows = x.shape[0]
    row = lambda n: pl.BlockSpec((tm, n), lambda i: (i, 0))
    return pl.pallas_call(
        _ml_proj_kernel,
        grid=(rows // tm,),
        in_specs=[row(D_MODEL), _resident((1, D_MODEL)), _mod_spec(mod, tm, tiles_per_seq),
                  _resident((D_MODEL, ML_N)), _resident((1, LANE))],
        out_specs=[row(ML_INNER), row(ML_INNER), row(LANE)],
        out_shape=[jax.ShapeDtypeStruct((rows, n), F32) for n in (ML_INNER, ML_INNER, LANE)],
        compiler_params=_params("arbitrary"),
        name="ml_proj",
    )(x, g, mod, w, b_gate)


def _conv_qkv(taps, cw_ref, cb_ref, wqk_ref, wv_ref, xc_ref, q_ref, k_ref, v_ref):
    xc = cb_ref[...]
    for w in range(CONV_W):
        xc = xc + taps[w] * cw_ref[w:w + 1, :]
    xc = jax.nn.silu(xc)
    xc_ref[...] = xc
    xcb = xc.astype(BF16)
    xmb = taps[CONV_W - 1].astype(BF16)
    for j in range(ML_INNER // LANE):
        sl = slice(j * LANE, (j + 1) * LANE)
        qk = _dot(xcb[:, sl], wqk_ref[j])
        q_ref[:, sl] = qk[:, :LANE].astype(q_ref.dtype)
        k_ref[:, sl] = (qk[:, LANE:] * (ML_HEAD_DIM ** -0.5)).astype(k_ref.dtype)
        v_ref[:, sl] = _dot(xmb[:, sl], wv_ref[j]).astype(v_ref.dtype)


def _ml_conv_prompt_kernel(xm_ref, halo_ref, cw_ref, cb_ref, wqk_ref, wv_ref,
                           xc_ref, q_ref, k_ref, v_ref, pad_ref, *, tm, tiles_per_seq):
    first = pl.program_id(0) % tiles_per_seq == 0
    pad_ref[0:8, :] = jnp.where(first, 0.0, halo_ref[...])
    pad_ref[8:, :] = xm_ref[...]
    taps = [pad_ref[pl.ds(8 - (CONV_W - 1) + w, tm), :] for w in range(CONV_W)]
    _conv_qkv(taps, cw_ref, cb_ref, wqk_ref, wv_ref, xc_ref, q_ref, k_ref, v_ref)


def _ml_conv_sample_kernel(t0_ref, t1_ref, t2_ref, t3_ref, cw_ref, cb_ref, wqk_ref, wv_ref,
                           xc_ref, q_ref, k_ref, v_ref):
    taps = [r[...] for r in (t0_ref, t1_ref, t2_ref, t3_ref)]
    _conv_qkv(taps, cw_ref, cb_ref, wqk_ref, wv_ref, xc_ref, q_ref, k_ref, v_ref)


def _headwise_weights(wq, wk, wv):
    def blockdiag(w):
        n_chunks = ML_INNER // LANE
        per = LANE // 4
        wc = w.reshape(n_chunks, per, 4, 4)
        eye = jnp.eye(per, dtype=w.dtype)
        return jnp.einsum('jncd,nm->jncmd', wc, eye).reshape(n_chunks, LANE, LANE)
    return jnp.concatenate([blockdiag(wq), blockdiag(wk)], axis=-1).astype(BF16), blockdiag(wv).astype(BF16)


def _ml_conv_prompt(xm, cw, cb, wqk, wv, tm, tiles_per_seq):
    rows = xm.shape[0]
    row = pl.BlockSpec((tm, ML_INNER), lambda i: (i, 0))
    out = jax.ShapeDtypeStruct((rows, ML_INNER), BF16)
    return pl.pallas_call(
        functools.partial(_ml_conv_prompt_kernel, tm=tm, tiles_per_seq=tiles_per_seq),
        grid=(rows // tm,),
        in_specs=[row, pl.BlockSpec((8, ML_INNER), lambda i: (jnp.maximum(i * (tm // 8) - 1, 0), 0)),
                  _resident(cw.shape), _resident(cb.shape), _resident(wqk.shape), _resident(wv.shape)],
        out_specs=[row] * 4,
        out_shape=[jax.ShapeDtypeStruct((rows, ML_INNER), F32), out, out, out],
        scratch_shapes=[pltpu.VMEM((tm + 8, ML_INNER), F32)],
        compiler_params=_params("arbitrary"),
        name="ml_conv_prompt",
    )(xm, xm, cw, cb, wqk, wv)


def _ml_conv_sample(taps, cw, cb, wqk, wv):
    rows = taps[0].shape[0]
    full = pl.BlockSpec((rows, ML_INNER), lambda i: (0, 0))
    const = lambda a: pl.BlockSpec(a.shape, lambda i: (0,) * a.ndim)
    out = jax.ShapeDtypeStruct((rows, ML_INNER), F32)
    return pl.pallas_call(
        _ml_conv_sample_kernel,
        grid=(1,),
        in_specs=[full] * 4 + [const(cw), const(cb), const(wqk), const(wv)],
        out_specs=[full] * 4,
        out_shape=[out] * 4,
        compiler_params=_params("arbitrary"),
        name="ml_conv_sample",
    )(*taps, cw, cb, wqk, wv)


def _ml_cell_kernel(q_ref, k_ref, v_ref, gl_ref, c0_ref, n0_ref, m0_ref,
                    h_ref, c_ref, n_ref, m_ref, *, chunk, n_chunks):
    step = pl.program_id(1)

    @pl.when(step == 0)
    def _():
        c_ref[...] = c0_ref[0]
        n_ref[...] = n0_ref[0]
        m_ref[...] = m0_ref[...]

    gl = gl_ref[0]
    ti = lax.broadcasted_iota(jnp.int32, (chunk, chunk), 0)
    si = lax.broadcasted_iota(jnp.int32, (chunk, chunk), 1)
    tri = si <= ti
    cum = _dot_exact(tri.astype(F32), gl)
    sel = (lax.broadcasted_iota(jnp.int32, (8, LANE), 0) == lax.broadcasted_iota(jnp.int32, (8, LANE), 1)).astype(F32)
    gl_rows = _dot_nt_exact(sel, gl)
    cum_rows = _dot_nt_exact(sel, cum)
    for h in range(ML_HEADS):
        hs = slice(h * ML_HEAD_DIM, (h + 1) * ML_HEAD_DIM)
        q = q_ref[0, :, hs].astype(BF16)
        k = k_ref[0, :, hs].astype(F32)
        v = v_ref[0, :, hs].astype(BF16)
        ig_col = gl[:, h:h + 1]
        ig_row = gl_rows[h:h + 1, :]
        b_col = cum[:, ML_HEADS + h:ML_HEADS + h + 1]
        b_row = cum_rows[ML_HEADS + h:ML_HEADS + h + 1, :]
        b_last = b_col[chunk - 1:chunk, :]
        m_prev = m_ref[0, 0:1, h:h + 1]
        c_prev = c_ref[0, h]
        n_prev = n_ref[0, h:h + 1, :]

        dmat = jnp.where(tri, b_col - b_row + ig_row, -jnp.inf)
        inter = b_col + m_prev
        m_t = jnp.maximum(inter, jnp.max(dmat, axis=-1, keepdims=True))
        w_int = jnp.exp(inter - m_t)
        s = _dot_nt(q, k.astype(BF16)) * jnp.exp(dmat - m_t)
        num = w_int * _dot(q, c_prev.astype(BF16)) + _dot(s.astype(BF16), v)
        den = w_int * jnp.sum(q.astype(F32) * n_prev, axis=-1, keepdims=True) + jnp.sum(s, axis=-1, keepdims=True)
        h_ref[0, :, hs] = num / jnp.maximum(jnp.abs(den), jnp.exp(-m_t))

        m_new = m_t[chunk - 1:chunk, :]
        w_dec = jnp.exp(b_last + m_prev - m_new)
        kw = k * jnp.exp(b_last - b_col + ig_col - m_new)
        c_ref[0, h] = w_dec * c_prev + _dot_tn(kw.astype(BF16), v)
        n_ref[0, h:h + 1, :] = w_dec * n_prev + jnp.sum(kw, axis=0, keepdims=True)
        m_ref[0, :, h:h + 1] = jnp.broadcast_to(m_new, (8, 1))


def _ml_cell(q, k, v, gl, c0, n0, layer, m0, chunk):
    b, t, _ = q.shape
    n_chunks = t // chunk
    seq = lambda n: pl.BlockSpec((1, chunk, n), lambda i, c: (i, c, 0))
    c_spec = pl.BlockSpec((1, ML_HEADS, ML_HEAD_DIM, ML_HEAD_DIM), lambda i, c: (i, 0, 0, 0))
    n_spec = pl.BlockSpec((1, ML_HEADS, ML_HEAD_DIM), lambda i, c: (i, 0, 0))
    m_spec = pl.BlockSpec((1, 8, LANE), lambda i, c: (i, 0, 0))
    c0_spec = pl.BlockSpec((1, 1, ML_HEADS, ML_HEAD_DIM, ML_HEAD_DIM), lambda i, c: (layer, i, 0, 0, 0))
    n0_spec = pl.BlockSpec((1, 1, ML_HEADS, ML_HEAD_DIM), lambda i, c: (layer, i, 0, 0))
    return pl.pallas_call(
        functools.partial(_ml_cell_kernel, chunk=chunk, n_chunks=n_chunks),
        grid=(b, n_chunks),
        in_specs=[seq(ML_INNER), seq(ML_INNER), seq(ML_INNER), seq(LANE), c0_spec, n0_spec, m_spec],
        out_specs=[seq(ML_INNER), c_spec, n_spec, m_spec],
        out_shape=[jax.ShapeDtypeStruct((b, t, ML_INNER), F32),
                   jax.ShapeDtypeStruct(c0.shape[1:], F32),
                   jax.ShapeDtypeStruct(n0.shape[1:], F32),
                   jax.ShapeDtypeStruct(m0.shape, F32)],
        compiler_params=_params("arbitrary", "arbitrary"),
        name="ml_cell",
    )(q, k, v, gl, c0, n0, m0)


def _ml_out_kernel(x_ref, h_ref, z_ref, xc_ref, gn_ref, skip_ref, w_ref, mod_ref, y_ref):
    hg = h_ref[...] * jax.nn.sigmoid(z_ref[...])
    parts = []
    for h in range(ML_HEADS):
        hh = hg[:, h * ML_HEAD_DIM:(h + 1) * ML_HEAD_DIM]
        parts.append(hh * lax.rsqrt(jnp.mean(hh * hh, axis=-1, keepdims=True) + NORM_EPS))
    hn = jnp.concatenate(parts, axis=-1) * gn_ref[...] + skip_ref[...] * xc_ref[...]
    y_ref[...] = x_ref[...] + (1.0 + mod_ref[2]) * _dot(hn.astype(BF16), w_ref[...])


def _ml_out(x, hcell, z, xc, gn, skip, w, mod, tm, tiles_per_seq=1):
    rows = x.shape[0]
    row = lambda n: pl.BlockSpec((tm, n), lambda i: (i, 0))
    return pl.pallas_call(
        _ml_out_kernel,
        grid=(rows // tm,),
        in_specs=[row(D_MODEL), row(ML_INNER), row(ML_INNER), row(ML_INNER),
                  _resident((1, ML_INNER)), _resident((1, ML_INNER)), _resident((ML_INNER, D_MODEL)),
                  _mod_spec(mod, tm, tiles_per_seq)],
        out_specs=row(D_MODEL),
        out_shape=jax.ShapeDtypeStruct((rows, D_MODEL), F32),
        compiler_params=_params("arbitrary"),
        name="ml_out",
    )(x, hcell, z, xc, gn, skip, w, mod)


TM_PROMPT = 512
ML_CHUNK_PROMPT = 256
ML_CHUNK_SAMPLE = 8


def _pad_cols(w, n):
    return jnp.pad(w, ((0, 0), (0, n - w.shape[1])))


def _heads_major(x, b, t, n_heads):
    return x.reshape(b, t, n_heads, HEAD_DIM).transpose(0, 2, 1, 3)


def _nsa_prompt_layer(x, g, mod, w_in, w_out, cmp_w, tables, b, t):
    tiles = t // TM_PROMPT
    q, cmp_kv, sel_kv, win_kv, gates = _nsa_proj(x, g, mod, w_in, tables, TM_PROMPT, tiles)
    pe, w1c, _, w2b = cmp_w
    cmp = _compress_prompt(cmp_kv.reshape(b, t // CMP_STRIDE, CMP_STRIDE * KV_LANES), pe, w1c, w2b)
    nseg = cmp.shape[2]
    cmp5 = cmp.reshape(b, 2, nseg, KV_HEADS, HEAD_DIM)
    qh = _heads_major(q, b, t, N_HEADS).astype(BF16)
    o_cmp, q_aug = _cmp_select(qh, cmp5[:, 0].transpose(0, 2, 1, 3), cmp5[:, 1].transpose(0, 2, 3, 1), gates, tq=256)
    keys = lambda kv: kv.reshape(b, t, 2, KV_HEADS, HEAD_DIM)[:, :, 0].transpose(0, 2, 1, 3).astype(BF16)
    vals_t = lambda kv: kv.reshape(b, t, 2, KV_HEADS, HEAD_DIM)[:, :, 1].transpose(0, 2, 3, 1).astype(BF16)
    onehot = (jnp.arange(t)[:, None] // SEL_BLOCK == jnp.arange(HEAD_DIM)[None, :]).astype(BF16)
    ks_aug = jnp.concatenate([keys(sel_kv), jnp.broadcast_to(onehot, (b, KV_HEADS, t, HEAD_DIM))], axis=-1)
    kw_aug = jnp.concatenate([keys(win_kv), jnp.zeros((b, KV_HEADS, t, HEAD_DIM), BF16)], axis=-1)
    o = _flash(q_aug, ks_aug, vals_t(sel_kv), kw_aug, vals_t(win_kv), o_cmp, gates, tq=256, tk=256)
    x = _out_proj(x, o.reshape(b * t, NSA_Q), w_out, mod, TM_PROMPT, tiles)
    kv6 = lambda a: a.reshape(b, t, 2, KV_HEADS, HEAD_DIM)
    return x, kv6(cmp_kv), kv6(sel_kv), kv6(win_kv)[:, t - min(WINDOW, t):]


def _paged_view(cache):
    return jnp.transpose(cache, (0, 1, 3, 4, 5, 2))


def _nsa_sample_layer(x, g, mod, w_in, w_out, cmp_w, tables, cmp_t, sel_t, win_t, layer, win_buf, page_table, b, t):
    rows = b * t
    past = page_table.shape[1] * PAGE_SIZE
    q, cmp_kv, sel_kv, win_kv, gates = _nsa_proj(x, g, mod, w_in, tables, rows)
    cmp = _compress_paged(cmp_t, layer, page_table, *cmp_w)
    q5 = (q * SCALE).reshape(b, t, KV_HEADS, GROUP, HEAD_DIM).transpose(0, 3, 2, 1, 4)
    qbd = (q5[:, :, :, :, None, :] * jnp.eye(KV_HEADS, dtype=F32)[None, None, :, None, :, None])
    qbd = qbd.reshape(b, GROUP * KV_HEADS * t, KV_HEADS * HEAD_DIM).astype(BF16)
    o_cmp, bias = _dec_cmp_select(qbd, cmp[:, 0], cmp[:, 1], t, past)
    n_sel_pad = bias.shape[2]
    onehot = (jnp.arange(past)[:, None] // SEL_BLOCK == jnp.arange(n_sel_pad)[None, :]).astype(BF16)
    pad8 = lambda a: jnp.pad(a.reshape(b, t, KV_LANES), ((0, 0), (0, 8 - t), (0, 0)))
    wb = win_buf.shape[1]
    o_sel, o_win = _dec_attend(sel_t, win_t, layer, page_table, qbd, bias, onehot, pad8(sel_kv), pad8(win_kv), t, past)
    lane_dense = lambda o: o.reshape(b, GROUP, KV_HEADS, t, HEAD_DIM).transpose(0, 3, 2, 1, 4).reshape(rows, NSA_Q)
    x = _gated_out_proj(x, lane_dense(o_cmp), lane_dense(o_sel), lane_dense(o_win), gates, w_out, mod)
    kv6 = lambda a: a.reshape(b, t, 2, KV_HEADS, HEAD_DIM)
    wkv = jnp.concatenate([win_buf, kv6(win_kv)], axis=1)
    return x, kv6(cmp_kv), kv6(sel_kv), wkv[:, wb + t - min(WINDOW, wb + t):]


def _ml_weights(w_in, b_gate, conv_w, conv_b, wq, wk, wv, skip, gn_g, w_out):
    wqk, wvb = _headwise_weights(wq, wk, wv)
    bg = jnp.pad(b_gate.reshape(1, 2 * ML_HEADS), ((0, 0), (0, LANE - 2 * ML_HEADS)))
    return dict(w_in=_pad_cols(w_in, ML_N).astype(BF16), bg=bg, cw=conv_w, cb=conv_b.reshape(1, ML_INNER),
                wqk=wqk, wv=wvb, skip=skip.reshape(1, ML_INNER), gn=gn_g.reshape(1, ML_INNER),
                w_out=w_out.astype(BF16))


def _ml_prompt_layer(x, g, mod, w, b, t):
    tiles = t // TM_PROMPT
    xm, z, gl = _ml_proj(x, g, mod, w['w_in'], w['bg'], TM_PROMPT, tiles)
    xc, q, k, v = _ml_conv_prompt(xm, w['cw'], w['cb'], w['wqk'], w['wv'], TM_PROMPT, tiles)
    seq = lambda a: a.reshape(b, t, a.shape[-1])
    c0 = jnp.zeros((1, b, ML_HEADS, ML_HEAD_DIM, ML_HEAD_DIM), F32)
    n0 = jnp.zeros((1, b, ML_HEADS, ML_HEAD_DIM), F32)
    m0 = jnp.zeros((b, 8, LANE), F32)
    hcell, c, n, m = _ml_cell(seq(q), seq(k), seq(v), seq(gl), c0, n0, 0, m0, ML_CHUNK_PROMPT)
    x = _ml_out(x, hcell.reshape(b * t, ML_INNER), z, xc, w['gn'], w['skip'], w['w_out'], mod, TM_PROMPT, tiles)
    return x, c, n, m[:, 0, :ML_HEADS], seq(xm)[:, t - (CONV_W - 1):]


def _ml_sample_layer(x, g, mod, w, conv_buf, c0, n0, layer, m0, b, t):
    rows = b * t
    xm, z, gl = _ml_proj(x, g, mod, w['w_in'], w['bg'], rows)
    xpad = jnp.concatenate([conv_buf, xm.reshape(b, t, ML_INNER)], axis=1)
    taps = [xpad[:, k:k + t].reshape(rows, ML_INNER) for k in range(CONV_W)]
    xc, q, k, v = _ml_conv_sample(taps, w['cw'], w['cb'], w['wqk'], w['wv'])
    tp = ML_CHUNK_SAMPLE
    pad = lambda a: jnp.pad(a.reshape(b, t, a.shape[-1]), ((0, 0), (0, tp - t), (0, 0)))
    lane = jnp.arange(LANE)[None, None, :]
    step = jnp.arange(tp)[None, :, None]
    glp = jnp.where(step < t, pad(gl), jnp.where(lane < ML_HEADS, NEG, 0.0))
    m0p = jnp.broadcast_to(jnp.pad(m0, ((0, 0), (0, LANE - ML_HEADS)))[:, None, :], (b, 8, LANE))
    hcell, c, n, m = _ml_cell(pad(q), pad(k), pad(v), glp, c0, n0, layer, m0p, tp)
    x = _ml_out(x, hcell[:, :t].reshape(rows, ML_INNER), z, xc, w['gn'], w['skip'], w['w_out'], mod, rows)
    return x, c, n, m[:, 0, :ML_HEADS], xpad[:, t:]


def kernel(x_prompt, x_sample, cache_cmp, cache_sel, cache_win, state_C, state_n, state_m, state_conv, page_table,
           c_prompt, c_sample, ada_w, ada_b, norm_g, ffn_w_in, ffn_w_out, final_norm_g,
           attn_w_in, attn_w_out, cmp_pos, cmp_w1, cmp_w2,
           ml_w_in, ml_b_gate, ml_conv_w, ml_conv_b, ml_wq, ml_wk, ml_wv, ml_skip, ml_gn_g, ml_w_out):
    bp, tp, _ = x_prompt.shape
    bs, ts, _ = x_sample.shape
    depth = ada_w.shape[0]
    rows_s = bs * ts
    past = page_table.shape[1] * PAGE_SIZE

    n_c = bp + bs
    c_all = jnp.pad(jnp.concatenate([c_prompt, c_sample], axis=0), ((0, -n_c % 8), (0, 0)))
    mods = _adaln(c_all, ada_w, ada_b)

    xp = x_prompt.reshape(bp * tp, D_MODEL)
    xs = x_sample.reshape(rows_s, D_MODEL)
    tables_p = _rope_tables(jnp.arange(tp))
    tables_s = tuple(jnp.tile(tb, (bs, 1)) for tb in _rope_tables(past + jnp.arange(ts)))
    tiles = tp // TM_PROMPT
    cmp_t, sel_t, win_t = _paged_view(cache_cmp), _paged_view(cache_sel), _paged_view(cache_win)

    outs_p = {k: [] for k in ('cmp', 'sel', 'win', 'c', 'n', 'm', 'conv')}
    outs_s = {k: [] for k in ('cmp', 'sel', 'win', 'c', 'n', 'm', 'conv')}
    for i in range(depth):
        mod_i = mods[i, :n_c].reshape(n_c, 9, D_MODEL)
        mp = [mod_i[:bp, 3 * s:3 * s + 3].reshape(bp, 3, 1, D_MODEL) for s in range(3)]
        ms = [jnp.repeat(mod_i[bp:, 3 * s:3 * s + 3].transpose(1, 0, 2), ts, axis=1) for s in range(3)]
        g = norm_g[i].reshape(3, 1, D_MODEL)
        w_in = ffn_w_in[i].astype(BF16)
        w_out = ffn_w_out[i].astype(BF16)

        xp = _ffn(xp, g[0], mp[0], w_in[0], w_out[0], TM_PROMPT, tiles)
        xs = _ffn(xs, g[0], ms[0], w_in[0], w_out[0], rows_s)
        j = i // 2
        if i % 2 == 0:
            a_in = _pad_cols(attn_w_in[j], NSA_N).astype(BF16)
            a_out = attn_w_out[j].astype(BF16)
            cmp_w = _compress_weights(cmp_pos[j], cmp_w1[j], cmp_w2[j])
            xp, ckv, skv, wkv = _nsa_prompt_layer(xp, g[1], mp[1], a_in, a_out, cmp_w, tables_p, bp, tp)
            xs, ckv_s, skv_s, wkv_s = _nsa_sample_layer(xs, g[1], ms[1], a_in, a_out, cmp_w, tables_s,
                                                        cmp_t, sel_t, win_t, j, cache_win[j], page_table, bs, ts)
            for d, vals in ((outs_p, (ckv, skv, wkv)), (outs_s, (ckv_s, skv_s, wkv_s))):
                for key, val in zip(('cmp', 'sel', 'win'), vals):
                    d[key].append(val)
        else:
            w = _ml_weights(ml_w_in[j], ml_b_gate[j], ml_conv_w[j], ml_conv_b[j], ml_wq[j], ml_wk[j], ml_wv[j],
                            ml_skip[j], ml_gn_g[j], ml_w_out[j])
            xp, c_p, n_p, m_p, buf_p = _ml_prompt_layer(xp, g[1], mp[1], w, bp, tp)
            xs, c_s, n_s, m_s, buf_s = _ml_sample_layer(xs, g[1], ms[1], w, state_conv[j], state_C, state_n, j,
                                                        state_m[j], bs, ts)
            for d, vals in ((outs_p, (c_p, n_p, m_p, buf_p)), (outs_s, (c_s, n_s, m_s, buf_s))):
                for key, val in zip(('c', 'n', 'm', 'conv'), vals):
                    d[key].append(val)
        xp = _ffn(xp, g[2], mp[2], w_in[1], w_out[1], TM_PROMPT, tiles)
        xs = _ffn(xs, g[2], ms[2], w_in[1], w_out[1], rows_s)

    fg = final_norm_g.reshape(1, D_MODEL)
    y_prompt = _final_norm(xp, fg, TM_PROMPT).reshape(bp, tp, D_MODEL)
    y_sample = _final_norm(xs, fg, rows_s).reshape(bs, ts, D_MODEL)
    st = lambda d, key: jnp.stack(d[key])
    return (y_prompt, y_sample,
            st(outs_p, 'cmp'), st(outs_p, 'sel'), st(outs_p, 'win'),
            st(outs_p, 'c'), st(outs_p, 'n'), st(outs_p, 'm'), st(outs_p, 'conv'),
            st(outs_s, 'cmp'), st(outs_s, 'sel'), st(outs_s, 'win'),
            st(outs_s, 'c'), st(outs_s, 'n'), st(outs_s, 'm'), st(outs_s, 'conv'))
```

```python
import functools
import math

import numpy as np
import jax
import jax.numpy as jnp
from jax import lax
from jax.experimental import pallas as pl
from jax.experimental.pallas import tpu as pltpu

D_MODEL = 1024
D_FF = 2816
N_HEADS = 16
HEAD_DIM = 64
KV_HEADS = 4
GROUP = 4
ROT_DIM = 16
ROPE_THETA = 500000.0
CMP_BLOCK = 32
CMP_STRIDE = 16
CMP_HIDDEN = 128
SEL_BLOCK = 64
SEL_TOPK = 16
WINDOW = 512
PAGE_SIZE = 128
KV_LANES = 2 * KV_HEADS * HEAD_DIM
NSA_Q = N_HEADS * HEAD_DIM
NSA_N = 2688
ML_INNER = 2048
ML_HEADS = 4
ML_HEAD_DIM = 512
ML_N = 4224
CONV_W = 4
NORM_EPS = 1e-6
NEG = -1e30
BIG = 1e30
SCALE = HEAD_DIM ** -0.5
LOG2E = 1.4426950408889634

F32 = jnp.float32
BF16 = jnp.bfloat16
LANE = 128
VMEM_LIMIT_BYTES = 56 * 1024 * 1024


def _params(*sem):
    return pltpu.CompilerParams(dimension_semantics=sem, vmem_limit_bytes=VMEM_LIMIT_BYTES)


def _resident(shape):
    nd = len(shape)
    return pl.BlockSpec(shape, lambda *_: (0,) * nd, pipeline_mode=pl.Buffered(1))


def _dot(a, b):
    return jnp.dot(a, b, preferred_element_type=F32)


def _dot_nt(a, b):
    return lax.dot_general(a, b, (((1,), (1,)), ((), ())), preferred_element_type=F32)


def _dot_tn(a, b):
    return lax.dot_general(a, b, (((0,), (0,)), ((), ())), preferred_element_type=F32)


def _dot_exact(a, b):
    return jnp.dot(a, b, preferred_element_type=F32, precision=lax.Precision.HIGHEST)


def _dot_nt_exact(a, b):
    return lax.dot_general(a, b, (((1,), (1,)), ((), ())), preferred_element_type=F32,
                           precision=lax.Precision.HIGHEST)


def _modulated_norm(x, g, shift, scale):
    y = x * lax.rsqrt(jnp.mean(x * x, axis=-1, keepdims=True) + NORM_EPS)
    return (y * g) * (1.0 + scale) + shift


def _mod_spec(mod, tm, tiles_per_seq):
    if mod.ndim == 4:
        return pl.BlockSpec((None, 3, 1, D_MODEL), lambda i: (i // tiles_per_seq, 0, 0, 0))
    return pl.BlockSpec((3, tm, D_MODEL), lambda i: (0, i, 0))


def _adaln_kernel(c_ref, w_ref, b_ref, o_ref):
    a = jax.nn.silu(c_ref[...]).astype(BF16)
    o_ref[0] = _dot(a, w_ref[0].astype(BF16)) + b_ref[0]


def _adaln(c_all, ada_w, ada_b):
    depth, _, n = ada_w.shape
    rows = c_all.shape[0]
    tn = 2304
    return pl.pallas_call(
        _adaln_kernel,
        grid=(depth, n // tn),
        in_specs=[pl.BlockSpec((rows, D_MODEL), lambda l, j: (0, 0)),
                  pl.BlockSpec((1, D_MODEL, tn), lambda l, j: (l, 0, j)),
                  pl.BlockSpec((1, 1, tn), lambda l, j: (l, 0, j))],
        out_specs=pl.BlockSpec((1, rows, tn), lambda l, j: (l, 0, j)),
        out_shape=jax.ShapeDtypeStruct((depth, rows, n), F32),
        compiler_params=_params("arbitrary", "arbitrary"),
        name="adaln",
    )(c_all, ada_w, ada_b.reshape(depth, 1, n))


FF_CHUNK = 1408


def _ffn_kernel(x_ref, g_ref, mod_ref, win_ref, wout_ref, o_ref):
    x = x_ref[...]
    h = _modulated_norm(x, g_ref[...], mod_ref[0], mod_ref[1]).astype(BF16)
    acc = jnp.zeros(x.shape, F32)
    for c in range(D_FF // FF_CHUNK):
        lo = c * FF_CHUNK
        gate = _dot(h, win_ref[:, lo:lo + FF_CHUNK])
        up = _dot(h, win_ref[:, D_FF + lo:D_FF + lo + FF_CHUNK])
        act = (jax.nn.silu(gate) * up).astype(BF16)
        acc = acc + _dot(act, wout_ref[lo:lo + FF_CHUNK, :])
    o_ref[...] = x + 0.5 * (1.0 + mod_ref[2]) * acc


def _ffn(x, g, mod, w_in, w_out, tm, tiles_per_seq=1):
    rows = x.shape[0]
    return pl.pallas_call(
        _ffn_kernel,
        grid=(rows // tm,),
        in_specs=[pl.BlockSpec((tm, D_MODEL), lambda i: (i, 0)),
                  _resident((1, D_MODEL)),
                  _mod_spec(mod, tm, tiles_per_seq),
                  _resident((D_MODEL, 2 * D_FF)),
                  _resident((D_FF, D_MODEL))],
        out_specs=pl.BlockSpec((tm, D_MODEL), lambda i: (i, 0)),
        out_shape=jax.ShapeDtypeStruct((rows, D_MODEL), F32),
        compiler_params=_params("arbitrary"),
        name="ffn_half",
    )(x, g, mod, w_in, w_out)


def _final_norm_kernel(x_ref, g_ref, o_ref):
    x = x_ref[...]
    o_ref[...] = x * lax.rsqrt(jnp.mean(x * x, axis=-1, keepdims=True) + NORM_EPS) * g_ref[...]


def _final_norm(x, g, tm):
    rows = x.shape[0]
    return pl.pallas_call(
        _final_norm_kernel,
        grid=(rows // tm,),
        in_specs=[pl.BlockSpec((tm, D_MODEL), lambda i: (i, 0)), _resident((1, D_MODEL))],
        out_specs=pl.BlockSpec((tm, D_MODEL), lambda i: (i, 0)),
        out_shape=jax.ShapeDtypeStruct((rows, D_MODEL), F32),
        compiler_params=_params("arbitrary"),
        name="final_norm",
    )(x, g)


def _rope_tables(pos):
    half = ROT_DIM // 2
    inv = ROPE_THETA ** (-jnp.arange(0, ROT_DIM, 2, dtype=F32) / ROT_DIM)
    ang = pos.astype(F32)[:, None] * inv[None, :]
    cos, sin = jnp.cos(ang), jnp.sin(ang)
    n = pos.shape[0]
    pad = jnp.zeros((n, HEAD_DIM - ROT_DIM), F32)
    zeros = jnp.zeros((n, half), F32)
    c = jnp.concatenate([cos, cos, pad + 1.0], axis=1)
    s1 = jnp.concatenate([-sin, zeros, pad], axis=1)
    s2 = jnp.concatenate([zeros, sin, pad], axis=1)
    return tuple(jnp.tile(t, (1, LANE // HEAD_DIM)) for t in (c, s1, s2))


def _nsa_proj_kernel(x_ref, g_ref, mod_ref, w_ref, c_ref, s1_ref, s2_ref, *out_refs, prompt_layouts, tiles_per_seq):
    h = _modulated_norm(x_ref[...], g_ref[...], mod_ref[0], mod_ref[1]).astype(BF16)
    proj = _dot(h, w_ref[...])
    c, s1, s2 = c_ref[...], s1_ref[...], s2_ref[...]
    tm = proj.shape[0]
    half = KV_LANES // 2

    def rope(lo, n):
        chunks = []
        for j in range(n // LANE):
            v = proj[:, lo + j * LANE:lo + (j + 1) * LANE]
            chunks.append(v * c + pltpu.roll(v, LANE - ROT_DIM // 2, 1) * s1 + pltpu.roll(v, ROT_DIM // 2, 1) * s2)
        return jnp.concatenate(chunks, axis=1)

    q = rope(0, NSA_Q)
    branches = []
    for br in range(3):
        base = NSA_Q + br * KV_LANES
        branches.append(jnp.concatenate([rope(base, half), proj[:, base + half:base + KV_LANES]], axis=1))
    gates = jax.nn.sigmoid(proj[:, NSA_Q + 3 * KV_LANES:])

    if not prompt_layouts:
        q_ref, cmp_ref, sel_ref, win_ref, gate_ref = out_refs
        q_ref[...] = q
        gate_ref[...] = gates
        for ref, kv in zip((cmp_ref, sel_ref, win_ref), branches):
            ref[...] = kv
        return

    q_ref, cmp_ref, gate_ref, cmp_t_ref, sel_t_ref, win_t_ref, ks_ref, kw_ref, vst_ref, vwt_ref = out_refs
    q_ref[...] = q
    gate_ref[...] = gates
    cmp_ref[...] = branches[0]
    transposed = [kv.T for kv in branches]
    for ref, kv_t in zip((cmp_t_ref, sel_t_ref, win_t_ref), transposed):
        ref[0] = kv_t
    vst_ref[0] = transposed[1][half:].astype(BF16)
    vwt_ref[0] = transposed[2][half:].astype(BF16)
    pos = (pl.program_id(0) % tiles_per_seq) * tm + lax.broadcasted_iota(jnp.int32, (tm, HEAD_DIM), 0)
    onehot = (pos // SEL_BLOCK == lax.broadcasted_iota(jnp.int32, (tm, HEAD_DIM), 1)).astype(BF16)
    zeros = jnp.zeros((tm, HEAD_DIM), BF16)
    for g in range(KV_HEADS):
        sl = slice(g * HEAD_DIM, (g + 1) * HEAD_DIM)
        ks_ref[0, g, :, 0:HEAD_DIM] = branches[1][:, sl].astype(BF16)
        ks_ref[0, g, :, HEAD_DIM:] = onehot
        kw_ref[0, g, :, 0:HEAD_DIM] = branches[2][:, sl].astype(BF16)
        kw_ref[0, g, :, HEAD_DIM:] = zeros


def _nsa_proj(x, g, mod, w, tables, tm, tiles_per_seq=1, prompt_layouts=False):
    rows = x.shape[0]
    trows = tables[0].shape[0]
    tspec = pl.BlockSpec((tm, LANE), lambda i: (i % (trows // tm), 0))
    row = lambda n: pl.BlockSpec((tm, n), lambda i: (i, 0))
    f32 = lambda *shape: jax.ShapeDtypeStruct(shape, F32)
    bf16 = lambda *shape: jax.ShapeDtypeStruct(shape, BF16)
    if prompt_layouts:
        b, t = rows // (tiles_per_seq * tm), tiles_per_seq * tm
        seq_t = lambda n: pl.BlockSpec((1, n, tm), lambda i: (i // tiles_per_seq, 0, i % tiles_per_seq))
        heads = pl.BlockSpec((1, KV_HEADS, tm, 2 * HEAD_DIM), lambda i: (i // tiles_per_seq, 0, i % tiles_per_seq, 0))
        out_specs = [row(NSA_Q), row(KV_LANES), row(LANE), seq_t(KV_LANES), seq_t(KV_LANES), seq_t(KV_LANES),
                     heads, heads, seq_t(KV_LANES // 2), seq_t(KV_LANES // 2)]
        out_shape = [f32(rows, NSA_Q), f32(rows, KV_LANES), f32(rows, LANE),
                     f32(b, KV_LANES, t), f32(b, KV_LANES, t), f32(b, KV_LANES, t),
                     bf16(b, KV_HEADS, t, 2 * HEAD_DIM), bf16(b, KV_HEADS, t, 2 * HEAD_DIM),
                     bf16(b, KV_LANES // 2, t), bf16(b, KV_LANES // 2, t)]
    else:
        out_specs = [row(NSA_Q), row(KV_LANES), row(KV_LANES), row(KV_LANES), row(LANE)]
        out_shape = [f32(rows, n) for n in (NSA_Q, KV_LANES, KV_LANES, KV_LANES, LANE)]
    return pl.pallas_call(
        functools.partial(_nsa_proj_kernel, prompt_layouts=prompt_layouts, tiles_per_seq=tiles_per_seq),
        grid=(rows // tm,),
        in_specs=[row(D_MODEL), _resident((1, D_MODEL)), _mod_spec(mod, tm, tiles_per_seq),
                  _resident((D_MODEL, NSA_N)), tspec, tspec, tspec],
        out_specs=out_specs,
        out_shape=out_shape,
        compiler_params=_params("arbitrary"),
        name="nsa_proj",
    )(x, g, mod, w, *tables)


def _compress_finish(seg_proj, pe_ref, w1_ref, w2_ref, out_ref, nseg):
    row = lax.broadcasted_iota(jnp.int32, (nseg, 1), 0)
    for s in range(2):
        bias = jnp.zeros((8, 2 * CMP_HIDDEN), F32)
        lane = lax.broadcasted_iota(jnp.int32, (8, 2 * CMP_HIDDEN), 1)
        for l in range(CMP_STRIDE):
            pa = jnp.broadcast_to(pe_ref[s, l:l + 1, :], (8, HEAD_DIM)).astype(BF16)
            pb = jnp.broadcast_to(pe_ref[s, CMP_STRIDE + l:CMP_STRIDE + l + 1, :], (8, HEAD_DIM)).astype(BF16)
            bias = bias + jnp.where(lane < CMP_HIDDEN, _dot(pa, w1_ref[s, l]), _dot(pb, w1_ref[s, l]))
        bias = bias[0:1]
        for g in range(KV_HEADS):
            a = seg_proj(s, g) + bias
            first, second = a[:, :CMP_HIDDEN], a[:, CMP_HIDDEN:]
            hid = jax.nn.gelu(first + pltpu.roll(second, nseg - 1, 0))
            o = _dot(hid.astype(BF16), w2_ref[s])
            o = jnp.where(row < nseg - 1, o, 0.0)
            out_ref[s, :, g * HEAD_DIM:(g + 1) * HEAD_DIM] = o.astype(out_ref.dtype)


def _compress_prompt_kernel(x_ref, pe_ref, w1_ref, w2_ref, out_ref, acc_ref, *, nseg):
    for l in range(CMP_STRIDE):
        xl = x_ref[0, :, l * KV_LANES:(l + 1) * KV_LANES].astype(BF16)
        for sg in range(2 * KV_HEADS):
            d = _dot(xl[:, sg * HEAD_DIM:(sg + 1) * HEAD_DIM], w1_ref[sg // KV_HEADS, l])
            if l == 0:
                acc_ref[sg] = d
            else:
                acc_ref[sg] += d
    _compress_finish(lambda s, g: acc_ref[s * KV_HEADS + g], pe_ref, w1_ref, w2_ref, out_ref.at[0], nseg)


def _compress_weights(pos_emb, w1, w2):
    pe = jnp.transpose(pos_emb, (1, 0, 2))
    w1c = jnp.concatenate([w1[:, :CMP_STRIDE], w1[:, CMP_STRIDE:]], axis=-1).astype(BF16)
    zero = jnp.zeros_like(w1c)
    w1bd = jnp.concatenate([jnp.concatenate([w1c, zero], axis=-1), jnp.concatenate([zero, w1c], axis=-1)], axis=-2)
    w1bd = w1bd.reshape(2, CMP_STRIDE // 2, 4 * HEAD_DIM, 4 * CMP_HIDDEN)
    return pe, w1c, w1bd, w2.astype(BF16)


def _compress_prompt(cmp_kv, pe, w1c, w2b):
    b, nseg, seg_lanes = cmp_kv.shape
    return pl.pallas_call(
        functools.partial(_compress_prompt_kernel, nseg=nseg),
        grid=(b,),
        in_specs=[pl.BlockSpec((1, nseg, seg_lanes), lambda i: (i, 0, 0)),
                  _resident(pe.shape), _resident(w1c.shape), _resident(w2b.shape)],
        out_specs=pl.BlockSpec((1, 2, nseg, KV_LANES // 2), lambda i: (i, 0, 0, 0)),
        out_shape=jax.ShapeDtypeStruct((b, 2, nseg, KV_LANES // 2), BF16),
        scratch_shapes=[pltpu.VMEM((2 * KV_HEADS, nseg, 2 * CMP_HIDDEN), F32)],
        compiler_params=_params("arbitrary"),
        name="compress_prompt",
    )(cmp_kv, pe, w1c, w2b)


PAGES_PER_STEP = 16
SEGS_PER_PAGE = PAGE_SIZE // CMP_STRIDE


def _page_spec(layer, k):
    return pl.BlockSpec((1, 1, 2, KV_HEADS, HEAD_DIM, PAGE_SIZE),
                        lambda i, s, pt: (layer, pt[i, s * PAGES_PER_STEP + k], 0, 0, 0, 0))


def _compress_paged_kernel(pt_ref, *refs, nseg, n_steps):
    del pt_ref
    pages = refs[:PAGES_PER_STEP]
    pe_ref, w1_ref, w1bd_ref, w2_ref, out_ref, rows_ref, acc_ref = refs[PAGES_PER_STEP:]
    step = pl.program_id(1)
    pair = 2 * HEAD_DIM
    for k, page in enumerate(pages):
        r0 = pl.multiple_of((step * PAGES_PER_STEP + k) * PAGE_SIZE, PAGE_SIZE)
        for s in range(2):
            for gp in range(KV_HEADS // 2):
                tile = page[0, 0, s, 2 * gp:2 * gp + 2].reshape(pair, PAGE_SIZE)
                rows_ref[s * 2 + gp, pl.ds(r0, PAGE_SIZE), :] = tile.T

    @pl.when(step == n_steps - 1)
    def _():
        for sgp in range(KV_HEADS):
            for l in range(0, CMP_STRIDE, 2):
                xl = jnp.concatenate([rows_ref[sgp, pl.ds(l + i, nseg, stride=CMP_STRIDE), :] for i in range(2)],
                                     axis=1).astype(BF16)
                d = _dot(xl, w1bd_ref[sgp // 2, l // 2])
                if l == 0:
                    acc_ref[sgp] = d
                else:
                    acc_ref[sgp] += d
        width = 2 * CMP_HIDDEN

        def seg_proj(s, g):
            return acc_ref[s * 2 + g // 2, :, (g % 2) * width:(g % 2 + 1) * width]

        _compress_finish(seg_proj, pe_ref, w1_ref, w2_ref, out_ref.at[0], nseg)


def _compress_paged(cache_t, layer, page_table, pe, w1c, w1bd, w2b):
    b, n_pages = page_table.shape
    n_steps = n_pages // PAGES_PER_STEP
    nseg = n_pages * SEGS_PER_PAGE
    const = lambda a: pl.BlockSpec(a.shape, lambda i, s, pt: (0,) * a.ndim)
    grid_spec = pltpu.PrefetchScalarGridSpec(
        num_scalar_prefetch=1,
        grid=(b, n_steps),
        in_specs=[_page_spec(layer, k) for k in range(PAGES_PER_STEP)] + [const(pe), const(w1c), const(w1bd), const(w2b)],
        out_specs=pl.BlockSpec((1, 2, nseg, KV_LANES // 2), lambda i, s, pt: (i, 0, 0, 0)),
        scratch_shapes=[pltpu.VMEM((KV_HEADS, n_pages * PAGE_SIZE, 2 * HEAD_DIM), F32),
                        pltpu.VMEM((KV_HEADS, nseg, 4 * CMP_HIDDEN), F32)],
    )
    return pl.pallas_call(
        functools.partial(_compress_paged_kernel, nseg=nseg, n_steps=n_steps),
        grid_spec=grid_spec,
        out_shape=jax.ShapeDtypeStruct((b, 2, nseg, KV_LANES // 2), BF16),
        compiler_params=_params("arbitrary", "arbitrary"),
        name="compress_paged",
    )(page_table, *([cache_t] * PAGES_PER_STEP), pe, w1c, w1bd, w2b)


def _importance_matrix(nc_pad, nc, n_sel_pad, n_sel):
    i = np.arange(nc_pad)[:, None]
    j = np.arange(n_sel_pad)[None, :]
    overlap = np.maximum(np.minimum(i * CMP_STRIDE + CMP_BLOCK, (j + 1) * SEL_BLOCK)
                         - np.maximum(i * CMP_STRIDE, j * SEL_BLOCK), 0)
    m = overlap / CMP_STRIDE * (i < nc) * (j < n_sel)
    return jnp.asarray(m, dtype=BF16)


def _topk_mask(score, k, axis):
    n = score.shape[axis]
    idx = lax.broadcasted_iota(jnp.int32, score.shape, axis).astype(F32)

    def body(_, carry):
        sc, sel = carry
        m = jnp.max(sc, axis=axis, keepdims=True)
        first = jnp.min(jnp.where(sc == m, idx, float(n)), axis=axis, keepdims=True)
        hit = idx == first
        return jnp.where(hit, -jnp.inf, sc), jnp.where(hit, 1.0, sel)

    _, sel = lax.fori_loop(0, k, body, (score, jnp.zeros(score.shape, F32)))
    return sel


def _block_scores(imp, qpos, n_valid, axis):
    jj = lax.broadcasted_iota(jnp.int32, imp.shape, axis)
    cur = qpos // SEL_BLOCK
    forced = (jj == 0) | (jj == cur) | (jj == cur - 1)
    score = jnp.where(forced, BIG, jnp.where(jj <= cur, imp, NEG))
    return jnp.where(jj < n_valid, score, -jnp.inf)


def _cmp_select_kernel(q_ref, kc_ref, vct_ref, impt_ref, gate_ref, o_ref, qa_ref, *, tq, nc, n_sel):
    q0 = pl.program_id(1) * tq
    rows = GROUP * tq
    qpos = q0 + lax.broadcasted_iota(jnp.int32, (1, rows), 1) % tq
    cidx = lax.broadcasted_iota(jnp.int32, (kc_ref.shape[2], 1), 0)
    valid = (cidx * CMP_STRIDE + CMP_BLOCK - 1 <= qpos) & (cidx < nc)
    tpos = q0 + lax.broadcasted_iota(jnp.int32, (1, tq), 1)
    gates_t = gate_ref[...].T
    q2 = (q_ref[0] * (SCALE * LOG2E)).astype(BF16)
    heads = [q2[:, h * HEAD_DIM:(h + 1) * HEAD_DIM] for h in range(N_HEADS)]
    scores = [_dot_nt(kc_ref[0, g], jnp.concatenate(heads[g * GROUP:(g + 1) * GROUP], axis=0))
              for g in range(KV_HEADS)]
    o_t, bias_t = [], []
    for g in range(KV_HEADS):
        s = jnp.where(valid, scores[g], NEG)
        e = jnp.exp2(s - jnp.max(s, axis=0, keepdims=True))
        p = jnp.where(valid, e / jnp.sum(e, axis=0, keepdims=True), 0.0)
        o = _dot(vct_ref[0, g], p.astype(BF16))
        p_grp = p[:, 0:tq]
        for r in range(1, GROUP):
            p_grp = p_grp + p[:, r * tq:(r + 1) * tq]
        imp = _dot(impt_ref[...], p_grp.astype(BF16))
        sel = _topk_mask(_block_scores(imp, tpos, n_sel, 0), min(SEL_TOPK, n_sel), 0)
        bias_t.append(jnp.where(sel > 0.0, 0.0, NEG))
        for r in range(GROUP):
            h = g * GROUP + r
            o_t.append(o[:, r * tq:(r + 1) * tq] * gates_t[h:h + 1, :])
    o_ref[0] = jnp.concatenate(o_t, axis=0).T
    bias = jnp.concatenate(bias_t, axis=0).T.astype(BF16)
    for h in range(N_HEADS):
        g = h // GROUP
        qa_ref[0, h, :, 0:HEAD_DIM] = heads[h]
        qa_ref[0, h, :, HEAD_DIM:] = bias[:, g * HEAD_DIM:(g + 1) * HEAD_DIM]


def _cmp_select(q, kch, vct, gates, tq):
    b, t, _ = q.shape
    nseg = kch.shape[2]
    n_sel = t // SEL_BLOCK
    assert n_sel <= HEAD_DIM
    imp_t = _importance_matrix(nseg, nseg - 1, HEAD_DIM, n_sel).T
    return pl.pallas_call(
        functools.partial(_cmp_select_kernel, tq=tq, nc=nseg - 1, n_sel=n_sel),
        grid=(b, t // tq),
        in_specs=[pl.BlockSpec((1, tq, NSA_Q), lambda i, j: (i, j, 0)),
                  pl.BlockSpec((1, KV_HEADS, nseg, HEAD_DIM), lambda i, j: (i, 0, 0, 0)),
                  pl.BlockSpec((1, KV_HEADS, HEAD_DIM, nseg), lambda i, j: (i, 0, 0, 0)),
                  pl.BlockSpec((HEAD_DIM, nseg), lambda i, j: (0, 0)),
                  pl.BlockSpec((tq, LANE), lambda i, j: (i * (t // tq) + j, 0))],
        out_specs=[pl.BlockSpec((1, tq, NSA_Q), lambda i, j: (i, j, 0)),
                   pl.BlockSpec((1, N_HEADS, tq, 2 * HEAD_DIM), lambda i, j: (i, 0, j, 0))],
        out_shape=[jax.ShapeDtypeStruct((b, t, NSA_Q), F32),
                   jax.ShapeDtypeStruct((b, N_HEADS, t, 2 * HEAD_DIM), BF16)],
        compiler_params=_params("arbitrary", "arbitrary"),
        name="cmp_select",
    )(q, kch, vct, imp_t, gates)


def _flash_kernel(qa_ref, ks_ref, vst_ref, kw_ref, vwt_ref, oc_ref, gate_ref, o_ref, *, tq, tk):
    g = pl.program_id(1)
    q0 = pl.program_id(2) * tq
    qpos = q0 + lax.broadcasted_iota(jnp.int32, (1, tq), 1)
    heads = [qa_ref[0, r] for r in range(GROUP)]

    def softmax_step(m, l, acc, s, v_t):
        m_new = jnp.maximum(m, jnp.max(s, axis=0, keepdims=True))
        a = jnp.exp2(m - m_new)
        p = jnp.exp2(s - m_new)
        return m_new, a * l + jnp.sum(p, axis=0, keepdims=True), a * acc + _dot(v_t, p.astype(BF16))

    def tile_pair(j, carry, causal):
        carry = list(carry)
        k0 = pl.multiple_of(j * (2 * tk), 2 * tk)
        starts = [k0, k0 + tk]
        scores = [[_dot_nt(ks_ref[0, 0, pl.ds(st, tk), :], heads[r]) for r in range(GROUP)] for st in starts]
        for st, s_tile in zip(starts, scores):
            v_t = vst_ref[0, 0, :, pl.ds(st, tk)]
            for r in range(GROUP):
                s = s_tile[r]
                if causal:
                    s = jnp.where(st + lax.broadcasted_iota(jnp.int32, (tk, 1), 0) <= qpos, s, NEG)
                carry[r] = softmax_step(*carry[r], s, v_t)
        return tuple(carry)

    n_full = q0 // (2 * tk)
    init = tuple((jnp.full((1, tq), -jnp.inf, F32), jnp.zeros((1, tq), F32), jnp.zeros((HEAD_DIM, tq), F32))
                 for _ in range(GROUP))
    carry = lax.fori_loop(0, n_full, lambda j, c: tile_pair(j, c, False), init)
    sel = tile_pair(n_full, carry, True)

    wk = WINDOW + tq
    w0 = pl.multiple_of(jnp.maximum(q0 - WINDOW, 0), tq)
    kw_tile = kw_ref[0, 0, pl.ds(w0, wk), :]
    vw_t = vwt_ref[0, 0, :, pl.ds(w0, wk)]
    d = qpos - (w0 + lax.broadcasted_iota(jnp.int32, (wk, 1), 0))
    in_window = (d >= 0) & (d < WINDOW)

    gates_t = gate_ref[...].T
    sub = lax.broadcasted_iota(jnp.int32, gates_t.shape, 0)
    parts = []
    win_scores = [_dot_nt(kw_tile, heads[r]) for r in range(GROUP)]
    for r in range(GROUP):
        _, l, acc = sel[r]
        s = jnp.where(in_window, win_scores[r], NEG)
        p = jnp.exp2(s - jnp.max(s, axis=0, keepdims=True))
        o_win = _dot(vw_t, p.astype(BF16)) / jnp.sum(p, axis=0, keepdims=True)
        g_sel = jnp.sum(jnp.where(sub == N_HEADS + g * GROUP + r, gates_t, 0.0), axis=0, keepdims=True)
        g_win = jnp.sum(jnp.where(sub == 2 * N_HEADS + g * GROUP + r, gates_t, 0.0), axis=0, keepdims=True)
        parts.append(g_sel * (acc / l) + g_win * o_win)
    o = oc_ref[0] + jnp.concatenate(parts, axis=0).T
    o_ref[0] = o.astype(o_ref.dtype)


def _flash(qa, ks, vst, kw, vwt, oc, gates, tq, tk):
    b, _, t, _ = qa.shape
    grp_lanes = GROUP * HEAD_DIM
    keys = pl.BlockSpec((1, 1, t, 2 * HEAD_DIM), lambda i, g, j: (i, g, 0, 0))
    vals = pl.BlockSpec((1, 1, HEAD_DIM, t), lambda i, g, j: (i, g, 0, 0))
    return pl.pallas_call(
        functools.partial(_flash_kernel, tq=tq, tk=tk),
        grid=(b, KV_HEADS, t // tq),
        in_specs=[pl.BlockSpec((1, GROUP, tq, 2 * HEAD_DIM), lambda i, g, j: (i, g, j, 0)),
                  keys, vals, keys, vals,
                  pl.BlockSpec((1, tq, grp_lanes), lambda i, g, j: (i, j, g)),
                  pl.BlockSpec((tq, LANE), lambda i, g, j: (i * (t // tq) + j, 0))],
        out_specs=pl.BlockSpec((1, tq, grp_lanes), lambda i, g, j: (i, j, g)),
        out_shape=jax.ShapeDtypeStruct((b, t, NSA_Q), BF16),
        compiler_params=_params("arbitrary", "arbitrary", "arbitrary"),
        name="nsa_flash",
    )(qa, ks, vst, kw, vwt, oc, gates)


def _out_proj_kernel(x_ref, o_ref, w_ref, mod_ref, y_ref):
    y = _dot(o_ref[...].astype(BF16), w_ref[...])
    y_ref[...] = x_ref[...] + (1.0 + mod_ref[2]) * y


def _out_proj(x, o, w, mod, tm, tiles_per_seq=1):
    rows, k = o.shape
    return pl.pallas_call(
        _out_proj_kernel,
        grid=(rows // tm,),
        in_specs=[pl.BlockSpec((tm, D_MODEL), lambda i: (i, 0)),
                  pl.BlockSpec((tm, k), lambda i: (i, 0)),
                  _resident((k, D_MODEL)),
                  _mod_spec(mod, tm, tiles_per_seq)],
        out_specs=pl.BlockSpec((tm, D_MODEL), lambda i: (i, 0)),
        out_shape=jax.ShapeDtypeStruct((rows, D_MODEL), F32),
        compiler_params=_params("arbitrary"),
        name="out_proj",
    )(x, o, w, mod)


def _diag_fold(acc):
    rows = acc.shape[0]
    t = rows // (GROUP * KV_HEADS)
    row_g = (lax.broadcasted_iota(jnp.int32, (rows, 1), 0) // t) % KV_HEADS
    out = jnp.zeros((rows, HEAD_DIM), F32)
    for g in range(KV_HEADS):
        out = out + jnp.where(row_g == g, acc[:, g * HEAD_DIM:(g + 1) * HEAD_DIM], 0.0)
    return out


def _dec_cmp_select_kernel(q_ref, kc_ref, vc_ref, imp_ref, o_ref, bias_ref, *, t, nc, n_sel, past):
    rows = q_ref.shape[1]
    qt = KV_HEADS * t
    qpos = past + lax.broadcasted_iota(jnp.int32, (rows, 1), 0) % t
    cidx = lax.broadcasted_iota(jnp.int32, (1, kc_ref.shape[1]), 1)
    valid = (cidx * CMP_STRIDE + CMP_BLOCK - 1 <= qpos) & (cidx < nc)
    s = jnp.where(valid, _dot_nt(q_ref[0], kc_ref[0]), NEG)
    e = jnp.exp(s - jnp.max(s, axis=-1, keepdims=True))
    p = jnp.where(valid, e / jnp.sum(e, axis=-1, keepdims=True), 0.0)
    o_ref[0] = _diag_fold(_dot(p.astype(BF16), vc_ref[0]))
    p_grp = p[0:qt]
    for r in range(1, GROUP):
        p_grp = p_grp + p[r * qt:(r + 1) * qt]
    imp = _dot(p_grp.astype(BF16), imp_ref[...])
    sel = _topk_mask(_block_scores(imp, qpos[0:qt], n_sel, 1), min(SEL_TOPK, n_sel), 1)
    bias = jnp.where(sel > 0.0, 0.0, NEG).astype(BF16)
    for r in range(GROUP):
        bias_ref[0, r * qt:(r + 1) * qt, :] = bias


def _dec_cmp_select(qbd, kc, vc, t, past):
    b, rows, lanes = qbd.shape
    nseg = kc.shape[1]
    n_sel = -(-(past + t) // SEL_BLOCK)
    n_sel_pad = -(-n_sel // LANE) * LANE
    imp_m = _importance_matrix(nseg, nseg - 1, n_sel_pad, n_sel)
    return pl.pallas_call(
        functools.partial(_dec_cmp_select_kernel, t=t, nc=nseg - 1, n_sel=n_sel, past=past),
        grid=(b,),
        in_specs=[pl.BlockSpec((1, rows, lanes), lambda i: (i, 0, 0)),
                  pl.BlockSpec((1, nseg, lanes), lambda i: (i, 0, 0)),
                  pl.BlockSpec((1, nseg, lanes), lambda i: (i, 0, 0)),
                  pl.BlockSpec((nseg, n_sel_pad), lambda i: (0, 0))],
        out_specs=[pl.BlockSpec((1, rows, HEAD_DIM), lambda i: (i, 0, 0)),
                   pl.BlockSpec((1, rows, n_sel_pad), lambda i: (i, 0, 0))],
        out_shape=[jax.ShapeDtypeStruct((b, rows, HEAD_DIM), F32),
                   jax.ShapeDtypeStruct((b, rows, n_sel_pad), BF16)],
        compiler_params=_params("arbitrary"),
        name="dec_cmp_select",
    )(qbd, kc, vc, imp_m)


def _dec_attend_kernel(pt_ref, *refs, t, n_steps, past):
    del pt_ref
    pages = refs[:PAGES_PER_STEP]
    (q_ref, bias_ref, e_ref, tail_ref, wbuf_ref, wtail_ref,
     osel_ref, owin_ref, m_ref, l_ref, acc_ref) = refs[PAGES_PER_STEP:]
    step = pl.program_id(1)
    half = KV_LANES // 2
    q = q_ref[0]
    rows = q.shape[0]
    tq = lax.broadcasted_iota(jnp.int32, (rows, 1), 0) % t

    @pl.when(step == 0)
    def _():
        m_ref[...] = jnp.full(m_ref.shape, -jnp.inf, F32)
        l_ref[...] = jnp.zeros(l_ref.shape, F32)
        acc_ref[...] = jnp.zeros(acc_ref.shape, F32)

    def update(s, weighted_values):
        m = m_ref[...]
        m_new = jnp.maximum(m, jnp.max(s, axis=-1, keepdims=True))
        a = jnp.exp(m - m_new)
        p = jnp.exp(s - m_new)
        l_ref[...] = a * l_ref[...] + jnp.sum(p, axis=-1, keepdims=True)
        acc_ref[...] = a * acc_ref[...] + weighted_values(p.astype(BF16))
        m_ref[...] = m_new

    bias = bias_ref[0]
    k_t = jnp.concatenate([page[0, 0, 0].reshape(half, PAGE_SIZE) for page in pages], axis=1).astype(BF16)
    v_t = jnp.concatenate([page[0, 0, 1].reshape(half, PAGE_SIZE) for page in pages], axis=1).astype(BF16)
    update(_dot(q, k_t) + _dot_nt(bias, e_ref[...]), lambda p: _dot_nt(p, v_t))

    @pl.when(step == n_steps - 1)
    def _():
        tail = tail_ref[0]
        tpos = lax.broadcasted_iota(jnp.int32, (1, tail.shape[0]), 1)
        blk = past // SEL_BLOCK
        lane = lax.broadcasted_iota(jnp.int32, bias.shape, 1)
        b_tail = jnp.sum(jnp.where(lane == blk, bias.astype(F32), 0.0), axis=-1, keepdims=True)
        s = _dot_nt(q, tail[:, :half].astype(BF16)) + b_tail
        update(jnp.where(tpos <= tq, s, NEG), lambda p: _dot(p, tail[:, half:].astype(BF16)))
        osel_ref[0] = _diag_fold(acc_ref[...] / l_ref[...])

        wb = wbuf_ref.shape[-1]
        wk_t = wbuf_ref[0, 0, 0].reshape(half, wb).astype(BF16)
        wv_t = wbuf_ref[0, 0, 1].reshape(half, wb).astype(BF16)
        wtail = wtail_ref[0]
        ipos = lax.broadcasted_iota(jnp.int32, (1, wb), 1)
        d = (past + tq) - (past - wb + ipos)
        s1 = jnp.where((d >= 0) & (d < WINDOW), _dot(q, wk_t), NEG)
        s2 = jnp.where(tpos <= tq, _dot_nt(q, wtail[:, :half].astype(BF16)), NEG)
        m = jnp.maximum(jnp.max(s1, axis=-1, keepdims=True), jnp.max(s2, axis=-1, keepdims=True))
        p1, p2 = jnp.exp(s1 - m), jnp.exp(s2 - m)
        den = jnp.sum(p1, axis=-1, keepdims=True) + jnp.sum(p2, axis=-1, keepdims=True)
        o = _dot_nt(p1.astype(BF16), wv_t) + _dot(p2.astype(BF16), wtail[:, half:].astype(BF16))
        owin_ref[0] = _diag_fold(o / den)


def _dec_attend(cache_t, win_t, layer, page_table, qbd, bias, onehot, sel_tail, win_tail, t, past):
    b, n_pages = page_table.shape
    n_steps = n_pages // PAGES_PER_STEP
    rows, lanes = qbd.shape[1:]
    n_sel_pad = bias.shape[2]
    keys_per_step = PAGES_PER_STEP * PAGE_SIZE
    per_seq = lambda arr: pl.BlockSpec((1,) + arr.shape[1:], lambda i, s, pt: (i,) + (0,) * (arr.ndim - 1))
    grid_spec = pltpu.PrefetchScalarGridSpec(
        num_scalar_prefetch=1,
        grid=(b, n_steps),
        in_specs=[_page_spec(layer, k) for k in range(PAGES_PER_STEP)]
        + [per_seq(qbd), per_seq(bias),
           pl.BlockSpec((keys_per_step, n_sel_pad), lambda i, s, pt: (s, 0)),
           per_seq(sel_tail),
           pl.BlockSpec((1, 1) + win_t.shape[2:], lambda i, s, pt: (layer, i, 0, 0, 0, 0)),
           per_seq(win_tail)],
        out_specs=[pl.BlockSpec((1, rows, HEAD_DIM), lambda i, s, pt: (i, 0, 0)),
                   pl.BlockSpec((1, rows, HEAD_DIM), lambda i, s, pt: (i, 0, 0))],
        scratch_shapes=[pltpu.VMEM((rows, 1), F32), pltpu.VMEM((rows, 1), F32), pltpu.VMEM((rows, lanes), F32)],
    )
    return pl.pallas_call(
        functools.partial(_dec_attend_kernel, t=t, n_steps=n_steps, past=past),
        grid_spec=grid_spec,
        out_shape=[jax.ShapeDtypeStruct((b, rows, HEAD_DIM), F32)] * 2,
        compiler_params=_params("arbitrary", "arbitrary"),
        name="dec_attend",
    )(page_table, *([cache_t] * PAGES_PER_STEP), qbd, bias, onehot, sel_tail, win_t, win_tail)


def _gated_out_proj_kernel(x_ref, oc_ref, os_ref, ow_ref, gate_ref, ex_ref, w_ref, mod_ref, y_ref):
    gates = gate_ref[...]
    o = jnp.zeros(oc_ref.shape, F32)
    for k, ref in enumerate((oc_ref, os_ref, ow_ref)):
        o = o + _dot_exact(gates, ex_ref[k]) * ref[...]
    y_ref[...] = x_ref[...] + (1.0 + mod_ref[2]) * _dot(o.astype(BF16), w_ref[...])


def _gate_expansion():
    ex = np.zeros((3, LANE, NSA_Q), np.float32)
    for k in range(3):
        for h in range(N_HEADS):
            ex[k, k * N_HEADS + h, h * HEAD_DIM:(h + 1) * HEAD_DIM] = 1.0
    return jnp.asarray(ex)


def _gated_out_proj(x, oc, osel, owin, gates, w, mod):
    rows = x.shape[0]
    full = lambda n: pl.BlockSpec((rows, n), lambda i: (0, 0))
    return pl.pallas_call(
        _gated_out_proj_kernel,
        grid=(1,),
        in_specs=[full(D_MODEL), full(NSA_Q), full(NSA_Q), full(NSA_Q), full(LANE),
                  pl.BlockSpec((3, LANE, NSA_Q), lambda i: (0, 0, 0)),
                  pl.BlockSpec((NSA_Q, D_MODEL), lambda i: (0, 0)),
                  pl.BlockSpec((3, rows, D_MODEL), lambda i: (0, 0, 0))],
        out_specs=full(D_MODEL),
        out_shape=jax.ShapeDtypeStruct((rows, D_MODEL), F32),
        compiler_params=_params("arbitrary"),
        name="gated_out_proj",
    )(x, oc, osel, owin, gates, _gate_expansion(), w, mod)


def _ml_proj_kernel(x_ref, g_ref, mod_ref, w_ref, bg_ref, xm_ref, z_ref, gl_ref):
    h = _modulated_norm(x_ref[...], g_ref[...], mod_ref[0], mod_ref[1]).astype(BF16)
    proj = _dot(h, w_ref[...])
    xm_ref[...] = proj[:, :ML_INNER]
    z_ref[...] = proj[:, ML_INNER:2 * ML_INNER]
    gl = proj[:, 2 * ML_INNER:] + bg_ref[...]
    lane = lax.broadcasted_iota(jnp.int32, gl.shape, 1)
    gl_ref[...] = jnp.where(lane < ML_HEADS, gl, jax.nn.log_sigmoid(gl))


def _ml_proj(x, g, mod, w, b_gate, tm, tiles_per_seq=1):
    rows = x.shape[0]
    row = lambda n: pl.BlockSpec((tm, n), lambda i: (i, 0))
    return pl.pallas_call(
        _ml_proj_kernel,
        grid=(rows // tm,),
        in_specs=[row(D_MODEL), _resident((1, D_MODEL)), _mod_spec(mod, tm, tiles_per_seq),
                  _resident((D_MODEL, ML_N)), _resident((1, LANE))],
        out_specs=[row(ML_INNER), row(ML_INNER), row(LANE)],
        out_shape=[jax.ShapeDtypeStruct((rows, n), F32) for n in (ML_INNER, ML_INNER, LANE)],
        compiler_params=_params("arbitrary"),
        name="ml_proj",
    )(x, g, mod, w, b_gate)


def _conv_qkv(taps, cw_ref, cb_ref, wqk_ref, wv_ref, xc_ref, q_ref, k_ref, v_ref):
    xc = cb_ref[...]
    for w in range(CONV_W):
        xc = xc + taps[w] * cw_ref[w:w + 1, :]
    xc = jax.nn.silu(xc)
    xc_ref[...] = xc
    xcb = xc.astype(BF16)
    xmb = taps[CONV_W - 1].astype(BF16)
    for j in range(ML_INNER // LANE):
        sl = slice(j * LANE, (j + 1) * LANE)
        qk = _dot(xcb[:, sl], wqk_ref[j])
        q_ref[:, sl] = qk[:, :LANE].astype(q_ref.dtype)
        k_ref[:, sl] = (qk[:, LANE:] * (ML_HEAD_DIM ** -0.5)).astype(k_ref.dtype)
        v_ref[:, sl] = _dot(xmb[:, sl], wv_ref[j]).astype(v_ref.dtype)


def _ml_conv_prompt_kernel(xm_ref, halo_ref, cw_ref, cb_ref, wqk_ref, wv_ref,
                           xc_ref, q_ref, k_ref, v_ref, pad_ref, *, tm, tiles_per_seq):
    first = pl.program_id(0) % tiles_per_seq == 0
    pad_ref[0:8, :] = jnp.where(first, 0.0, halo_ref[...])
    pad_ref[8:, :] = xm_ref[...]
    taps = [pad_ref[pl.ds(8 - (CONV_W - 1) + w, tm), :] for w in range(CONV_W)]
    _conv_qkv(taps, cw_ref, cb_ref, wqk_ref, wv_ref, xc_ref, q_ref, k_ref, v_ref)


def _ml_conv_sample_kernel(t0_ref, t1_ref, t2_ref, t3_ref, cw_ref, cb_ref, wqk_ref, wv_ref,
                           xc_ref, q_ref, k_ref, v_ref):
    taps = [r[...] for r in (t0_ref, t1_ref, t2_ref, t3_ref)]
    _conv_qkv(taps, cw_ref, cb_ref, wqk_ref, wv_ref, xc_ref, q_ref, k_ref, v_ref)


def _headwise_weights(wq, wk, wv):
    def blockdiag(w):
        n_chunks = ML_INNER // LANE
        per = LANE // 4
        wc = w.reshape(n_chunks, per, 4, 4)
        eye = jnp.eye(per, dtype=w.dtype)
        return jnp.einsum('jncd,nm->jncmd', wc, eye).reshape(n_chunks, LANE, LANE)
    return jnp.concatenate([blockdiag(wq), blockdiag(wk)], axis=-1).astype(BF16), blockdiag(wv).astype(BF16)


def _ml_conv_prompt(xm, cw, cb, wqk, wv, tm, tiles_per_seq):
    rows = xm.shape[0]
    row = pl.BlockSpec((tm, ML_INNER), lambda i: (i, 0))
    out = jax.ShapeDtypeStruct((rows, ML_INNER), BF16)
    return pl.pallas_call(
        functools.partial(_ml_conv_prompt_kernel, tm=tm, tiles_per_seq=tiles_per_seq),
        grid=(rows // tm,),
        in_specs=[row, pl.BlockSpec((8, ML_INNER), lambda i: (jnp.maximum(i * (tm // 8) - 1, 0), 0)),
                  _resident(cw.shape), _resident(cb.shape), _resident(wqk.shape), _resident(wv.shape)],
        out_specs=[row] * 4,
        out_shape=[jax.ShapeDtypeStruct((rows, ML_INNER), F32), out, out, out],
        scratch_shapes=[pltpu.VMEM((tm + 8, ML_INNER), F32)],
        compiler_params=_params("arbitrary"),
        name="ml_conv_prompt",
    )(xm, xm, cw, cb, wqk, wv)


def _ml_conv_sample(taps, cw, cb, wqk, wv):
    rows = taps[0].shape[0]
    full = pl.BlockSpec((rows, ML_INNER), lambda i: (0, 0))
    const = lambda a: pl.BlockSpec(a.shape, lambda i: (0,) * a.ndim)
    out = jax.ShapeDtypeStruct((rows, ML_INNER), F32)
    return pl.pallas_call(
        _ml_conv_sample_kernel,
        grid=(1,),
        in_specs=[full] * 4 + [const(cw), const(cb), const(wqk), const(wv)],
        out_specs=[full] * 4,
        out_shape=[out] * 4,
        compiler_params=_params("arbitrary"),
        name="ml_conv_sample",
    )(*taps, cw, cb, wqk, wv)


def _ml_cell_kernel(q_ref, k_ref, v_ref, gl_ref, c0_ref, n0_ref, m0_ref,
                    h_ref, c_ref, n_ref, m_ref, *, chunk, n_chunks):
    step = pl.program_id(1)

    @pl.when(step == 0)
    def _():
        c_ref[...] = c0_ref[0]
        n_ref[...] = n0_ref[0]
        m_ref[...] = m0_ref[...]

    gl = gl_ref[0]
    ti = lax.broadcasted_iota(jnp.int32, (chunk, chunk), 0)
    si = lax.broadcasted_iota(jnp.int32, (chunk, chunk), 1)
    tri = si <= ti
    cum = _dot_exact(tri.astype(F32), gl)
    sel = (lax.broadcasted_iota(jnp.int32, (8, LANE), 0) == lax.broadcasted_iota(jnp.int32, (8, LANE), 1)).astype(F32)
    gl_rows = _dot_nt_exact(sel, gl)
    cum_rows = _dot_nt_exact(sel, cum)
    for h in range(ML_HEADS):
        hs = slice(h * ML_HEAD_DIM, (h + 1) * ML_HEAD_DIM)
        q = q_ref[0, :, hs].astype(BF16)
        k = k_ref[0, :, hs].astype(F32)
        v = v_ref[0, :, hs].astype(BF16)
        ig_col = gl[:, h:h + 1]
        ig_row = gl_rows[h:h + 1, :]
        b_col = cum[:, ML_HEADS + h:ML_HEADS + h + 1]
        b_row = cum_rows[ML_HEADS + h:ML_HEADS + h + 1, :]
        b_last = b_col[chunk - 1:chunk, :]
        m_prev = m_ref[0, 0:1, h:h + 1]
        c_prev = c_ref[0, h]
        n_prev = n_ref[0, h:h + 1, :]

        dmat = jnp.where(tri, b_col - b_row + ig_row, -jnp.inf)
        inter = b_col + m_prev
        m_t = jnp.maximum(inter, jnp.max(dmat, axis=-1, keepdims=True))
        w_int = jnp.exp(inter - m_t)
        s = _dot_nt(q, k.astype(BF16)) * jnp.exp(dmat - m_t)
        num = w_int * _dot(q, c_prev.astype(BF16)) + _dot(s.astype(BF16), v)
        den = w_int * jnp.sum(q.astype(F32) * n_prev, axis=-1, keepdims=True) + jnp.sum(s, axis=-1, keepdims=True)
        h_ref[0, :, hs] = num / jnp.maximum(jnp.abs(den), jnp.exp(-m_t))

        m_new = m_t[chunk - 1:chunk, :]
        w_dec = jnp.exp(b_last + m_prev - m_new)
        kw = k * jnp.exp(b_last - b_col + ig_col - m_new)
        c_ref[0, h] = w_dec * c_prev + _dot_tn(kw.astype(BF16), v)
        n_ref[0, h:h + 1, :] = w_dec * n_prev + jnp.sum(kw, axis=0, keepdims=True)
        m_ref[0, :, h:h + 1] = jnp.broadcast_to(m_new, (8, 1))


def _ml_cell(q, k, v, gl, c0, n0, layer, m0, chunk):
    b, t, _ = q.shape
    n_chunks = t // chunk
    seq = lambda n: pl.BlockSpec((1, chunk, n), lambda i, c: (i, c, 0))
    c_spec = pl.BlockSpec((1, ML_HEADS, ML_HEAD_DIM, ML_HEAD_DIM), lambda i, c: (i, 0, 0, 0))
    n_spec = pl.BlockSpec((1, ML_HEADS, ML_HEAD_DIM), lambda i, c: (i, 0, 0))
    m_spec = pl.BlockSpec((1, 8, LANE), lambda i, c: (i, 0, 0))
    c0_spec = pl.BlockSpec((1, 1, ML_HEADS, ML_HEAD_DIM, ML_HEAD_DIM), lambda i, c: (layer, i, 0, 0, 0))
    n0_spec = pl.BlockSpec((1, 1, ML_HEADS, ML_HEAD_DIM), lambda i, c: (layer, i, 0, 0))
    return pl.pallas_call(
        functools.partial(_ml_cell_kernel, chunk=chunk, n_chunks=n_chunks),
        grid=(b, n_chunks),
        in_specs=[seq(ML_INNER), seq(ML_INNER), seq(ML_INNER), seq(LANE), c0_spec, n0_spec, m_spec],
        out_specs=[seq(ML_INNER), c_spec, n_spec, m_spec],
        out_shape=[jax.ShapeDtypeStruct((b, t, ML_INNER), F32),
                   jax.ShapeDtypeStruct(c0.shape[1:], F32),
                   jax.ShapeDtypeStruct(n0.shape[1:], F32),
                   jax.ShapeDtypeStruct(m0.shape, F32)],
        compiler_params=_params("arbitrary", "arbitrary"),
        name="ml_cell",
    )(q, k, v, gl, c0, n0, m0)


def _ml_out_kernel(x_ref, h_ref, z_ref, xc_ref, gn_ref, skip_ref, w_ref, mod_ref, y_ref):
    hg = h_ref[...] * jax.nn.sigmoid(z_ref[...])
    parts = []
    for h in range(ML_HEADS):
        hh = hg[:, h * ML_HEAD_DIM:(h + 1) * ML_HEAD_DIM]
        parts.append(hh * lax.rsqrt(jnp.mean(hh * hh, axis=-1, keepdims=True) + NORM_EPS))
    hn = jnp.concatenate(parts, axis=-1) * gn_ref[...] + skip_ref[...] * xc_ref[...]
    y_ref[...] = x_ref[...] + (1.0 + mod_ref[2]) * _dot(hn.astype(BF16), w_ref[...])


def _ml_out(x, hcell, z, xc, gn, skip, w, mod, tm, tiles_per_seq=1):
    rows = x.shape[0]
    row = lambda n: pl.BlockSpec((tm, n), lambda i: (i, 0))
    return pl.pallas_call(
        _ml_out_kernel,
        grid=(rows // tm,),
        in_specs=[row(D_MODEL), row(ML_INNER), row(ML_INNER), row(ML_INNER),
                  _resident((1, ML_INNER)), _resident((1, ML_INNER)), _resident((ML_INNER, D_MODEL)),
                  _mod_spec(mod, tm, tiles_per_seq)],
        out_specs=row(D_MODEL),
        out_shape=jax.ShapeDtypeStruct((rows, D_MODEL), F32),
        compiler_params=_params("arbitrary"),
        name="ml_out",
    )(x, hcell, z, xc, gn, skip, w, mod)


TM_PROMPT = 512
ML_CHUNK_PROMPT = 256
ML_CHUNK_SAMPLE = 8


def _pad_cols(w, n):
    return jnp.pad(w, ((0, 0), (0, n - w.shape[1])))


def _nsa_prompt_layer(x, g, mod, w_in, w_out, cmp_w, tables, b, t):
    tiles = t // TM_PROMPT
    q, cmp_kv, gates, cmp_t, sel_t, win_t, ks_aug, kw_aug, vs_t, vw_t = _nsa_proj(
        x, g, mod, w_in, tables, TM_PROMPT, tiles, prompt_layouts=True)
    pe, w1c, _, w2b = cmp_w
    cmp = _compress_prompt(cmp_kv.reshape(b, t // CMP_STRIDE, CMP_STRIDE * KV_LANES), pe, w1c, w2b)
    nseg = cmp.shape[2]
    cmp5 = cmp.reshape(b, 2, nseg, KV_HEADS, HEAD_DIM)
    o_cmp, q_aug = _cmp_select(q.reshape(b, t, NSA_Q), cmp5[:, 0].transpose(0, 2, 1, 3),
                               cmp5[:, 1].transpose(0, 2, 3, 1), gates, tq=256)
    per_head = lambda v_t: v_t.reshape(b, KV_HEADS, HEAD_DIM, t)
    o = _flash(q_aug, ks_aug, per_head(vs_t), kw_aug, per_head(vw_t), o_cmp, gates, tq=256, tk=256)
    x = _out_proj(x, o.reshape(b * t, NSA_Q), w_out, mod, TM_PROMPT, tiles)
    kv6 = lambda a_t: jnp.transpose(a_t.reshape(b, 2, KV_HEADS, HEAD_DIM, a_t.shape[-1]), (0, 4, 1, 2, 3))
    return x, kv6(cmp_t), kv6(sel_t), kv6(win_t[:, :, t - min(WINDOW, t):])


def _paged_view(cache):
    return jnp.transpose(cache, (0, 1, 3, 4, 5, 2))


def _nsa_sample_layer(x, g, mod, w_in, w_out, cmp_w, tables, cmp_t, sel_t, win_t, layer, win_buf, page_table, b, t):
    rows = b * t
    past = page_table.shape[1] * PAGE_SIZE
    q, cmp_kv, sel_kv, win_kv, gates = _nsa_proj(x, g, mod, w_in, tables, rows)
    cmp = _compress_paged(cmp_t, layer, page_table, *cmp_w)
    q5 = (q * SCALE).reshape(b, t, KV_HEADS, GROUP, HEAD_DIM).transpose(0, 3, 2, 1, 4)
    qbd = (q5[:, :, :, :, None, :] * jnp.eye(KV_HEADS, dtype=F32)[None, None, :, None, :, None])
    qbd = qbd.reshape(b, GROUP * KV_HEADS * t, KV_HEADS * HEAD_DIM).astype(BF16)
    o_cmp, bias = _dec_cmp_select(qbd, cmp[:, 0], cmp[:, 1], t, past)
    n_sel_pad = bias.shape[2]
    onehot = (jnp.arange(past)[:, None] // SEL_BLOCK == jnp.arange(n_sel_pad)[None, :]).astype(BF16)
    pad8 = lambda a: jnp.pad(a.reshape(b, t, KV_LANES), ((0, 0), (0, 8 - t), (0, 0)))
    wb = win_buf.shape[1]
    o_sel, o_win = _dec_attend(sel_t, win_t, layer, page_table, qbd, bias, onehot, pad8(sel_kv), pad8(win_kv), t, past)
    lane_dense = lambda o: o.reshape(b, GROUP, KV_HEADS, t, HEAD_DIM).transpose(0, 3, 2, 1, 4).reshape(rows, NSA_Q)
    x = _gated_out_proj(x, lane_dense(o_cmp), lane_dense(o_sel), lane_dense(o_win), gates, w_out, mod)
    kv6 = lambda a: a.reshape(b, t, 2, KV_HEADS, HEAD_DIM)
    wkv = jnp.concatenate([win_buf, kv6(win_kv)], axis=1)
    return x, kv6(cmp_kv), kv6(sel_kv), wkv[:, wb + t - min(WINDOW, wb + t):]


def _ml_weights(w_in, b_gate, conv_w, conv_b, wq, wk, wv, skip, gn_g, w_out):
    wqk, wvb = _headwise_weights(wq, wk, wv)
    bg = jnp.pad(b_gate.reshape(1, 2 * ML_HEADS), ((0, 0), (0, LANE - 2 * ML_HEADS)))
    return dict(w_in=_pad_cols(w_in, ML_N).astype(BF16), bg=bg, cw=conv_w, cb=conv_b.reshape(1, ML_INNER),
                wqk=wqk, wv=wvb, skip=skip.reshape(1, ML_INNER), gn=gn_g.reshape(1, ML_INNER),
                w_out=w_out.astype(BF16))


def _ml_prompt_layer(x, g, mod, w, b, t):
    tiles = t // TM_PROMPT
    xm, z, gl = _ml_proj(x, g, mod, w['w_in'], w['bg'], TM_PROMPT, tiles)
    xc, q, k, v = _ml_conv_prompt(xm, w['cw'], w['cb'], w['wqk'], w['wv'], TM_PROMPT, tiles)
    seq = lambda a: a.reshape(b, t, a.shape[-1])
    c0 = jnp.zeros((1, b, ML_HEADS, ML_HEAD_DIM, ML_HEAD_DIM), F32)
    n0 = jnp.zeros((1, b, ML_HEADS, ML_HEAD_DIM), F32)
    m0 = jnp.zeros((b, 8, LANE), F32)
    hcell, c, n, m = _ml_cell(seq(q), seq(k), seq(v), seq(gl), c0, n0, 0, m0, ML_CHUNK_PROMPT)
    x = _ml_out(x, hcell.reshape(b * t, ML_INNER), z, xc, w['gn'], w['skip'], w['w_out'], mod, TM_PROMPT, tiles)
    return x, c, n, m[:, 0, :ML_HEADS], seq(xm)[:, t - (CONV_W - 1):]


def _ml_sample_layer(x, g, mod, w, conv_buf, c0, n0, layer, m0, b, t):
    rows = b * t
    xm, z, gl = _ml_proj(x, g, mod, w['w_in'], w['bg'], rows)
    xpad = jnp.concatenate([conv_buf, xm.reshape(b, t, ML_INNER)], axis=1)
    taps = [xpad[:, k:k + t].reshape(rows, ML_INNER) for k in range(CONV_W)]
    xc, q, k, v = _ml_conv_sample(taps, w['cw'], w['cb'], w['wqk'], w['wv'])
    tp = ML_CHUNK_SAMPLE
    pad = lambda a: jnp.pad(a.reshape(b, t, a.shape[-1]), ((0, 0), (0, tp - t), (0, 0)))
    lane = jnp.arange(LANE)[None, None, :]
    step = jnp.arange(tp)[None, :, None]
    glp = jnp.where(step < t, pad(gl), jnp.where(lane < ML_HEADS, NEG, 0.0))
    m0p = jnp.broadcast_to(jnp.pad(m0, ((0, 0), (0, LANE - ML_HEADS)))[:, None, :], (b, 8, LANE))
    hcell, c, n, m = _ml_cell(pad(q), pad(k), pad(v), glp, c0, n0, layer, m0p, tp)
    x = _ml_out(x, hcell[:, :t].reshape(rows, ML_INNER), z, xc, w['gn'], w['skip'], w['w_out'], mod, rows)
    return x, c, n, m[:, 0, :ML_HEADS], xpad[:, t:]


def kernel(x_prompt, x_sample, cache_cmp, cache_sel, cache_win, state_C, state_n, state_m, state_conv, page_table,
           c_prompt, c_sample, ada_w, ada_b, norm_g, ffn_w_in, ffn_w_out, final_norm_g,
           attn_w_in, attn_w_out, cmp_pos, cmp_w1, cmp_w2,
           ml_w_in, ml_b_gate, ml_conv_w, ml_conv_b, ml_wq, ml_wk, ml_wv, ml_skip, ml_gn_g, ml_w_out):
    bp, tp, _ = x_prompt.shape
    bs, ts, _ = x_sample.shape
    depth = ada_w.shape[0]
    rows_s = bs * ts
    past = page_table.shape[1] * PAGE_SIZE

    n_c = bp + bs
    c_all = jnp.pad(jnp.concatenate([c_prompt, c_sample], axis=0), ((0, -n_c % 8), (0, 0)))
    mods = _adaln(c_all, ada_w, ada_b)

    xp = x_prompt.reshape(bp * tp, D_MODEL)
    xs = x_sample.reshape(rows_s, D_MODEL)
    tables_p = _rope_tables(jnp.arange(tp))
    tables_s = tuple(jnp.tile(tb, (bs, 1)) for tb in _rope_tables(past + jnp.arange(ts)))
    tiles = tp // TM_PROMPT
    cmp_t, sel_t, win_t = _paged_view(cache_cmp), _paged_view(cache_sel), _paged_view(cache_win)

    outs_p = {k: [] for k in ('cmp', 'sel', 'win', 'c', 'n', 'm', 'conv')}
    outs_s = {k: [] for k in ('cmp', 'sel', 'win', 'c', 'n', 'm', 'conv')}
    for i in range(depth):
        mod_i = mods[i, :n_c].reshape(n_c, 9, D_MODEL)
        mp = [mod_i[:bp, 3 * s:3 * s + 3].reshape(bp, 3, 1, D_MODEL) for s in range(3)]
        ms = [jnp.repeat(mod_i[bp:, 3 * s:3 * s + 3].transpose(1, 0, 2), ts, axis=1) for s in range(3)]
        g = norm_g[i].reshape(3, 1, D_MODEL)
        w_in = ffn_w_in[i].astype(BF16)
        w_out = ffn_w_out[i].astype(BF16)

        xp = _ffn(xp, g[0], mp[0], w_in[0], w_out[0], TM_PROMPT, tiles)
        xs = _ffn(xs, g[0], ms[0], w_in[0], w_out[0], rows_s)
        j = i // 2
        if i % 2 == 0:
            a_in = _pad_cols(attn_w_in[j], NSA_N).astype(BF16)
            a_out = attn_w_out[j].astype(BF16)
            cmp_w = _compress_weights(cmp_pos[j], cmp_w1[j], cmp_w2[j])
            xp, ckv, skv, wkv = _nsa_prompt_layer(xp, g[1], mp[1], a_in, a_out, cmp_w, tables_p, bp, tp)
            xs, ckv_s, skv_s, wkv_s = _nsa_sample_layer(xs, g[1], ms[1], a_in, a_out, cmp_w, tables_s,
                                                        cmp_t, sel_t, win_t, j, cache_win[j], page_table, bs, ts)
            for d, vals in ((outs_p, (ckv, skv, wkv)), (outs_s, (ckv_s, skv_s, wkv_s))):
                for key, val in zip(('cmp', 'sel', 'win'), vals):
                    d[key].append(val)
        else:
            w = _ml_weights(ml_w_in[j], ml_b_gate[j], ml_conv_w[j], ml_conv_b[j], ml_wq[j], ml_wk[j], ml_wv[j],
                            ml_skip[j], ml_gn_g[j], ml_w_out[j])
            xp, c_p, n_p, m_p, buf_p = _ml_prompt_layer(xp, g[1], mp[1], w, bp, tp)
            xs, c_s, n_s, m_s, buf_s = _ml_sample_layer(xs, g[1], ms[1], w, state_conv[j], state_C, state_n, j,
                                                        state_m[j], bs, ts)
            for d, vals in ((outs_p, (c_p, n_p, m_p, buf_p)), (outs_s, (c_s, n_s, m_s, buf_s))):
                for key, val in zip(('c', 'n', 'm', 'conv'), vals):
                    d[key].append(val)
        xp = _ffn(xp, g[2], mp[2], w_in[1], w_out[1], TM_PROMPT, tiles)
        xs = _ffn(xs, g[2], ms[2], w_in[1], w_out[1], rows_s)

    fg = final_norm_g.reshape(1, D_MODEL)
    y_prompt = _final_norm(xp, fg, TM_PROMPT).reshape(bp, tp, D_MODEL)
    y_sample = _final_norm(xs, fg, rows_s).reshape(bs, ts, D_MODEL)
    st = lambda d, key: jnp.stack(d[key])
    return (y_prompt, y_sample,
            st(outs_p, 'cmp'), st(outs_p, 'sel'), st(outs_p, 'win'),
            st(outs_p, 'c'), st(outs_p, 'n'), st(outs_p, 'm'), st(outs_p, 'conv'),
            st(outs_s, 'cmp'), st(outs_s, 'sel'), st(outs_s, 'win'),
            st(outs_s, 'c'), st(outs_s, 'n'), st(outs_s, 'm'), st(outs_s, 'conv'))
```

```python
import functools
import math

import numpy as np
import jax
import jax.numpy as jnp
from jax import lax
from jax.experimental import pallas as pl
from jax.experimental.pallas import tpu as pltpu

D_MODEL = 1024
D_FF = 2816
N_HEADS = 16
HEAD_DIM = 64
KV_HEADS = 4
GROUP = 4
ROT_DIM = 16
ROPE_THETA = 500000.0
CMP_BLOCK = 32
CMP_STRIDE = 16
CMP_HIDDEN = 128
SEL_BLOCK = 64
SEL_TOPK = 16
WINDOW = 512
PAGE_SIZE = 128
KV_LANES = 2 * KV_HEADS * HEAD_DIM
NSA_Q = N_HEADS * HEAD_DIM
NSA_N = 2688
ML_INNER = 2048
ML_HEADS = 4
ML_HEAD_DIM = 512
ML_N = 4224
CONV_W = 4
NORM_EPS = 1e-6
NEG = -1e30
BIG = 1e30
SCALE = HEAD_DIM ** -0.5
LOG2E = 1.4426950408889634
V_ROWS = HEAD_DIM + 16

F32 = jnp.float32
BF16 = jnp.bfloat16
LANE = 128
VMEM_LIMIT_BYTES = 56 * 1024 * 1024


def _params(*sem):
    return pltpu.CompilerParams(dimension_semantics=sem, vmem_limit_bytes=VMEM_LIMIT_BYTES)


def _resident(shape):
    nd = len(shape)
    return pl.BlockSpec(shape, lambda *_: (0,) * nd, pipeline_mode=pl.Buffered(1))


def _dot(a, b):
    return jnp.dot(a, b, preferred_element_type=F32)


def _dot_nt(a, b):
    return lax.dot_general(a, b, (((1,), (1,)), ((), ())), preferred_element_type=F32)


def _dot_tn(a, b):
    return lax.dot_general(a, b, (((0,), (0,)), ((), ())), preferred_element_type=F32)


def _dot_exact(a, b):
    return jnp.dot(a, b, preferred_element_type=F32, precision=lax.Precision.HIGHEST)


def _dot_nt_exact(a, b):
    return lax.dot_general(a, b, (((1,), (1,)), ((), ())), preferred_element_type=F32,
                           precision=lax.Precision.HIGHEST)


def _modulated_norm(x, g, shift, scale):
    y = x * lax.rsqrt(jnp.mean(x * x, axis=-1, keepdims=True) + NORM_EPS)
    return (y * g) * (1.0 + scale) + shift


def _mod_spec(mod, tm, tiles_per_seq):
    if mod.ndim == 4:
        return pl.BlockSpec((None, 3, 1, D_MODEL), lambda i: (i // tiles_per_seq, 0, 0, 0))
    return pl.BlockSpec((3, tm, D_MODEL), lambda i: (0, i, 0))


def _adaln_kernel(c_ref, w_ref, b_ref, o_ref):
    a = jax.nn.silu(c_ref[...]).astype(BF16)
    o_ref[0] = _dot(a, w_ref[0].astype(BF16)) + b_ref[0]


def _adaln(c_all, ada_w, ada_b):
    depth, _, n = ada_w.shape
    rows = c_all.shape[0]
    tn = 2304
    return pl.pallas_call(
        _adaln_kernel,
        grid=(depth, n // tn),
        in_specs=[pl.BlockSpec((rows, D_MODEL), lambda l, j: (0, 0)),
                  pl.BlockSpec((1, D_MODEL, tn), lambda l, j: (l, 0, j)),
                  pl.BlockSpec((1, 1, tn), lambda l, j: (l, 0, j))],
        out_specs=pl.BlockSpec((1, rows, tn), lambda l, j: (l, 0, j)),
        out_shape=jax.ShapeDtypeStruct((depth, rows, n), F32),
        compiler_params=_params("arbitrary", "arbitrary"),
        name="adaln",
    )(c_all, ada_w, ada_b.reshape(depth, 1, n))


MXU_DIM = 256
FF_CHUNKS = (6 * MXU_DIM, 5 * MXU_DIM)
assert sum(FF_CHUNKS) == D_FF


def _ffn_kernel(x_ref, g_ref, mod_ref, win_ref, wout_ref, o_ref):
    x = x_ref[...]
    h = _modulated_norm(x, g_ref[...], mod_ref[0], mod_ref[1]).astype(BF16)
    acc = jnp.zeros(x.shape, F32)
    lo = 0
    for width in FF_CHUNKS:
        gate = _dot(h, win_ref[:, lo:lo + width])
        up = _dot(h, win_ref[:, D_FF + lo:D_FF + lo + width])
        act = (jax.nn.silu(gate) * up).astype(BF16)
        acc = acc + _dot(act, wout_ref[lo:lo + width, :])
        lo += width
    o_ref[...] = x + 0.5 * (1.0 + mod_ref[2]) * acc


def _ffn(x, g, mod, w_in, w_out, tm, tiles_per_seq=1):
    rows = x.shape[0]
    return pl.pallas_call(
        _ffn_kernel,
        grid=(rows // tm,),
        in_specs=[pl.BlockSpec((tm, D_MODEL), lambda i: (i, 0)),
                  _resident((1, D_MODEL)),
                  _mod_spec(mod, tm, tiles_per_seq),
                  _resident((D_MODEL, 2 * D_FF)),
                  _resident((D_FF, D_MODEL))],
        out_specs=pl.BlockSpec((tm, D_MODEL), lambda i: (i, 0)),
        out_shape=jax.ShapeDtypeStruct((rows, D_MODEL), F32),
        compiler_params=_params("arbitrary"),
        name="ffn_half",
    )(x, g, mod, w_in, w_out)


def _final_norm_kernel(x_ref, g_ref, o_ref):
    x = x_ref[...]
    o_ref[...] = x * lax.rsqrt(jnp.mean(x * x, axis=-1, keepdims=True) + NORM_EPS) * g_ref[...]


def _final_norm(x, g, tm):
    rows = x.shape[0]
    return pl.pallas_call(
        _final_norm_kernel,
        grid=(rows // tm,),
        in_specs=[pl.BlockSpec((tm, D_MODEL), lambda i: (i, 0)), _resident((1, D_MODEL))],
        out_specs=pl.BlockSpec((tm, D_MODEL), lambda i: (i, 0)),
        out_shape=jax.ShapeDtypeStruct((rows, D_MODEL), F32),
        compiler_params=_params("arbitrary"),
        name="final_norm",
    )(x, g)


def _rope_tables(pos):
    half = ROT_DIM // 2
    inv = ROPE_THETA ** (-jnp.arange(0, ROT_DIM, 2, dtype=F32) / ROT_DIM)
    ang = pos.astype(F32)[:, None] * inv[None, :]
    cos, sin = jnp.cos(ang), jnp.sin(ang)
    n = pos.shape[0]
    pad = jnp.zeros((n, HEAD_DIM - ROT_DIM), F32)
    zeros = jnp.zeros((n, half), F32)
    c = jnp.concatenate([cos, cos, pad + 1.0], axis=1)
    s1 = jnp.concatenate([-sin, zeros, pad], axis=1)
    s2 = jnp.concatenate([zeros, sin, pad], axis=1)
    return tuple(jnp.tile(t, (1, LANE // HEAD_DIM)) for t in (c, s1, s2))


def _nsa_proj_kernel(x_ref, g_ref, mod_ref, w_ref, c_ref, s1_ref, s2_ref, *out_refs, prompt_layouts, tiles_per_seq):
    h = _modulated_norm(x_ref[...], g_ref[...], mod_ref[0], mod_ref[1]).astype(BF16)
    proj = _dot(h, w_ref[...])
    c, s1, s2 = c_ref[...], s1_ref[...], s2_ref[...]
    tm = proj.shape[0]
    half = KV_LANES // 2

    def rope(lo, n):
        chunks = []
        for j in range(n // LANE):
            v = proj[:, lo + j * LANE:lo + (j + 1) * LANE]
            chunks.append(v * c + pltpu.roll(v, LANE - ROT_DIM // 2, 1) * s1 + pltpu.roll(v, ROT_DIM // 2, 1) * s2)
        return jnp.concatenate(chunks, axis=1)

    q = rope(0, NSA_Q)
    branches = []
    for br in range(3):
        base = NSA_Q + br * KV_LANES
        branches.append(jnp.concatenate([rope(base, half), proj[:, base + half:base + KV_LANES]], axis=1))
    gates = jax.nn.sigmoid(proj[:, NSA_Q + 3 * KV_LANES:])

    if not prompt_layouts:
        q_ref, cmp_ref, sel_ref, win_ref, gate_ref = out_refs
        q_ref[...] = q
        gate_ref[...] = gates
        for ref, kv in zip((cmp_ref, sel_ref, win_ref), branches):
            ref[...] = kv
        return

    q_ref, cmp_ref, gate_ref, cmp_t_ref, sel_t_ref, win_t_ref, ks_ref, kw_ref, vst_ref, vwt_ref = out_refs
    q_ref[...] = q
    gate_ref[...] = gates
    cmp_ref[...] = branches[0]
    transposed = [kv.T for kv in branches]
    for ref, kv_t in zip((cmp_t_ref, sel_t_ref, win_t_ref), transposed):
        ref[0] = kv_t
    ones_pad = (lax.broadcasted_iota(jnp.int32, (V_ROWS - HEAD_DIM, tm), 0) == 0).astype(F32)
    for ref, kv_t in ((vst_ref, transposed[1]), (vwt_ref, transposed[2])):
        rows = []
        for g in range(KV_HEADS):
            rows += [kv_t[half + g * HEAD_DIM:half + (g + 1) * HEAD_DIM], ones_pad]
        ref[0] = jnp.concatenate(rows, axis=0).astype(BF16)
    pos = (pl.program_id(0) % tiles_per_seq) * tm + lax.broadcasted_iota(jnp.int32, (tm, HEAD_DIM), 0)
    onehot = (pos // SEL_BLOCK == lax.broadcasted_iota(jnp.int32, (tm, HEAD_DIM), 1)).astype(BF16)
    zeros = jnp.zeros((tm, HEAD_DIM), BF16)
    for g in range(KV_HEADS):
        sl = slice(g * HEAD_DIM, (g + 1) * HEAD_DIM)
        ks_ref[0, g, :, 0:HEAD_DIM] = branches[1][:, sl].astype(BF16)
        ks_ref[0, g, :, HEAD_DIM:] = onehot
        kw_ref[0, g, :, 0:HEAD_DIM] = branches[2][:, sl].astype(BF16)
        kw_ref[0, g, :, HEAD_DIM:] = zeros


def _nsa_proj(x, g, mod, w, tables, tm, tiles_per_seq=1, prompt_layouts=False):
    rows = x.shape[0]
    trows = tables[0].shape[0]
    tspec = pl.BlockSpec((tm, LANE), lambda i: (i % (trows // tm), 0))
    row = lambda n: pl.BlockSpec((tm, n), lambda i: (i, 0))
    f32 = lambda *shape: jax.ShapeDtypeStruct(shape, F32)
    bf16 = lambda *shape: jax.ShapeDtypeStruct(shape, BF16)
    if prompt_layouts:
        b, t = rows // (tiles_per_seq * tm), tiles_per_seq * tm
        seq_t = lambda n: pl.BlockSpec((1, n, tm), lambda i: (i // tiles_per_seq, 0, i % tiles_per_seq))
        heads = pl.BlockSpec((1, KV_HEADS, tm, 2 * HEAD_DIM), lambda i: (i // tiles_per_seq, 0, i % tiles_per_seq, 0))
        out_specs = [row(NSA_Q), row(KV_LANES), row(LANE), seq_t(KV_LANES), seq_t(KV_LANES), seq_t(KV_LANES),
                     heads, heads, seq_t(KV_HEADS * V_ROWS), seq_t(KV_HEADS * V_ROWS)]
        out_shape = [f32(rows, NSA_Q), f32(rows, KV_LANES), f32(rows, LANE),
                     f32(b, KV_LANES, t), f32(b, KV_LANES, t), f32(b, KV_LANES, t),
                     bf16(b, KV_HEADS, t, 2 * HEAD_DIM), bf16(b, KV_HEADS, t, 2 * HEAD_DIM),
                     bf16(b, KV_HEADS * V_ROWS, t), bf16(b, KV_HEADS * V_ROWS, t)]
    else:
        out_specs = [row(NSA_Q), row(KV_LANES), row(KV_LANES), row(KV_LANES), row(LANE)]
        out_shape = [f32(rows, n) for n in (NSA_Q, KV_LANES, KV_LANES, KV_LANES, LANE)]
    return pl.pallas_call(
        functools.partial(_nsa_proj_kernel, prompt_layouts=prompt_layouts, tiles_per_seq=tiles_per_seq),
        grid=(rows // tm,),
        in_specs=[row(D_MODEL), _resident((1, D_MODEL)), _mod_spec(mod, tm, tiles_per_seq),
                  _resident((D_MODEL, NSA_N)), tspec, tspec, tspec],
        out_specs=out_specs,
        out_shape=out_shape,
        compiler_params=_params("arbitrary"),
        name="nsa_proj",
    )(x, g, mod, w, *tables)


def _compress_finish(seg_proj, pe_ref, w1_ref, w2_ref, out_ref, nseg):
    row = lax.broadcasted_iota(jnp.int32, (nseg, 1), 0)
    for s in range(2):
        bias = jnp.zeros((8, 2 * CMP_HIDDEN), F32)
        lane = lax.broadcasted_iota(jnp.int32, (8, 2 * CMP_HIDDEN), 1)
        for l in range(CMP_STRIDE):
            pa = jnp.broadcast_to(pe_ref[s, l:l + 1, :], (8, HEAD_DIM)).astype(BF16)
            pb = jnp.broadcast_to(pe_ref[s, CMP_STRIDE + l:CMP_STRIDE + l + 1, :], (8, HEAD_DIM)).astype(BF16)
            bias = bias + jnp.where(lane < CMP_HIDDEN, _dot(pa, w1_ref[s, l]), _dot(pb, w1_ref[s, l]))
        bias = bias[0:1]
        for g in range(KV_HEADS):
            a = seg_proj(s, g) + bias
            first, second = a[:, :CMP_HIDDEN], a[:, CMP_HIDDEN:]
            hid = jax.nn.gelu(first + pltpu.roll(second, nseg - 1, 0))
            o = _dot(hid.astype(BF16), w2_ref[s])
            o = jnp.where(row < nseg - 1, o, 0.0)
            out_ref[s, :, g * HEAD_DIM:(g + 1) * HEAD_DIM] = o.astype(out_ref.dtype)


def _compress_prompt_kernel(x_ref, pe_ref, w1_ref, w2_ref, out_ref, acc_ref, *, nseg):
    for l in range(CMP_STRIDE):
        xl = x_ref[0, :, l * KV_LANES:(l + 1) * KV_LANES].astype(BF16)
        for sg in range(2 * KV_HEADS):
            d = _dot(xl[:, sg * HEAD_DIM:(sg + 1) * HEAD_DIM], w1_ref[sg // KV_HEADS, l])
            if l == 0:
                acc_ref[sg] = d
            else:
                acc_ref[sg] += d
    _compress_finish(lambda s, g: acc_ref[s * KV_HEADS + g], pe_ref, w1_ref, w2_ref, out_ref.at[0], nseg)


def _compress_weights(pos_emb, w1, w2):
    pe = jnp.transpose(pos_emb, (1, 0, 2))
    w1c = jnp.concatenate([w1[:, :CMP_STRIDE], w1[:, CMP_STRIDE:]], axis=-1).astype(BF16)
    zero = jnp.zeros_like(w1c)
    w1bd = jnp.concatenate([jnp.concatenate([w1c, zero], axis=-1), jnp.concatenate([zero, w1c], axis=-1)], axis=-2)
    w1bd = w1bd.reshape(2, CMP_STRIDE // 2, 4 * HEAD_DIM, 4 * CMP_HIDDEN)
    return pe, w1c, w1bd, w2.astype(BF16)


def _compress_prompt(cmp_kv, pe, w1c, w2b):
    b, nseg, seg_lanes = cmp_kv.shape
    return pl.pallas_call(
        functools.partial(_compress_prompt_kernel, nseg=nseg),
        grid=(b,),
        in_specs=[pl.BlockSpec((1, nseg, seg_lanes), lambda i: (i, 0, 0)),
                  _resident(pe.shape), _resident(w1c.shape), _resident(w2b.shape)],
        out_specs=pl.BlockSpec((1, 2, nseg, KV_LANES // 2), lambda i: (i, 0, 0, 0)),
        out_shape=jax.ShapeDtypeStruct((b, 2, nseg, KV_LANES // 2), BF16),
        scratch_shapes=[pltpu.VMEM((2 * KV_HEADS, nseg, 2 * CMP_HIDDEN), F32)],
        compiler_params=_params("arbitrary"),
        name="compress_prompt",
    )(cmp_kv, pe, w1c, w2b)


PAGES_PER_STEP = 16
SEGS_PER_PAGE = PAGE_SIZE // CMP_STRIDE


def _page_spec(layer, k):
    return pl.BlockSpec((1, 1, 2, KV_HEADS, HEAD_DIM, PAGE_SIZE),
                        lambda i, s, pt: (layer, pt[i, s * PAGES_PER_STEP + k], 0, 0, 0, 0))


def _compress_paged_kernel(pt_ref, *refs, nseg, n_steps):
    del pt_ref
    pages = refs[:PAGES_PER_STEP]
    pe_ref, w1_ref, w1bd_ref, w2_ref, out_ref, rows_ref, acc_ref = refs[PAGES_PER_STEP:]
    step = pl.program_id(1)
    pair = 2 * HEAD_DIM
    for k, page in enumerate(pages):
        for s in range(2):
            for gp in range(KV_HEADS // 2):
                tile = page[0, 0, s, 2 * gp:2 * gp + 2].reshape(pair, PAGE_SIZE)
                rows_ref[s * 2 + gp, k * PAGE_SIZE:(k + 1) * PAGE_SIZE, :] = tile.T
    segs = PAGES_PER_STEP * SEGS_PER_PAGE
    seg0 = pl.multiple_of(step * segs, segs)
    for sgp in range(KV_HEADS):
        d = None
        for l in range(0, CMP_STRIDE, 2):
            xl = jnp.concatenate([rows_ref[sgp, pl.ds(l + i, segs, stride=CMP_STRIDE), :] for i in range(2)],
                                 axis=1).astype(BF16)
            dl = _dot(xl, w1bd_ref[sgp // 2, l // 2])
            d = dl if d is None else d + dl
        acc_ref[sgp, pl.ds(seg0, segs), :] = d

    @pl.when(step == n_steps - 1)
    def _():
        width = 2 * CMP_HIDDEN

        def seg_proj(s, g):
            return acc_ref[s * 2 + g // 2, :, (g % 2) * width:(g % 2 + 1) * width]

        _compress_finish(seg_proj, pe_ref, w1_ref, w2_ref, out_ref.at[0], nseg)


def _compress_paged(cache_t, layer, page_table, pe, w1c, w1bd, w2b):
    b, n_pages = page_table.shape
    n_steps = n_pages // PAGES_PER_STEP
    nseg = n_pages * SEGS_PER_PAGE
    const = lambda a: pl.BlockSpec(a.shape, lambda i, s, pt: (0,) * a.ndim)
    grid_spec = pltpu.PrefetchScalarGridSpec(
        num_scalar_prefetch=1,
        grid=(b, n_steps),
        in_specs=[_page_spec(layer, k) for k in range(PAGES_PER_STEP)] + [const(pe), const(w1c), const(w1bd), const(w2b)],
        out_specs=pl.BlockSpec((1, 2, nseg, KV_LANES // 2), lambda i, s, pt: (i, 0, 0, 0)),
        scratch_shapes=[pltpu.VMEM((KV_HEADS, PAGES_PER_STEP * PAGE_SIZE, 2 * HEAD_DIM), F32),
                        pltpu.VMEM((KV_HEADS, nseg, 4 * CMP_HIDDEN), F32)],
    )
    return pl.pallas_call(
        functools.partial(_compress_paged_kernel, nseg=nseg, n_steps=n_steps),
        grid_spec=grid_spec,
        out_shape=jax.ShapeDtypeStruct((b, 2, nseg, KV_LANES // 2), BF16),
        compiler_params=_params("arbitrary", "arbitrary"),
        name="compress_paged",
    )(page_table, *([cache_t] * PAGES_PER_STEP), pe, w1c, w1bd, w2b)


def _importance_matrix(nc_pad, nc, n_sel_pad, n_sel):
    i = np.arange(nc_pad)[:, None]
    j = np.arange(n_sel_pad)[None, :]
    overlap = np.maximum(np.minimum(i * CMP_STRIDE + CMP_BLOCK, (j + 1) * SEL_BLOCK)
                         - np.maximum(i * CMP_STRIDE, j * SEL_BLOCK), 0)
    m = overlap / CMP_STRIDE * (i < nc) * (j < n_sel)
    return jnp.asarray(m, dtype=BF16)


def _topk_mask(score, k, axis):
    n = score.shape[axis]
    idx = lax.broadcasted_iota(jnp.int32, score.shape, axis).astype(F32)

    def body(_, carry):
        sc, sel = carry
        m = jnp.max(sc, axis=axis, keepdims=True)
        first = jnp.min(jnp.where(sc == m, idx, float(n)), axis=axis, keepdims=True)
        hit = idx == first
        return jnp.where(hit, -jnp.inf, sc), jnp.where(hit, 1.0, sel)

    _, sel = lax.fori_loop(0, k, body, (score, jnp.zeros(score.shape, F32)))
    return sel


def _block_scores(imp, qpos, n_valid, axis):
    jj = lax.broadcasted_iota(jnp.int32, imp.shape, axis)
    cur = qpos // SEL_BLOCK
    forced = (jj == 0) | (jj == cur) | (jj == cur - 1)
    score = jnp.where(forced, BIG, jnp.where(jj <= cur, imp, NEG))
    return jnp.where(jj < n_valid, score, -jnp.inf)


def _cmp_select_kernel(q_ref, kc_ref, vct_ref, impt_ref, gate_ref, o_ref, qa_ref, *, tq, nc, n_sel):
    q0 = pl.program_id(1) * tq
    rows = GROUP * tq
    qpos = q0 + lax.broadcasted_iota(jnp.int32, (1, rows), 1) % tq
    cidx = lax.broadcasted_iota(jnp.int32, (kc_ref.shape[2], 1), 0)
    valid = (cidx * CMP_STRIDE + CMP_BLOCK - 1 <= qpos) & (cidx < nc)
    tpos = q0 + lax.broadcasted_iota(jnp.int32, (1, tq), 1)
    gates_t = gate_ref[...].T
    q2 = (q_ref[0] * (SCALE * LOG2E)).astype(BF16)
    heads = [q2[:, h * HEAD_DIM:(h + 1) * HEAD_DIM] for h in range(N_HEADS)]
    scores = [_dot_nt(kc_ref[0, g], jnp.concatenate(heads[g * GROUP:(g + 1) * GROUP], axis=0))
              for g in range(KV_HEADS)]
    o_t, imp = [], []
    for g in range(KV_HEADS):
        s = jnp.where(valid, scores[g], NEG)
        e = jnp.exp2(s - jnp.max(s, axis=0, keepdims=True))
        p = jnp.where(valid, e / jnp.sum(e, axis=0, keepdims=True), 0.0)
        o = _dot(vct_ref[0, g], p.astype(BF16))
        p_grp = p[:, 0:tq]
        for r in range(1, GROUP):
            p_grp = p_grp + p[:, r * tq:(r + 1) * tq]
        imp.append(_dot(impt_ref[...], p_grp.astype(BF16)))
        for r in range(GROUP):
            h = g * GROUP + r
            o_t.append(o[:, r * tq:(r + 1) * tq] * gates_t[h:h + 1, :])
    o_ref[0] = jnp.concatenate(o_t, axis=0).T
    imp = jnp.concatenate(imp, axis=1)
    tpos = jnp.concatenate([tpos] * KV_HEADS, axis=1)
    sel = _topk_mask(_block_scores(imp, tpos, n_sel, 0), min(SEL_TOPK, n_sel), 0)
    bias_t = jnp.where(sel > 0.0, 0.0, NEG)
    bias = jnp.concatenate([bias_t[:, g * tq:(g + 1) * tq] for g in range(KV_HEADS)], axis=0).T.astype(BF16)
    for h in range(N_HEADS):
        g = h // GROUP
        qa_ref[0, h, :, 0:HEAD_DIM] = heads[h]
        qa_ref[0, h, :, HEAD_DIM:] = bias[:, g * HEAD_DIM:(g + 1) * HEAD_DIM]


def _cmp_select(q, kch, vct, gates, tq):
    b, t, _ = q.shape
    nseg = kch.shape[2]
    n_sel = t // SEL_BLOCK
    assert n_sel <= HEAD_DIM
    imp_t = _importance_matrix(nseg, nseg - 1, HEAD_DIM, n_sel).T
    return pl.pallas_call(
        functools.partial(_cmp_select_kernel, tq=tq, nc=nseg - 1, n_sel=n_sel),
        grid=(b, t // tq),
        in_specs=[pl.BlockSpec((1, tq, NSA_Q), lambda i, j: (i, j, 0)),
                  pl.BlockSpec((1, KV_HEADS, nseg, HEAD_DIM), lambda i, j: (i, 0, 0, 0)),
                  pl.BlockSpec((1, KV_HEADS, HEAD_DIM, nseg), lambda i, j: (i, 0, 0, 0)),
                  pl.BlockSpec((HEAD_DIM, nseg), lambda i, j: (0, 0)),
                  pl.BlockSpec((tq, LANE), lambda i, j: (i * (t // tq) + j, 0))],
        out_specs=[pl.BlockSpec((1, tq, NSA_Q), lambda i, j: (i, j, 0)),
                   pl.BlockSpec((1, N_HEADS, tq, 2 * HEAD_DIM), lambda i, j: (i, 0, j, 0))],
        out_shape=[jax.ShapeDtypeStruct((b, t, NSA_Q), F32),
                   jax.ShapeDtypeStruct((b, N_HEADS, t, 2 * HEAD_DIM), BF16)],
        compiler_params=_params("arbitrary", "arbitrary"),
        name="cmp_select",
    )(q, kch, vct, imp_t, gates)


def _flash_kernel(qa_ref, ks_ref, vst_ref, kw_ref, vwt_ref, oc_ref, gate_ref, o_ref, *, tq, tk):
    g = pl.program_id(1)
    q0 = pl.program_id(2) * tq
    heads = [qa_ref[0, r] for r in range(GROUP)]

    def softmax_step(m, acc, s, v_t):
        m_new = jnp.maximum(m, jnp.max(s, axis=0, keepdims=True))
        a = jnp.exp2(m - m_new)
        p = jnp.exp2(s - m_new)
        return m_new, a * acc + _dot(v_t, p.astype(BF16))

    def tile_pair(j, carry):
        carry = list(carry)
        k0 = pl.multiple_of(j * (2 * tk), 2 * tk)
        starts = [k0, k0 + tk]
        scores = [[_dot_nt(ks_ref[0, 0, pl.ds(st, tk), :], heads[r]) for r in range(GROUP)] for st in starts]
        for st, s_tile in zip(starts, scores):
            v_t = vst_ref[0, 0, :, pl.ds(st, tk)]
            for r in range(GROUP):
                carry[r] = softmax_step(*carry[r], s_tile[r], v_t)
        return tuple(carry)

    init = tuple((jnp.full((1, tq), -jnp.inf, F32), jnp.zeros((V_ROWS, tq), F32)) for _ in range(GROUP))
    carry = lax.fori_loop(0, q0 // tq, tile_pair, init)

    k_loc = lax.broadcasted_iota(jnp.int32, (tk, 1), 0)
    q_loc = lax.broadcasted_iota(jnp.int32, (1, tk), 1)
    causal, older = k_loc <= q_loc, k_loc > q_loc
    plan = []
    for r in range(GROUP):
        for half in range(2):
            qs = q0 + half * tk
            plan += [(r, half, 0, q0 + i * tk, 'causal' if i == half else None) for i in range(half + 1)]
            plan += [(r, half, 1, qs - 2 * tk, 'older'), (r, half, 1, qs - tk, 'valid'), (r, half, 1, qs, 'causal')]
    q_half = [[heads[r][0:tk], heads[r][tk:tq]] for r in range(GROUP)]
    k_refs, v_refs = (ks_ref, kw_ref), (vst_ref, vwt_ref)
    clamp = lambda st: pl.multiple_of(jnp.maximum(st, 0), tk)
    scores = [_dot_nt(k_refs[br][0, 0, pl.ds(clamp(st), tk), :], q_half[r][half]) for r, half, br, st, _ in plan]
    state = {}
    for r in range(GROUP):
        m, acc = carry[r]
        for half in range(2):
            cols = slice(half * tk, (half + 1) * tk)
            state[r, half, 0] = (m[:, cols], acc[:, cols])
            state[r, half, 1] = (jnp.full((1, tk), -jnp.inf, F32), jnp.zeros((V_ROWS, tk), F32))
    for (r, half, br, st, kind), s in zip(plan, scores):
        if kind == 'causal':
            s = jnp.where(causal, s, NEG)
        elif kind == 'older':
            s = jnp.where(older & (st >= 0), s, NEG)
        elif kind == 'valid':
            s = jnp.where(st >= 0, s, NEG)
        state[r, half, br] = softmax_step(*state[r, half, br], s, v_refs[br][0, 0, :, pl.ds(clamp(st), tk)])

    gates_t = gate_ref[...].T
    sub = lax.broadcasted_iota(jnp.int32, gates_t.shape, 0)
    parts = []
    for r in range(GROUP):
        acc = jnp.concatenate([state[r, half, 0][1] for half in range(2)], axis=1)
        acc_w = jnp.concatenate([state[r, half, 1][1] for half in range(2)], axis=1)
        g_sel = jnp.sum(jnp.where(sub == N_HEADS + g * GROUP + r, gates_t, 0.0), axis=0, keepdims=True)
        g_win = jnp.sum(jnp.where(sub == 2 * N_HEADS + g * GROUP + r, gates_t, 0.0), axis=0, keepdims=True)
        d_sel, d_win = acc[HEAD_DIM:HEAD_DIM + 1], acc_w[HEAD_DIM:HEAD_DIM + 1]
        parts.append(acc[:HEAD_DIM] * (g_sel / d_sel) + acc_w[:HEAD_DIM] * (g_win / d_win))
    o = oc_ref[0] + jnp.concatenate(parts, axis=0).T
    o_ref[0] = o.astype(o_ref.dtype)


def _flash(qa, ks, vst, kw, vwt, oc, gates, tq, tk):
    assert tq == 2 * tk == WINDOW
    b, _, t, _ = qa.shape
    grp_lanes = GROUP * HEAD_DIM
    keys = pl.BlockSpec((1, 1, t, 2 * HEAD_DIM), lambda i, g, j: (i, g, 0, 0))
    vals = pl.BlockSpec((1, 1, V_ROWS, t), lambda i, g, j: (i, g, 0, 0))
    return pl.pallas_call(
        functools.partial(_flash_kernel, tq=tq, tk=tk),
        grid=(b, KV_HEADS, t // tq),
        in_specs=[pl.BlockSpec((1, GROUP, tq, 2 * HEAD_DIM), lambda i, g, j: (i, g, j, 0)),
                  keys, vals, keys, vals,
                  pl.BlockSpec((1, tq, grp_lanes), lambda i, g, j: (i, j, g)),
                  pl.BlockSpec((tq, LANE), lambda i, g, j: (i * (t // tq) + j, 0))],
        out_specs=pl.BlockSpec((1, tq, grp_lanes), lambda i, g, j: (i, j, g)),
        out_shape=jax.ShapeDtypeStruct((b, t, NSA_Q), BF16),
        compiler_params=_params("arbitrary", "arbitrary", "arbitrary"),
        name="nsa_flash",
    )(qa, ks, vst, kw, vwt, oc, gates)


def _out_proj_kernel(x_ref, o_ref, w_ref, mod_ref, y_ref):
    y = _dot(o_ref[...].astype(BF16), w_ref[...])
    y_ref[...] = x_ref[...] + (1.0 + mod_ref[2]) * y


def _out_proj(x, o, w, mod, tm, tiles_per_seq=1):
    rows, k = o.shape
    return pl.pallas_call(
        _out_proj_kernel,
        grid=(rows // tm,),
        in_specs=[pl.BlockSpec((tm, D_MODEL), lambda i: (i, 0)),
                  pl.BlockSpec((tm, k), lambda i: (i, 0)),
                  _resident((k, D_MODEL)),
                  _mod_spec(mod, tm, tiles_per_seq)],
        out_specs=pl.BlockSpec((tm, D_MODEL), lambda i: (i, 0)),
        out_shape=jax.ShapeDtypeStruct((rows, D_MODEL), F32),
        compiler_params=_params("arbitrary"),
        name="out_proj",
    )(x, o, w, mod)


def _diag_fold(acc):
    rows = acc.shape[0]
    t = rows // (GROUP * KV_HEADS)
    row_g = (lax.broadcasted_iota(jnp.int32, (rows, 1), 0) // t) % KV_HEADS
    out = jnp.zeros((rows, HEAD_DIM), F32)
    for g in range(KV_HEADS):
        out = out + jnp.where(row_g == g, acc[:, g * HEAD_DIM:(g + 1) * HEAD_DIM], 0.0)
    return out


def _dec_cmp_select_kernel(q_ref, kc_ref, vc_ref, imp_ref, o_ref, bias_ref, *, t, nc, n_sel, past):
    rows = q_ref.shape[1]
    qt = KV_HEADS * t
    qpos = past + lax.broadcasted_iota(jnp.int32, (rows, 1), 0) % t
    cidx = lax.broadcasted_iota(jnp.int32, (1, kc_ref.shape[1]), 1)
    valid = (cidx * CMP_STRIDE + CMP_BLOCK - 1 <= qpos) & (cidx < nc)
    s = jnp.where(valid, _dot_nt(q_ref[0], kc_ref[0]), NEG)
    e = jnp.exp(s - jnp.max(s, axis=-1, keepdims=True))
    p = jnp.where(valid, e / jnp.sum(e, axis=-1, keepdims=True), 0.0)
    o_ref[0] = _diag_fold(_dot(p.astype(BF16), vc_ref[0]))
    p_grp = p[0:qt]
    for r in range(1, GROUP):
        p_grp = p_grp + p[r * qt:(r + 1) * qt]
    imp = _dot(p_grp.astype(BF16), imp_ref[...])
    sel = _topk_mask(_block_scores(imp, qpos[0:qt], n_sel, 1), min(SEL_TOPK, n_sel), 1)
    bias = jnp.where(sel > 0.0, 0.0, NEG).astype(BF16)
    for r in range(GROUP):
        bias_ref[0, r * qt:(r + 1) * qt, :] = bias


def _dec_cmp_select(qbd, kc, vc, t, past):
    b, rows, lanes = qbd.shape
    nseg = kc.shape[1]
    n_sel = -(-(past + t) // SEL_BLOCK)
    n_sel_pad = -(-n_sel // LANE) * LANE
    imp_m = _importance_matrix(nseg, nseg - 1, n_sel_pad, n_sel)
    return pl.pallas_call(
        functools.partial(_dec_cmp_select_kernel, t=t, nc=nseg - 1, n_sel=n_sel, past=past),
        grid=(b,),
        in_specs=[pl.BlockSpec((1, rows, lanes), lambda i: (i, 0, 0)),
                  pl.BlockSpec((1, nseg, lanes), lambda i: (i, 0, 0)),
                  pl.BlockSpec((1, nseg, lanes), lambda i: (i, 0, 0)),
                  pl.BlockSpec((nseg, n_sel_pad), lambda i: (0, 0))],
        out_specs=[pl.BlockSpec((1, rows, HEAD_DIM), lambda i: (i, 0, 0)),
                   pl.BlockSpec((1, rows, n_sel_pad), lambda i: (i, 0, 0))],
        out_shape=[jax.ShapeDtypeStruct((b, rows, HEAD_DIM), F32),
                   jax.ShapeDtypeStruct((b, rows, n_sel_pad), BF16)],
        compiler_params=_params("arbitrary"),
        name="dec_cmp_select",
    )(qbd, kc, vc, imp_m)


def _dec_attend_kernel(pt_ref, *refs, t, n_steps, past):
    del pt_ref
    pages = refs[:PAGES_PER_STEP]
    (q_ref, bias_ref, e_ref, tail_ref, wbuf_ref, wtail_ref,
     osel_ref, owin_ref, m_ref, l_ref, acc_ref) = refs[PAGES_PER_STEP:]
    step = pl.program_id(1)
    half = KV_LANES // 2
    q = q_ref[0]
    rows = q.shape[0]
    tq = lax.broadcasted_iota(jnp.int32, (rows, 1), 0) % t

    @pl.when(step == 0)
    def _():
        m_ref[...] = jnp.full(m_ref.shape, -jnp.inf, F32)
        l_ref[...] = jnp.zeros(l_ref.shape, F32)
        acc_ref[...] = jnp.zeros(acc_ref.shape, F32)

    def update(s, weighted_values):
        m = m_ref[...]
        m_new = jnp.maximum(m, jnp.max(s, axis=-1, keepdims=True))
        a = jnp.exp(m - m_new)
        p = jnp.exp(s - m_new)
        l_ref[...] = a * l_ref[...] + jnp.sum(p, axis=-1, keepdims=True)
        acc_ref[...] = a * acc_ref[...] + weighted_values(p.astype(BF16))
        m_ref[...] = m_new

    bias = bias_ref[0]
    k_t = jnp.concatenate([page[0, 0, 0].reshape(half, PAGE_SIZE) for page in pages], axis=1).astype(BF16)
    v_t = jnp.concatenate([page[0, 0, 1].reshape(half, PAGE_SIZE) for page in pages], axis=1).astype(BF16)
    update(_dot(q, k_t) + _dot_nt(bias, e_ref[...]), lambda p: _dot_nt(p, v_t))

    @pl.when(step == n_steps - 1)
    def _():
        tail = tail_ref[0]
        tpos = lax.broadcasted_iota(jnp.int32, (1, tail.shape[0]), 1)
        blk = past // SEL_BLOCK
        lane = lax.broadcasted_iota(jnp.int32, bias.shape, 1)
        b_tail = jnp.sum(jnp.where(lane == blk, bias.astype(F32), 0.0), axis=-1, keepdims=True)
        s = _dot_nt(q, tail[:, :half].astype(BF16)) + b_tail
        update(jnp.where(tpos <= tq, s, NEG), lambda p: _dot(p, tail[:, half:].astype(BF16)))
        osel_ref[0] = _diag_fold(acc_ref[...] / l_ref[...])

        wb = wbuf_ref.shape[-1]
        wk_t = wbuf_ref[0, 0, 0].reshape(half, wb).astype(BF16)
        wv_t = wbuf_ref[0, 0, 1].reshape(half, wb).astype(BF16)
        wtail = wtail_ref[0]
        ipos = lax.broadcasted_iota(jnp.int32, (1, wb), 1)
        d = (past + tq) - (past - wb + ipos)
        s1 = jnp.where((d >= 0) & (d < WINDOW), _dot(q, wk_t), NEG)
        s2 = jnp.where(tpos <= tq, _dot_nt(q, wtail[:, :half].astype(BF16)), NEG)
        m = jnp.maximum(jnp.max(s1, axis=-1, keepdims=True), jnp.max(s2, axis=-1, keepdims=True))
        p1, p2 = jnp.exp(s1 - m), jnp.exp(s2 - m)
        den = jnp.sum(p1, axis=-1, keepdims=True) + jnp.sum(p2, axis=-1, keepdims=True)
        o = _dot_nt(p1.astype(BF16), wv_t) + _dot(p2.astype(BF16), wtail[:, half:].astype(BF16))
        owin_ref[0] = _diag_fold(o / den)


def _dec_attend(cache_t, win_t, layer, page_table, qbd, bias, onehot, sel_tail, win_tail, t, past):
    b, n_pages = page_table.shape
    n_steps = n_pages // PAGES_PER_STEP
    rows, lanes = qbd.shape[1:]
    n_sel_pad = bias.shape[2]
    keys_per_step = PAGES_PER_STEP * PAGE_SIZE
    per_seq = lambda arr: pl.BlockSpec((1,) + arr.shape[1:], lambda i, s, pt: (i,) + (0,) * (arr.ndim - 1))
    grid_spec = pltpu.PrefetchScalarGridSpec(
        num_scalar_prefetch=1,
        grid=(b, n_steps),
        in_specs=[_page_spec(layer, k) for k in range(PAGES_PER_STEP)]
        + [per_seq(qbd), per_seq(bias),
           pl.BlockSpec((keys_per_step, n_sel_pad), lambda i, s, pt: (s, 0)),
           per_seq(sel_tail),
           pl.BlockSpec((1, 1) + win_t.shape[2:], lambda i, s, pt: (layer, i, 0, 0, 0, 0)),
           per_seq(win_tail)],
        out_specs=[pl.BlockSpec((1, rows, HEAD_DIM), lambda i, s, pt: (i, 0, 0)),
                   pl.BlockSpec((1, rows, HEAD_DIM), lambda i, s, pt: (i, 0, 0))],
        scratch_shapes=[pltpu.VMEM((rows, 1), F32), pltpu.VMEM((rows, 1), F32), pltpu.VMEM((rows, lanes), F32)],
    )
    return pl.pallas_call(
        functools.partial(_dec_attend_kernel, t=t, n_steps=n_steps, past=past),
        grid_spec=grid_spec,
        out_shape=[jax.ShapeDtypeStruct((b, rows, HEAD_DIM), F32)] * 2,
        compiler_params=_params("arbitrary", "arbitrary"),
        name="dec_attend",
    )(page_table, *([cache_t] * PAGES_PER_STEP), qbd, bias, onehot, sel_tail, win_t, win_tail)


def _gated_out_proj_kernel(x_ref, oc_ref, os_ref, ow_ref, gate_ref, ex_ref, w_ref, mod_ref, y_ref):
    gates = gate_ref[...]
    o = jnp.zeros(oc_ref.shape, F32)
    for k, ref in enumerate((oc_ref, os_ref, ow_ref)):
        o = o + _dot_exact(gates, ex_ref[k]) * ref[...]
    y_ref[...] = x_ref[...] + (1.0 + mod_ref[2]) * _dot(o.astype(BF16), w_ref[...])


def _gate_expansion():
    ex = np.zeros((3, LANE, NSA_Q), np.float32)
    for k in range(3):
        for h in range(N_HEADS):
            ex[k, k * N_HEADS + h, h * HEAD_DIM:(h + 1) * HEAD_DIM] = 1.0
    return jnp.asarray(ex)


def _gated_out_proj(x, oc, osel, owin, gates, w, mod):
    rows = x.shape[0]
    full = lambda n: pl.BlockSpec((rows, n), lambda i: (0, 0))
    return pl.pallas_call(
        _gated_out_proj_kernel,
        grid=(1,),
        in_specs=[full(D_MODEL), full(NSA_Q), full(NSA_Q), full(NSA_Q), full(LANE),
                  pl.BlockSpec((3, LANE, NSA_Q), lambda i: (0, 0, 0)),
                  pl.BlockSpec((NSA_Q, D_MODEL), lambda i: (0, 0)),
                  pl.BlockSpec((3, rows, D_MODEL), lambda i: (0, 0, 0))],
        out_specs=full(D_MODEL),
        out_shape=jax.ShapeDtypeStruct((rows, D_MODEL), F32),
        compiler_params=_params("arbitrary"),
        name="gated_out_proj",
    )(x, oc, osel, owin, gates, _gate_expansion(), w, mod)


def _ml_proj_kernel(x_ref, g_ref, mod_ref, w_ref, bg_ref, xm_ref, z_ref, gl_ref):
    h = _modulated_norm(x_ref[...], g_ref[...], mod_ref[0], mod_ref[1]).astype(BF16)
    proj = _dot(h, w_ref[...])
    xm_ref[...] = proj[:, :ML_INNER]
    z_ref[...] = proj[:, ML_INNER:2 * ML_INNER]
    gl = proj[:, 2 * ML_INNER:] + bg_ref[...]
    lane = lax.broadcasted_iota(jnp.int32, gl.shape, 1)
    gl_ref[...] = jnp.where(lane < ML_HEADS, gl, jax.nn.log_sigmoid(gl))


def _ml_proj(x, g, mod, w, b_gate, tm, tiles_per_seq=1):
    rows = x.shape[0]
    row = lambda n: pl.BlockSpec((tm, n), lambda i: (i, 0))
    return pl.pallas_call(
        _ml_proj_kernel,
        grid=(rows // tm,),
        in_specs=[row(D_MODEL), _resident((1, D_MODEL)), _mod_spec(mod, tm, tiles_per_seq),
                  _resident((D_MODEL, ML_N)), _resident((1, LANE))],
        out_specs=[row(ML_INNER), row(ML_INNER), row(LANE)],
        out_shape=[jax.ShapeDtypeStruct((rows, n), F32) for n in (ML_INNER, ML_INNER, LANE)],
        compiler_params=_params("arbitrary"),
        name="ml_proj",
    )(x, g, mod, w, b_gate)


def _conv_qkv(taps, cw_ref, cb_ref, wqk_ref, wv_ref, xc_ref, q_ref, k_ref, v_ref):
    xc = cb_ref[...]
    for w in range(CONV_W):
        xc = xc + taps[w] * cw_ref[w:w + 1, :]
    xc = jax.nn.silu(xc)
    xc_ref[...] = xc
    xcb = xc.astype(BF16)
    xmb = taps[CONV_W - 1].astype(BF16)
    for j in range(ML_INNER // LANE):
        sl = slice(j * LANE, (j + 1) * LANE)
        qk = _dot(xcb[:, sl], wqk_ref[j])
        q_ref[:, sl] = qk[:, :LANE].astype(q_ref.dtype)
        k_ref[:, sl] = (qk[:, LANE:] * (ML_HEAD_DIM ** -0.5)).astype(k_ref.dtype)
        v_ref[:, sl] = _dot(xmb[:, sl], wv_ref[j]).astype(v_ref.dtype)


def _ml_conv_prompt_kernel(xm_ref, halo_ref, cw_ref, cb_ref, wqk_ref, wv_ref,
                           xc_ref, q_ref, k_ref, v_ref, pad_ref, *, tm, tiles_per_seq):
    first = pl.program_id(0) % tiles_per_seq == 0
    pad_ref[0:8, :] = jnp.where(first, 0.0, halo_ref[...])
    pad_ref[8:, :] = xm_ref[...]
    taps = [pad_ref[pl.ds(8 - (CONV_W - 1) + w, tm), :] for w in range(CONV_W)]
    _conv_qkv(taps, cw_ref, cb_ref, wqk_ref, wv_ref, xc_ref, q_ref, k_ref, v_ref)


def _ml_conv_sample_kernel(t0_ref, t1_ref, t2_ref, t3_ref, cw_ref, cb_ref, wqk_ref, wv_ref,
                           xc_ref, q_ref, k_ref, v_ref):
    taps = [r[...] for r in (t0_ref, t1_ref, t2_ref, t3_ref)]
    _conv_qkv(taps, cw_ref, cb_ref, wqk_ref, wv_ref, xc_ref, q_ref, k_ref, v_ref)


def _headwise_weights(wq, wk, wv):
    def blockdiag(w):
        n_chunks = ML_INNER // LANE
        per = LANE // 4
        wc = w.reshape(n_chunks, per, 4, 4)
        eye = jnp.eye(per, dtype=w.dtype)
        return jnp.einsum('jncd,nm->jncmd', wc, eye).reshape(n_chunks, LANE, LANE)
    return jnp.concatenate([blockdiag(wq), blockdiag(wk)], axis=-1).astype(BF16), blockdiag(wv).astype(BF16)


def _ml_conv_prompt(xm, cw, cb, wqk, wv, tm, tiles_per_seq):
    rows = xm.shape[0]
    row = pl.BlockSpec((tm, ML_INNER), lambda i: (i, 0))
    out = jax.ShapeDtypeStruct((rows, ML_INNER), BF16)
    return pl.pallas_call(
        functools.partial(_ml_conv_prompt_kernel, tm=tm, tiles_per_seq=tiles_per_seq),
        grid=(rows // tm,),
        in_specs=[row, pl.BlockSpec((8, ML_INNER), lambda i: (jnp.maximum(i * (tm // 8) - 1, 0), 0)),
                  _resident(cw.shape), _resident(cb.shape), _resident(wqk.shape), _resident(wv.shape)],
        out_specs=[row] * 4,
        out_shape=[jax.ShapeDtypeStruct((rows, ML_INNER), F32), out, out, out],
        scratch_shapes=[pltpu.VMEM((tm + 8, ML_INNER), F32)],
        compiler_params=_params("arbitrary"),
        name="ml_conv_prompt",
    )(xm, xm, cw, cb, wqk, wv)


def _ml_conv_sample(taps, cw, cb, wqk, wv):
    rows = taps[0].shape[0]
    full = pl.BlockSpec((rows, ML_INNER), lambda i: (0, 0))
    const = lambda a: pl.BlockSpec(a.shape, lambda i: (0,) * a.ndim)
    out = jax.ShapeDtypeStruct((rows, ML_INNER), F32)
    return pl.pallas_call(
        _ml_conv_sample_kernel,
        grid=(1,),
        in_specs=[full] * 4 + [const(cw), const(cb), const(wqk), const(wv)],
        out_specs=[full] * 4,
        out_shape=[out] * 4,
        compiler_params=_params("arbitrary"),
        name="ml_conv_sample",
    )(*taps, cw, cb, wqk, wv)


def _ml_cell_kernel(q_ref, k_ref, v_ref, gl_ref, c0_ref, n0_ref, m0_ref,
                    h_ref, c_ref, n_ref, m_ref, *, chunk, n_chunks):
    step = pl.program_id(1)

    @pl.when(step == 0)
    def _():
        c_ref[...] = c0_ref[0]
        n_ref[...] = n0_ref[0]
        m_ref[...] = m0_ref[...]

    gl = gl_ref[0]
    ti = lax.broadcasted_iota(jnp.int32, (chunk, chunk), 0)
    si = lax.broadcasted_iota(jnp.int32, (chunk, chunk), 1)
    tri = si <= ti
    cum = _dot_exact(tri.astype(F32), gl)
    sel = (lax.broadcasted_iota(jnp.int32, (8, LANE), 0) == lax.broadcasted_iota(jnp.int32, (8, LANE), 1)).astype(F32)
    gl_rows = _dot_nt_exact(sel, gl)
    cum_rows = _dot_nt_exact(sel, cum)
    for h in range(ML_HEADS):
        hs = slice(h * ML_HEAD_DIM, (h + 1) * ML_HEAD_DIM)
        q = q_ref[0, :, hs].astype(BF16)
        k = k_ref[0, :, hs].astype(F32)
        v = v_ref[0, :, hs].astype(BF16)
        ig_col = gl[:, h:h + 1]
        ig_row = gl_rows[h:h + 1, :]
        b_col = cum[:, ML_HEADS + h:ML_HEADS + h + 1]
        b_row = cum_rows[ML_HEADS + h:ML_HEADS + h + 1, :]
        b_last = b_col[chunk - 1:chunk, :]
        m_prev = m_ref[0, 0:1, h:h + 1]
        c_prev = c_ref[0, h]
        n_prev = n_ref[0, h:h + 1, :]

        dmat = jnp.where(tri, b_col - b_row + ig_row, -jnp.inf)
        inter = b_col + m_prev
        m_t = jnp.maximum(inter, jnp.max(dmat, axis=-1, keepdims=True))
        w_int = jnp.exp(inter - m_t)
        s = _dot_nt(q, k.astype(BF16)) * jnp.exp(dmat - m_t)
        num = w_int * _dot(q, c_prev.astype(BF16)) + _dot(s.astype(BF16), v)
        den = w_int * jnp.sum(q.astype(F32) * n_prev, axis=-1, keepdims=True) + jnp.sum(s, axis=-1, keepdims=True)
        h_ref[0, :, hs] = num / jnp.maximum(jnp.abs(den), jnp.exp(-m_t))

        m_new = m_t[chunk - 1:chunk, :]
        w_dec = jnp.exp(b_last + m_prev - m_new)
        kw = k * jnp.exp(b_last - b_col + ig_col - m_new)
        c_ref[0, h] = w_dec * c_prev + _dot_tn(kw.astype(BF16), v)
        n_ref[0, h:h + 1, :] = w_dec * n_prev + jnp.sum(kw, axis=0, keepdims=True)
        m_ref[0, :, h:h + 1] = jnp.broadcast_to(m_new, (8, 1))


def _ml_cell(q, k, v, gl, c0, n0, layer, m0, chunk):
    b, t, _ = q.shape
    n_chunks = t // chunk
    seq = lambda n: pl.BlockSpec((1, chunk, n), lambda i, c: (i, c, 0))
    c_spec = pl.BlockSpec((1, ML_HEADS, ML_HEAD_DIM, ML_HEAD_DIM), lambda i, c: (i, 0, 0, 0))
    n_spec = pl.BlockSpec((1, ML_HEADS, ML_HEAD_DIM), lambda i, c: (i, 0, 0))
    m_spec = pl.BlockSpec((1, 8, LANE), lambda i, c: (i, 0, 0))
    c0_spec = pl.BlockSpec((1, 1, ML_HEADS, ML_HEAD_DIM, ML_HEAD_DIM), lambda i, c: (layer, i, 0, 0, 0))
    n0_spec = pl.BlockSpec((1, 1, ML_HEADS, ML_HEAD_DIM), lambda i, c: (layer, i, 0, 0))
    return pl.pallas_call(
        functools.partial(_ml_cell_kernel, chunk=chunk, n_chunks=n_chunks),
        grid=(b, n_chunks),
        in_specs=[seq(ML_INNER), seq(ML_INNER), seq(ML_INNER), seq(LANE), c0_spec, n0_spec, m_spec],
        out_specs=[seq(ML_INNER), c_spec, n_spec, m_spec],
        out_shape=[jax.ShapeDtypeStruct((b, t, ML_INNER), F32),
                   jax.ShapeDtypeStruct(c0.shape[1:], F32),
                   jax.ShapeDtypeStruct(n0.shape[1:], F32),
                   jax.ShapeDtypeStruct(m0.shape, F32)],
        compiler_params=_params("arbitrary", "arbitrary"),
        name="ml_cell",
    )(q, k, v, gl, c0, n0, m0)


def _ml_out_kernel(x_ref, h_ref, z_ref, xc_ref, gn_ref, skip_ref, w_ref, mod_ref, y_ref):
    hg = h_ref[...] * jax.nn.sigmoid(z_ref[...])
    parts = []
    for h in range(ML_HEADS):
        hh = hg[:, h * ML_HEAD_DIM:(h + 1) * ML_HEAD_DIM]
        parts.append(hh * lax.rsqrt(jnp.mean(hh * hh, axis=-1, keepdims=True) + NORM_EPS))
    hn = jnp.concatenate(parts, axis=-1) * gn_ref[...] + skip_ref[...] * xc_ref[...]
    y_ref[...] = x_ref[...] + (1.0 + mod_ref[2]) * _dot(hn.astype(BF16), w_ref[...])


def _ml_out(x, hcell, z, xc, gn, skip, w, mod, tm, tiles_per_seq=1):
    rows = x.shape[0]
    row = lambda n: pl.BlockSpec((tm, n), lambda i: (i, 0))
    return pl.pallas_call(
        _ml_out_kernel,
        grid=(rows // tm,),
        in_specs=[row(D_MODEL), row(ML_INNER), row(ML_INNER), row(ML_INNER),
                  _resident((1, ML_INNER)), _resident((1, ML_INNER)), _resident((ML_INNER, D_MODEL)),
                  _mod_spec(mod, tm, tiles_per_seq)],
        out_specs=row(D_MODEL),
        out_shape=jax.ShapeDtypeStruct((rows, D_MODEL), F32),
        compiler_params=_params("arbitrary"),
        name="ml_out",
    )(x, hcell, z, xc, gn, skip, w, mod)


TM_PROMPT = 512
ML_CHUNK_PROMPT = 256
ML_CHUNK_SAMPLE = 8


def _pad_cols(w, n):
    return jnp.pad(w, ((0, 0), (0, n - w.shape[1])))


def _nsa_prompt_layer(x, g, mod, w_in, w_out, cmp_w, tables, b, t):
    tiles = t // TM_PROMPT
    q, cmp_kv, gates, cmp_t, sel_t, win_t, ks_aug, kw_aug, vs_t, vw_t = _nsa_proj(
        x, g, mod, w_in, tables, TM_PROMPT, tiles, prompt_layouts=True)
    pe, w1c, _, w2b = cmp_w
    cmp = _compress_prompt(cmp_kv.reshape(b, t // CMP_STRIDE, CMP_STRIDE * KV_LANES), pe, w1c, w2b)
    nseg = cmp.shape[2]
    cmp5 = cmp.reshape(b, 2, nseg, KV_HEADS, HEAD_DIM)
    o_cmp, q_aug = _cmp_select(q.reshape(b, t, NSA_Q), cmp5[:, 0].transpose(0, 2, 1, 3),
                               cmp5[:, 1].transpose(0, 2, 3, 1), gates, tq=256)
    per_head = lambda v_t: v_t.reshape(b, KV_HEADS, V_ROWS, t)
    o = _flash(q_aug, ks_aug, per_head(vs_t), kw_aug, per_head(vw_t), o_cmp, gates, tq=512, tk=256)
    x = _out_proj(x, o.reshape(b * t, NSA_Q), w_out, mod, TM_PROMPT, tiles)
    kv6 = lambda a_t: jnp.transpose(a_t.reshape(b, 2, KV_HEADS, HEAD_DIM, a_t.shape[-1]), (0, 4, 1, 2, 3))
    return x, kv6(cmp_t), kv6(sel_t), kv6(win_t[:, :, t - min(WINDOW, t):])


def _paged_view(cache):
    return jnp.transpose(cache, (0, 1, 3, 4, 5, 2))


def _nsa_sample_layer(x, g, mod, w_in, w_out, cmp_w, tables, cmp_t, sel_t, win_t, layer, win_buf, page_table, b, t):
    rows = b * t
    past = page_table.shape[1] * PAGE_SIZE
    q, cmp_kv, sel_kv, win_kv, gates = _nsa_proj(x, g, mod, w_in, tables, rows)
    cmp = _compress_paged(cmp_t, layer, page_table, *cmp_w)
    q5 = (q * SCALE).reshape(b, t, KV_HEADS, GROUP, HEAD_DIM).transpose(0, 3, 2, 1, 4)
    qbd = (q5[:, :, :, :, None, :] * jnp.eye(KV_HEADS, dtype=F32)[None, None, :, None, :, None])
    qbd = qbd.reshape(b, GROUP * KV_HEADS * t, KV_HEADS * HEAD_DIM).astype(BF16)
    o_cmp, bias = _dec_cmp_select(qbd, cmp[:, 0], cmp[:, 1], t, past)
    n_sel_pad = bias.shape[2]
    onehot = (jnp.arange(past)[:, None] // SEL_BLOCK == jnp.arange(n_sel_pad)[None, :]).astype(BF16)
    pad8 = lambda a: jnp.pad(a.reshape(b, t, KV_LANES), ((0, 0), (0, 8 - t), (0, 0)))
    wb = win_buf.shape[1]
    o_sel, o_win = _dec_attend(sel_t, win_t, layer, page_table, qbd, bias, onehot, pad8(sel_kv), pad8(win_kv), t, past)
    lane_dense = lambda o: o.reshape(b, GROUP, KV_HEADS, t, HEAD_DIM).transpose(0, 3, 2, 1, 4).reshape(rows, NSA_Q)
    x = _gated_out_proj(x, lane_dense(o_cmp), lane_dense(o_sel), lane_dense(o_win), gates, w_out, mod)
    kv6 = lambda a: a.reshape(b, t, 2, KV_HEADS, HEAD_DIM)
    wkv = jnp.concatenate([win_buf, kv6(win_kv)], axis=1)
    return x, kv6(cmp_kv), kv6(sel_kv), wkv[:, wb + t - min(WINDOW, wb + t):]


def _ml_weights(w_in, b_gate, conv_w, conv_b, wq, wk, wv, skip, gn_g, w_out):
    wqk, wvb = _headwise_weights(wq, wk, wv)
    bg = jnp.pad(b_gate.reshape(1, 2 * ML_HEADS), ((0, 0), (0, LANE - 2 * ML_HEADS)))
    return dict(w_in=_pad_cols(w_in, ML_N).astype(BF16), bg=bg, cw=conv_w, cb=conv_b.reshape(1, ML_INNER),
                wqk=wqk, wv=wvb, skip=skip.reshape(1, ML_INNER), gn=gn_g.reshape(1, ML_INNER),
                w_out=w_out.astype(BF16))


def _ml_prompt_layer(x, g, mod, w, b, t):
    tiles = t // TM_PROMPT
    xm, z, gl = _ml_proj(x, g, mod, w['w_in'], w['bg'], TM_PROMPT, tiles)
    xc, q, k, v = _ml_conv_prompt(xm, w['cw'], w['cb'], w['wqk'], w['wv'], TM_PROMPT, tiles)
    seq = lambda a: a.reshape(b, t, a.shape[-1])
    c0 = jnp.zeros((1, b, ML_HEADS, ML_HEAD_DIM, ML_HEAD_DIM), F32)
    n0 = jnp.zeros((1, b, ML_HEADS, ML_HEAD_DIM), F32)
    m0 = jnp.zeros((b, 8, LANE), F32)
    hcell, c, n, m = _ml_cell(seq(q), seq(k), seq(v), seq(gl), c0, n0, 0, m0, ML_CHUNK_PROMPT)
    x = _ml_out(x, hcell.reshape(b * t, ML_INNER), z, xc, w['gn'], w['skip'], w['w_out'], mod, TM_PROMPT, tiles)
    return x, c, n, m[:, 0, :ML_HEADS], seq(xm)[:, t - (CONV_W - 1):]


def _ml_sample_layer(x, g, mod, w, conv_buf, c0, n0, layer, m0, b, t):
    rows = b * t
    xm, z, gl = _ml_proj(x, g, mod, w['w_in'], w['bg'], rows)
    xpad = jnp.concatenate([conv_buf, xm.reshape(b, t, ML_INNER)], axis=1)
    taps = [xpad[:, k:k + t].reshape(rows, ML_INNER) for k in range(CONV_W)]
    xc, q, k, v = _ml_conv_sample(taps, w['cw'], w['cb'], w['wqk'], w['wv'])
    tp = ML_CHUNK_SAMPLE
    pad = lambda a: jnp.pad(a.reshape(b, t, a.shape[-1]), ((0, 0), (0, tp - t), (0, 0)))
    lane = jnp.arange(LANE)[None, None, :]
    step = jnp.arange(tp)[None, :, None]
    glp = jnp.where(step < t, pad(gl), jnp.where(lane < ML_HEADS, NEG, 0.0))
    m0p = jnp.broadcast_to(jnp.pad(m0, ((0, 0), (0, LANE - ML_HEADS)))[:, None, :], (b, 8, LANE))
    hcell, c, n, m = _ml_cell(pad(q), pad(k), pad(v), glp, c0, n0, layer, m0p, tp)
    x = _ml_out(x, hcell[:, :t].reshape(rows, ML_INNER), z, xc, w['gn'], w['skip'], w['w_out'], mod, rows)
    return x, c, n, m[:, 0, :ML_HEADS], xpad[:, t:]


def kernel(x_prompt, x_sample, cache_cmp, cache_sel, cache_win, state_C, state_n, state_m, state_conv, page_table,
           c_prompt, c_sample, ada_w, ada_b, norm_g, ffn_w_in, ffn_w_out, final_norm_g,
           attn_w_in, attn_w_out, cmp_pos, cmp_w1, cmp_w2,
           ml_w_in, ml_b_gate, ml_conv_w, ml_conv_b, ml_wq, ml_wk, ml_wv, ml_skip, ml_gn_g, ml_w_out):
    bp, tp, _ = x_prompt.shape
    bs, ts, _ = x_sample.shape
    depth = ada_w.shape[0]
    rows_s = bs * ts
    past = page_table.shape[1] * PAGE_SIZE

    n_c = bp + bs
    c_all = jnp.pad(jnp.concatenate([c_prompt, c_sample], axis=0), ((0, -n_c % 8), (0, 0)))
    mods = _adaln(c_all, ada_w, ada_b)

    xp = x_prompt.reshape(bp * tp, D_MODEL)
    xs = x_sample.reshape(rows_s, D_MODEL)
    tables_p = _rope_tables(jnp.arange(tp))
    tables_s = tuple(jnp.tile(tb, (bs, 1)) for tb in _rope_tables(past + jnp.arange(ts)))
    tiles = tp // TM_PROMPT
    cmp_t, sel_t, win_t = _paged_view(cache_cmp), _paged_view(cache_sel), _paged_view(cache_win)

    outs_p = {k: [] for k in ('cmp', 'sel', 'win', 'c', 'n', 'm', 'conv')}
    outs_s = {k: [] for k in ('cmp', 'sel', 'win', 'c', 'n', 'm', 'conv')}
    for i in range(depth):
        mod_i = mods[i, :n_c].reshape(n_c, 9, D_MODEL)
        mp = [mod_i[:bp, 3 * s:3 * s + 3].reshape(bp, 3, 1, D_MODEL) for s in range(3)]
        ms = [jnp.repeat(mod_i[bp:, 3 * s:3 * s + 3].transpose(1, 0, 2), ts, axis=1) for s in range(3)]
        g = norm_g[i].reshape(3, 1, D_MODEL)
        w_in = ffn_w_in[i].astype(BF16)
        w_out = ffn_w_out[i].astype(BF16)

        xp = _ffn(xp, g[0], mp[0], w_in[0], w_out[0], TM_PROMPT, tiles)
        xs = _ffn(xs, g[0], ms[0], w_in[0], w_out[0], rows_s)
        j = i // 2
        if i % 2 == 0:
            a_in = _pad_cols(attn_w_in[j], NSA_N).astype(BF16)
            a_out = attn_w_out[j].astype(BF16)
            cmp_w = _compress_weights(cmp_pos[j], cmp_w1[j], cmp_w2[j])
            xp, ckv, skv, wkv = _nsa_prompt_layer(xp, g[1], mp[1], a_in, a_out, cmp_w, tables_p, bp, tp)
            xs, ckv_s, skv_s, wkv_s = _nsa_sample_layer(xs, g[1], ms[1], a_in, a_out, cmp_w, tables_s,
                                                        cmp_t, sel_t, win_t, j, cache_win[j], page_table, bs, ts)
            for d, vals in ((outs_p, (ckv, skv, wkv)), (outs_s, (ckv_s, skv_s, wkv_s))):
                for key, val in zip(('cmp', 'sel', 'win'), vals):
                    d[key].append(val)
        else:
            w = _ml_weights(ml_w_in[j], ml_b_gate[j], ml_conv_w[j], ml_conv_b[j], ml_wq[j], ml_wk[j], ml_wv[j],
                            ml_skip[j], ml_gn_g[j], ml_w_out[j])
            xp, c_p, n_p, m_p, buf_p = _ml_prompt_layer(xp, g[1], mp[1], w, bp, tp)
            xs, c_s, n_s, m_s, buf_s = _ml_sample_layer(xs, g[1], ms[1], w, state_conv[j], state_C, state_n, j,
                                                        state_m[j], bs, ts)
            for d, vals in ((outs_p, (c_p, n_p, m_p, buf_p)), (outs_s, (c_s, n_s, m_s, buf_s))):
                for key, val in zip(('c', 'n', 'm', 'conv'), vals):
                    d[key].append(val)
        xp = _ffn(xp, g[2], mp[2], w_in[1], w_out[1], TM_PROMPT, tiles)
        xs = _ffn(xs, g[2], ms[2], w_in[1], w_out[1], rows_s)

    fg = final_norm_g.reshape(1, D_MODEL)
    y_prompt = _final_norm(xp, fg, TM_PROMPT).reshape(bp, tp, D_MODEL)
    y_sample = _final_norm(xs, fg, rows_s).reshape(bs, ts, D_MODEL)
    st = lambda d, key: jnp.stack(d[key])
    return (y_prompt, y_sample,
            st(outs_p, 'cmp'), st(outs_p, 'sel'), st(outs_p, 'win'),
            st(outs_p, 'c'), st(outs_p, 'n'), st(outs_p, 'm'), st(outs_p, 'conv'),
            st(outs_s, 'cmp'), st(outs_s, 'sel'), st(outs_s, 'win'),
            st(outs_s, 'c'), st(outs_s, 'n'), st(outs_s, 'm'), st(outs_s, 'conv'))
```

```python
import functools
import math

import numpy as np
import jax
import jax.numpy as jnp
from jax import lax
from jax.experimental import pallas as pl
from jax.experimental.pallas import tpu as pltpu

D_MODEL = 1024
D_FF = 2816
N_HEADS = 16
HEAD_DIM = 64
KV_HEADS = 4
GROUP = 4
ROT_DIM = 16
ROPE_THETA = 500000.0
CMP_BLOCK = 32
CMP_STRIDE = 16
CMP_HIDDEN = 128
SEL_BLOCK = 64
SEL_TOPK = 16
WINDOW = 512
PAGE_SIZE = 128
KV_LANES = 2 * KV_HEADS * HEAD_DIM
NSA_Q = N_HEADS * HEAD_DIM
NSA_N = 2688
ML_INNER = 2048
ML_HEADS = 4
ML_HEAD_DIM = 512
ML_N = 4224
CONV_W = 4
NORM_EPS = 1e-6
NEG = -1e30
BIG = 1e30
SCALE = HEAD_DIM ** -0.5
LOG2E = 1.4426950408889634
V_ROWS = HEAD_DIM + 16

F32 = jnp.float32
BF16 = jnp.bfloat16
LANE = 128
VMEM_LIMIT_BYTES = 56 * 1024 * 1024


def _params(*sem):
    return pltpu.CompilerParams(dimension_semantics=sem, vmem_limit_bytes=VMEM_LIMIT_BYTES)


def _resident(shape):
    nd = len(shape)
    return pl.BlockSpec(shape, lambda *_: (0,) * nd, pipeline_mode=pl.Buffered(1))


def _dot(a, b):
    return jnp.dot(a, b, preferred_element_type=F32)


def _dot_nt(a, b):
    return lax.dot_general(a, b, (((1,), (1,)), ((), ())), preferred_element_type=F32)


def _dot_tn(a, b):
    return lax.dot_general(a, b, (((0,), (0,)), ((), ())), preferred_element_type=F32)


def _dot_exact(a, b):
    return jnp.dot(a, b, preferred_element_type=F32, precision=lax.Precision.HIGHEST)


def _dot_nt_exact(a, b):
    return lax.dot_general(a, b, (((1,), (1,)), ((), ())), preferred_element_type=F32,
                           precision=lax.Precision.HIGHEST)


def _modulated_norm(x, g, shift, scale):
    y = x * lax.rsqrt(jnp.mean(x * x, axis=-1, keepdims=True) + NORM_EPS)
    return (y * g) * (1.0 + scale) + shift


def _mod_spec(mod, tm, tiles_per_seq):
    if mod.ndim == 4:
        return pl.BlockSpec((None, 3, 1, D_MODEL), lambda i: (i // tiles_per_seq, 0, 0, 0))
    return pl.BlockSpec((3, tm, D_MODEL), lambda i: (0, i, 0))


def _adaln_kernel(c_ref, w_ref, b_ref, o_ref):
    a = jax.nn.silu(c_ref[...]).astype(BF16)
    o_ref[0] = _dot(a, w_ref[0].astype(BF16)) + b_ref[0]


def _adaln(c_all, ada_w, ada_b):
    depth, _, n = ada_w.shape
    rows = c_all.shape[0]
    tn = 2304
    return pl.pallas_call(
        _adaln_kernel,
        grid=(depth, n // tn),
        in_specs=[pl.BlockSpec((rows, D_MODEL), lambda l, j: (0, 0)),
                  pl.BlockSpec((1, D_MODEL, tn), lambda l, j: (l, 0, j)),
                  pl.BlockSpec((1, 1, tn), lambda l, j: (l, 0, j))],
        out_specs=pl.BlockSpec((1, rows, tn), lambda l, j: (l, 0, j)),
        out_shape=jax.ShapeDtypeStruct((depth, rows, n), F32),
        compiler_params=_params("arbitrary", "arbitrary"),
        name="adaln",
    )(c_all, ada_w, ada_b.reshape(depth, 1, n))


MXU_DIM = 256
FF_CHUNKS = (6 * MXU_DIM, 5 * MXU_DIM)
assert sum(FF_CHUNKS) == D_FF


def _ffn_kernel(x_ref, g_ref, mod_ref, win_ref, wout_ref, o_ref):
    x = x_ref[...]
    h = _modulated_norm(x, g_ref[...], mod_ref[0], mod_ref[1]).astype(BF16)
    acc = jnp.zeros(x.shape, F32)
    lo = 0
    for width in FF_CHUNKS:
        gate = _dot(h, win_ref[:, lo:lo + width])
        up = _dot(h, win_ref[:, D_FF + lo:D_FF + lo + width])
        act = (jax.nn.silu(gate) * up).astype(BF16)
        acc = acc + _dot(act, wout_ref[lo:lo + width, :])
        lo += width
    o_ref[...] = x + 0.5 * (1.0 + mod_ref[2]) * acc


def _ffn(x, g, mod, w_in, w_out, tm, tiles_per_seq=1):
    rows = x.shape[0]
    return pl.pallas_call(
        _ffn_kernel,
        grid=(rows // tm,),
        in_specs=[pl.BlockSpec((tm, D_MODEL), lambda i: (i, 0)),
                  _resident((1, D_MODEL)),
                  _mod_spec(mod, tm, tiles_per_seq),
                  _resident((D_MODEL, 2 * D_FF)),
                  _resident((D_FF, D_MODEL))],
        out_specs=pl.BlockSpec((tm, D_MODEL), lambda i: (i, 0)),
        out_shape=jax.ShapeDtypeStruct((rows, D_MODEL), F32),
        compiler_params=_params("arbitrary"),
        name="ffn_half",
    )(x, g, mod, w_in, w_out)


def _final_norm_kernel(x_ref, g_ref, o_ref):
    x = x_ref[...]
    o_ref[...] = x * lax.rsqrt(jnp.mean(x * x, axis=-1, keepdims=True) + NORM_EPS) * g_ref[...]


def _final_norm(x, g, tm):
    rows = x.shape[0]
    return pl.pallas_call(
        _final_norm_kernel,
        grid=(rows // tm,),
        in_specs=[pl.BlockSpec((tm, D_MODEL), lambda i: (i, 0)), _resident((1, D_MODEL))],
        out_specs=pl.BlockSpec((tm, D_MODEL), lambda i: (i, 0)),
        out_shape=jax.ShapeDtypeStruct((rows, D_MODEL), F32),
        compiler_params=_params("arbitrary"),
        name="final_norm",
    )(x, g)


def _rope_tables(pos):
    half = ROT_DIM // 2
    inv = ROPE_THETA ** (-jnp.arange(0, ROT_DIM, 2, dtype=F32) / ROT_DIM)
    ang = pos.astype(F32)[:, None] * inv[None, :]
    cos, sin = jnp.cos(ang), jnp.sin(ang)
    n = pos.shape[0]
    pad = jnp.zeros((n, HEAD_DIM - ROT_DIM), F32)
    zeros = jnp.zeros((n, half), F32)
    c = jnp.concatenate([cos, cos, pad + 1.0], axis=1)
    s1 = jnp.concatenate([-sin, zeros, pad], axis=1)
    s2 = jnp.concatenate([zeros, sin, pad], axis=1)
    return tuple(jnp.tile(t, (1, LANE // HEAD_DIM)) for t in (c, s1, s2))


def _nsa_proj_kernel(x_ref, g_ref, mod_ref, w_ref, c_ref, s1_ref, s2_ref, *out_refs, prompt_layouts, tiles_per_seq):
    h = _modulated_norm(x_ref[...], g_ref[...], mod_ref[0], mod_ref[1]).astype(BF16)
    proj = _dot(h, w_ref[...])
    c, s1, s2 = c_ref[...], s1_ref[...], s2_ref[...]
    tm = proj.shape[0]
    half = KV_LANES // 2

    def rope(lo, n):
        chunks = []
        for j in range(n // LANE):
            v = proj[:, lo + j * LANE:lo + (j + 1) * LANE]
            chunks.append(v * c + pltpu.roll(v, LANE - ROT_DIM // 2, 1) * s1 + pltpu.roll(v, ROT_DIM // 2, 1) * s2)
        return jnp.concatenate(chunks, axis=1)

    q = rope(0, NSA_Q)
    branches = []
    for br in range(3):
        base = NSA_Q + br * KV_LANES
        branches.append(jnp.concatenate([rope(base, half), proj[:, base + half:base + KV_LANES]], axis=1))
    gates = jax.nn.sigmoid(proj[:, NSA_Q + 3 * KV_LANES:])

    if not prompt_layouts:
        q_ref, cmp_ref, sel_ref, win_ref, gate_ref = out_refs
        q_ref[...] = q
        gate_ref[...] = gates
        for ref, kv in zip((cmp_ref, sel_ref, win_ref), branches):
            ref[...] = kv
        return

    q_ref, cmp_ref, gate_ref, cmp_t_ref, sel_t_ref, win_t_ref, ks_ref, kw_ref, vst_ref, vwt_ref = out_refs
    q_ref[...] = q
    gate_ref[...] = gates
    cmp_ref[...] = branches[0]
    transposed = [kv.T for kv in branches]
    for ref, kv_t in zip((cmp_t_ref, sel_t_ref, win_t_ref), transposed):
        ref[0] = kv_t
    ones_pad = (lax.broadcasted_iota(jnp.int32, (V_ROWS - HEAD_DIM, tm), 0) == 0).astype(F32)
    for ref, kv_t in ((vst_ref, transposed[1]), (vwt_ref, transposed[2])):
        rows = []
        for g in range(KV_HEADS):
            rows += [kv_t[half + g * HEAD_DIM:half + (g + 1) * HEAD_DIM], ones_pad]
        ref[0] = jnp.concatenate(rows, axis=0).astype(BF16)
    pos = (pl.program_id(0) % tiles_per_seq) * tm + lax.broadcasted_iota(jnp.int32, (tm, HEAD_DIM), 0)
    onehot = (pos // SEL_BLOCK == lax.broadcasted_iota(jnp.int32, (tm, HEAD_DIM), 1)).astype(BF16)
    zeros = jnp.zeros((tm, HEAD_DIM), BF16)
    for g in range(KV_HEADS):
        sl = slice(g * HEAD_DIM, (g + 1) * HEAD_DIM)
        ks_ref[0, g, :, 0:HEAD_DIM] = branches[1][:, sl].astype(BF16)
        ks_ref[0, g, :, HEAD_DIM:] = onehot
        kw_ref[0, g, :, 0:HEAD_DIM] = branches[2][:, sl].astype(BF16)
        kw_ref[0, g, :, HEAD_DIM:] = zeros


def _nsa_proj(x, g, mod, w, tables, tm, tiles_per_seq=1, prompt_layouts=False):
    rows = x.shape[0]
    trows = tables[0].shape[0]
    tspec = pl.BlockSpec((tm, LANE), lambda i: (i % (trows // tm), 0))
    row = lambda n: pl.BlockSpec((tm, n), lambda i: (i, 0))
    f32 = lambda *shape: jax.ShapeDtypeStruct(shape, F32)
    bf16 = lambda *shape: jax.ShapeDtypeStruct(shape, BF16)
    if prompt_layouts:
        b, t = rows // (tiles_per_seq * tm), tiles_per_seq * tm
        seq_t = lambda n: pl.BlockSpec((1, n, tm), lambda i: (i // tiles_per_seq, 0, i % tiles_per_seq))
        heads = pl.BlockSpec((1, KV_HEADS, tm, 2 * HEAD_DIM), lambda i: (i // tiles_per_seq, 0, i % tiles_per_seq, 0))
        out_specs = [row(NSA_Q), row(KV_LANES), row(LANE), seq_t(KV_LANES), seq_t(KV_LANES), seq_t(KV_LANES),
                     heads, heads, seq_t(KV_HEADS * V_ROWS), seq_t(KV_HEADS * V_ROWS)]
        out_shape = [f32(rows, NSA_Q), f32(rows, KV_LANES), f32(rows, LANE),
                     f32(b, KV_LANES, t), f32(b, KV_LANES, t), f32(b, KV_LANES, t),
                     bf16(b, KV_HEADS, t, 2 * HEAD_DIM), bf16(b, KV_HEADS, t, 2 * HEAD_DIM),
                     bf16(b, KV_HEADS * V_ROWS, t), bf16(b, KV_HEADS * V_ROWS, t)]
    else:
        out_specs = [row(NSA_Q), row(KV_LANES), row(KV_LANES), row(KV_LANES), row(LANE)]
        out_shape = [f32(rows, n) for n in (NSA_Q, KV_LANES, KV_LANES, KV_LANES, LANE)]
    return pl.pallas_call(
        functools.partial(_nsa_proj_kernel, prompt_layouts=prompt_layouts, tiles_per_seq=tiles_per_seq),
        grid=(rows // tm,),
        in_specs=[row(D_MODEL), _resident((1, D_MODEL)), _mod_spec(mod, tm, tiles_per_seq),
                  _resident((D_MODEL, NSA_N)), tspec, tspec, tspec],
        out_specs=out_specs,
        out_shape=out_shape,
        compiler_params=_params("arbitrary"),
        name="nsa_proj",
    )(x, g, mod, w, *tables)


def _compress_finish(seg_proj, pe_ref, w1_ref, w2_ref, out_ref, nseg):
    row = lax.broadcasted_iota(jnp.int32, (nseg, 1), 0)
    for s in range(2):
        bias = jnp.zeros((8, 2 * CMP_HIDDEN), F32)
        lane = lax.broadcasted_iota(jnp.int32, (8, 2 * CMP_HIDDEN), 1)
        for l in range(CMP_STRIDE):
            pa = jnp.broadcast_to(pe_ref[s, l:l + 1, :], (8, HEAD_DIM)).astype(BF16)
            pb = jnp.broadcast_to(pe_ref[s, CMP_STRIDE + l:CMP_STRIDE + l + 1, :], (8, HEAD_DIM)).astype(BF16)
            bias = bias + jnp.where(lane < CMP_HIDDEN, _dot(pa, w1_ref[s, l]), _dot(pb, w1_ref[s, l]))
        bias = bias[0:1]
        for g in range(KV_HEADS):
            a = seg_proj(s, g) + bias
            first, second = a[:, :CMP_HIDDEN], a[:, CMP_HIDDEN:]
            hid = jax.nn.gelu(first + pltpu.roll(second, nseg - 1, 0))
            o = _dot(hid.astype(BF16), w2_ref[s])
            o = jnp.where(row < nseg - 1, o, 0.0)
            out_ref[s, :, g * HEAD_DIM:(g + 1) * HEAD_DIM] = o.astype(out_ref.dtype)


def _compress_prompt_kernel(x_ref, pe_ref, w1_ref, w2_ref, out_ref, acc_ref, *, nseg):
    for l in range(CMP_STRIDE):
        xl = x_ref[0, :, l * KV_LANES:(l + 1) * KV_LANES].astype(BF16)
        for sg in range(2 * KV_HEADS):
            d = _dot(xl[:, sg * HEAD_DIM:(sg + 1) * HEAD_DIM], w1_ref[sg // KV_HEADS, l])
            if l == 0:
                acc_ref[sg] = d
            else:
                acc_ref[sg] += d
    _compress_finish(lambda s, g: acc_ref[s * KV_HEADS + g], pe_ref, w1_ref, w2_ref, out_ref.at[0], nseg)


def _compress_weights(pos_emb, w1, w2):
    pe = jnp.transpose(pos_emb, (1, 0, 2))
    w1c = jnp.concatenate([w1[:, :CMP_STRIDE], w1[:, CMP_STRIDE:]], axis=-1).astype(BF16)
    zero = jnp.zeros_like(w1c)
    w1bd = jnp.concatenate([jnp.concatenate([w1c, zero], axis=-1), jnp.concatenate([zero, w1c], axis=-1)], axis=-2)
    w1bd = w1bd.reshape(2, CMP_STRIDE // 2, 4 * HEAD_DIM, 4 * CMP_HIDDEN)
    return pe, w1c, w1bd, w2.astype(BF16)


def _compress_prompt(cmp_kv, pe, w1c, w2b):
    b, nseg, seg_lanes = cmp_kv.shape
    return pl.pallas_call(
        functools.partial(_compress_prompt_kernel, nseg=nseg),
        grid=(b,),
        in_specs=[pl.BlockSpec((1, nseg, seg_lanes), lambda i: (i, 0, 0)),
                  _resident(pe.shape), _resident(w1c.shape), _resident(w2b.shape)],
        out_specs=pl.BlockSpec((1, 2, nseg, KV_LANES // 2), lambda i: (i, 0, 0, 0)),
        out_shape=jax.ShapeDtypeStruct((b, 2, nseg, KV_LANES // 2), BF16),
        scratch_shapes=[pltpu.VMEM((2 * KV_HEADS, nseg, 2 * CMP_HIDDEN), F32)],
        compiler_params=_params("arbitrary"),
        name="compress_prompt",
    )(cmp_kv, pe, w1c, w2b)


PAGES_PER_STEP = 16
DEC_SEQS_PER_STEP = 2
SEGS_PER_PAGE = PAGE_SIZE // CMP_STRIDE


def _page_spec(layer, k):
    return pl.BlockSpec((1, 1, 2, KV_HEADS, HEAD_DIM, PAGE_SIZE),
                        lambda i, s, pt: (layer, pt[i, s * PAGES_PER_STEP + k], 0, 0, 0, 0))


def _compress_paged_kernel(pt_ref, *refs, nseg, n_steps):
    del pt_ref
    pages = refs[:PAGES_PER_STEP]
    pe_ref, w1_ref, w1bd_ref, w2_ref, out_ref, rows_ref, acc_ref = refs[PAGES_PER_STEP:]
    step = pl.program_id(1)
    pair = 2 * HEAD_DIM
    for k, page in enumerate(pages):
        for s in range(2):
            for gp in range(KV_HEADS // 2):
                tile = page[0, 0, s, 2 * gp:2 * gp + 2].reshape(pair, PAGE_SIZE)
                rows_ref[s * 2 + gp, k * PAGE_SIZE:(k + 1) * PAGE_SIZE, :] = tile.T
    segs = PAGES_PER_STEP * SEGS_PER_PAGE
    seg0 = pl.multiple_of(step * segs, segs)
    for sgp in range(KV_HEADS):
        d = None
        for l in range(0, CMP_STRIDE, 2):
            xl = jnp.concatenate([rows_ref[sgp, pl.ds(l + i, segs, stride=CMP_STRIDE), :] for i in range(2)],
                                 axis=1).astype(BF16)
            dl = _dot(xl, w1bd_ref[sgp // 2, l // 2])
            d = dl if d is None else d + dl
        acc_ref[sgp, pl.ds(seg0, segs), :] = d

    @pl.when(step == n_steps - 1)
    def _():
        width = 2 * CMP_HIDDEN

        def seg_proj(s, g):
            return acc_ref[s * 2 + g // 2, :, (g % 2) * width:(g % 2 + 1) * width]

        _compress_finish(seg_proj, pe_ref, w1_ref, w2_ref, out_ref.at[0], nseg)


def _compress_paged(cache_t, layer, page_table, pe, w1c, w1bd, w2b):
    b, n_pages = page_table.shape
    n_steps = n_pages // PAGES_PER_STEP
    nseg = n_pages * SEGS_PER_PAGE
    const = lambda a: pl.BlockSpec(a.shape, lambda i, s, pt: (0,) * a.ndim)
    grid_spec = pltpu.PrefetchScalarGridSpec(
        num_scalar_prefetch=1,
        grid=(b, n_steps),
        in_specs=[_page_spec(layer, k) for k in range(PAGES_PER_STEP)] + [const(pe), const(w1c), const(w1bd), const(w2b)],
        out_specs=pl.BlockSpec((1, 2, nseg, KV_LANES // 2), lambda i, s, pt: (i, 0, 0, 0)),
        scratch_shapes=[pltpu.VMEM((KV_HEADS, PAGES_PER_STEP * PAGE_SIZE, 2 * HEAD_DIM), F32),
                        pltpu.VMEM((KV_HEADS, nseg, 4 * CMP_HIDDEN), F32)],
    )
    return pl.pallas_call(
        functools.partial(_compress_paged_kernel, nseg=nseg, n_steps=n_steps),
        grid_spec=grid_spec,
        out_shape=jax.ShapeDtypeStruct((b, 2, nseg, KV_LANES // 2), BF16),
        compiler_params=_params("arbitrary", "arbitrary"),
        name="compress_paged",
    )(page_table, *([cache_t] * PAGES_PER_STEP), pe, w1c, w1bd, w2b)


def _importance_matrix(nc_pad, nc, n_sel_pad, n_sel):
    i = np.arange(nc_pad)[:, None]
    j = np.arange(n_sel_pad)[None, :]
    overlap = np.maximum(np.minimum(i * CMP_STRIDE + CMP_BLOCK, (j + 1) * SEL_BLOCK)
                         - np.maximum(i * CMP_STRIDE, j * SEL_BLOCK), 0)
    m = overlap / CMP_STRIDE * (i < nc) * (j < n_sel)
    return jnp.asarray(m, dtype=BF16)


def _topk_mask(score, k, axis):
    n = score.shape[axis]

    def scan(sc, unrolled):
        idx = lax.broadcasted_iota(jnp.int32, sc.shape, axis).astype(F32)

        def body(_, carry):
            sc, sel = carry
            m = jnp.max(sc, axis=axis, keepdims=True)
            first = jnp.min(jnp.where(sc == m, idx, float(n)), axis=axis, keepdims=True)
            hit = idx == first
            return jnp.where(hit, -jnp.inf, sc), jnp.where(hit, 1.0, sel)

        carry = (sc, jnp.zeros(sc.shape, F32))
        if unrolled:
            for i in range(k):
                carry = body(i, carry)
            return carry[1]
        return lax.fori_loop(0, k, body, carry)[1]

    if axis == 0 and score.shape[1] % LANE == 0:
        return jnp.concatenate([scan(score[:, c:c + LANE], True) for c in range(0, score.shape[1], LANE)], axis=1)
    return scan(score, False)


def _block_scores(imp, qpos, n_valid, axis):
    jj = lax.broadcasted_iota(jnp.int32, imp.shape, axis)
    cur = qpos // SEL_BLOCK
    forced = (jj == 0) | (jj == cur) | (jj == cur - 1)
    score = jnp.where(forced, BIG, jnp.where(jj <= cur, imp, NEG))
    return jnp.where(jj < n_valid, score, -jnp.inf)


def _cmp_select_kernel(q_ref, kc_ref, vct_ref, impt_ref, gate_ref, o_ref, qa_ref, *, tq, nc, n_sel):
    q0 = pl.program_id(1) * tq
    rows = GROUP * tq
    qpos = q0 + lax.broadcasted_iota(jnp.int32, (1, rows), 1) % tq
    cidx = lax.broadcasted_iota(jnp.int32, (kc_ref.shape[2], 1), 0)
    valid = (cidx * CMP_STRIDE + CMP_BLOCK - 1 <= qpos) & (cidx < nc)
    tpos = q0 + lax.broadcasted_iota(jnp.int32, (1, tq), 1)
    gates_t = gate_ref[...].T
    q2 = (q_ref[0] * (SCALE * LOG2E)).astype(BF16)
    heads = [q2[:, h * HEAD_DIM:(h + 1) * HEAD_DIM] for h in range(N_HEADS)]
    scores = [_dot_nt(kc_ref[0, g], jnp.concatenate(heads[g * GROUP:(g + 1) * GROUP], axis=0))
              for g in range(KV_HEADS)]
    o_t, imp = [], []
    for g in range(KV_HEADS):
        s = jnp.where(valid, scores[g], NEG)
        e = jnp.exp2(s - jnp.max(s, axis=0, keepdims=True))
        p = jnp.where(valid, e / jnp.sum(e, axis=0, keepdims=True), 0.0)
        o = _dot(vct_ref[0, g], p.astype(BF16))
        p_grp = p[:, 0:tq]
        for r in range(1, GROUP):
            p_grp = p_grp + p[:, r * tq:(r + 1) * tq]
        imp.append(_dot(impt_ref[...], p_grp.astype(BF16)))
        for r in range(GROUP):
            h = g * GROUP + r
            o_t.append(o[:, r * tq:(r + 1) * tq] * gates_t[h:h + 1, :])
    o_ref[0] = jnp.concatenate(o_t, axis=0).T
    imp = jnp.concatenate(imp, axis=1)
    tpos = jnp.concatenate([tpos] * KV_HEADS, axis=1)
    sel = _topk_mask(_block_scores(imp, tpos, n_sel, 0), min(SEL_TOPK, n_sel), 0)
    bias_t = jnp.where(sel > 0.0, 0.0, NEG)
    bias = jnp.concatenate([bias_t[:, g * tq:(g + 1) * tq] for g in range(KV_HEADS)], axis=0).T.astype(BF16)
    for h in range(N_HEADS):
        g = h // GROUP
        qa_ref[0, h, :, 0:HEAD_DIM] = heads[h]
        qa_ref[0, h, :, HEAD_DIM:] = bias[:, g * HEAD_DIM:(g + 1) * HEAD_DIM]


def _cmp_select(q, kch, vct, gates, tq):
    b, t, _ = q.shape
    nseg = kch.shape[2]
    n_sel = t // SEL_BLOCK
    assert n_sel <= HEAD_DIM
    imp_t = _importance_matrix(nseg, nseg - 1, HEAD_DIM, n_sel).T
    return pl.pallas_call(
        functools.partial(_cmp_select_kernel, tq=tq, nc=nseg - 1, n_sel=n_sel),
        grid=(b, t // tq),
        in_specs=[pl.BlockSpec((1, tq, NSA_Q), lambda i, j: (i, j, 0)),
                  pl.BlockSpec((1, KV_HEADS, nseg, HEAD_DIM), lambda i, j: (i, 0, 0, 0)),
                  pl.BlockSpec((1, KV_HEADS, HEAD_DIM, nseg), lambda i, j: (i, 0, 0, 0)),
                  pl.BlockSpec((HEAD_DIM, nseg), lambda i, j: (0, 0)),
                  pl.BlockSpec((tq, LANE), lambda i, j: (i * (t // tq) + j, 0))],
        out_specs=[pl.BlockSpec((1, tq, NSA_Q), lambda i, j: (i, j, 0)),
                   pl.BlockSpec((1, N_HEADS, tq, 2 * HEAD_DIM), lambda i, j: (i, 0, j, 0))],
        out_shape=[jax.ShapeDtypeStruct((b, t, NSA_Q), F32),
                   jax.ShapeDtypeStruct((b, N_HEADS, t, 2 * HEAD_DIM), BF16)],
        compiler_params=_params("arbitrary", "arbitrary"),
        name="cmp_select",
    )(q, kch, vct, imp_t, gates)


def _flash_kernel(qa_ref, ks_ref, vst_ref, kw_ref, vwt_ref, oc_ref, gate_ref, o_ref, *, tq, tk):
    g = pl.program_id(1)
    q0 = pl.program_id(2) * tq
    heads = [qa_ref[0, r] for r in range(GROUP)]

    def softmax_step(m, acc, s, v_t):
        m_new = jnp.maximum(m, jnp.max(s, axis=0, keepdims=True))
        a = jnp.exp2(m - m_new)
        p = jnp.exp2(s - m_new)
        return m_new, a * acc + _dot(v_t, p.astype(BF16))

    def tile_pair(j, carry):
        carry = list(carry)
        k0 = pl.multiple_of(j * (2 * tk), 2 * tk)
        starts = [k0, k0 + tk]
        scores = [[_dot_nt(ks_ref[0, 0, pl.ds(st, tk), :], heads[r]) for r in range(GROUP)] for st in starts]
        for st, s_tile in zip(starts, scores):
            v_t = vst_ref[0, 0, :, pl.ds(st, tk)]
            for r in range(GROUP):
                carry[r] = softmax_step(*carry[r], s_tile[r], v_t)
        return tuple(carry)

    init = tuple((jnp.full((1, tq), -jnp.inf, F32), jnp.zeros((V_ROWS, tq), F32)) for _ in range(GROUP))
    carry = lax.fori_loop(0, q0 // tq, tile_pair, init)

    k_loc = lax.broadcasted_iota(jnp.int32, (tk, 1), 0)
    q_loc = lax.broadcasted_iota(jnp.int32, (1, tk), 1)
    causal, older = k_loc <= q_loc, k_loc > q_loc
    plan = []
    for r in range(GROUP):
        for half in range(2):
            qs = q0 + half * tk
            plan += [(r, half, 0, q0 + i * tk, 'causal' if i == half else None) for i in range(half + 1)]
            plan += [(r, half, 1, qs - 2 * tk, 'older'), (r, half, 1, qs - tk, 'valid'), (r, half, 1, qs, 'causal')]
    q_half = [[heads[r][0:tk], heads[r][tk:tq]] for r in range(GROUP)]
    k_refs, v_refs = (ks_ref, kw_ref), (vst_ref, vwt_ref)
    clamp = lambda st: pl.multiple_of(jnp.maximum(st, 0), tk)
    scores = [_dot_nt(k_refs[br][0, 0, pl.ds(clamp(st), tk), :], q_half[r][half]) for r, half, br, st, _ in plan]
    state = {}
    for r in range(GROUP):
        m, acc = carry[r]
        for half in range(2):
            cols = slice(half * tk, (half + 1) * tk)
            state[r, half, 0] = (m[:, cols], acc[:, cols])
            state[r, half, 1] = (jnp.full((1, tk), -jnp.inf, F32), jnp.zeros((V_ROWS, tk), F32))
    for (r, half, br, st, kind), s in zip(plan, scores):
        if kind == 'causal':
            s = jnp.where(causal, s, NEG)
        elif kind == 'older':
            s = jnp.where(older & (st >= 0), s, NEG)
        elif kind == 'valid':
            s = jnp.where(st >= 0, s, NEG)
        state[r, half, br] = softmax_step(*state[r, half, br], s, v_refs[br][0, 0, :, pl.ds(clamp(st), tk)])

    gates_t = gate_ref[...].T
    sub = lax.broadcasted_iota(jnp.int32, gates_t.shape, 0)
    parts = []
    for r in range(GROUP):
        acc = jnp.concatenate([state[r, half, 0][1] for half in range(2)], axis=1)
        acc_w = jnp.concatenate([state[r, half, 1][1] for half in range(2)], axis=1)
        g_sel = jnp.sum(jnp.where(sub == N_HEADS + g * GROUP + r, gates_t, 0.0), axis=0, keepdims=True)
        g_win = jnp.sum(jnp.where(sub == 2 * N_HEADS + g * GROUP + r, gates_t, 0.0), axis=0, keepdims=True)
        d_sel, d_win = acc[HEAD_DIM:HEAD_DIM + 1], acc_w[HEAD_DIM:HEAD_DIM + 1]
        parts.append(acc[:HEAD_DIM] * (g_sel / d_sel) + acc_w[:HEAD_DIM] * (g_win / d_win))
    o = oc_ref[0] + jnp.concatenate(parts, axis=0).T
    o_ref[0] = o.astype(o_ref.dtype)


def _flash(qa, ks, vst, kw, vwt, oc, gates, tq, tk):
    assert tq == 2 * tk == WINDOW
    b, _, t, _ = qa.shape
    grp_lanes = GROUP * HEAD_DIM
    keys = pl.BlockSpec((1, 1, t, 2 * HEAD_DIM), lambda i, g, j: (i, g, 0, 0))
    vals = pl.BlockSpec((1, 1, V_ROWS, t), lambda i, g, j: (i, g, 0, 0))
    return pl.pallas_call(
        functools.partial(_flash_kernel, tq=tq, tk=tk),
        grid=(b, KV_HEADS, t // tq),
        in_specs=[pl.BlockSpec((1, GROUP, tq, 2 * HEAD_DIM), lambda i, g, j: (i, g, j, 0)),
                  keys, vals, keys, vals,
                  pl.BlockSpec((1, tq, grp_lanes), lambda i, g, j: (i, j, g)),
                  pl.BlockSpec((tq, LANE), lambda i, g, j: (i * (t // tq) + j, 0))],
        out_specs=pl.BlockSpec((1, tq, grp_lanes), lambda i, g, j: (i, j, g)),
        out_shape=jax.ShapeDtypeStruct((b, t, NSA_Q), BF16),
        compiler_params=_params("arbitrary", "arbitrary", "arbitrary"),
        name="nsa_flash",
    )(qa, ks, vst, kw, vwt, oc, gates)


def _out_proj_kernel(x_ref, o_ref, w_ref, mod_ref, y_ref):
    y = _dot(o_ref[...].astype(BF16), w_ref[...])
    y_ref[...] = x_ref[...] + (1.0 + mod_ref[2]) * y


def _out_proj(x, o, w, mod, tm, tiles_per_seq=1):
    rows, k = o.shape
    return pl.pallas_call(
        _out_proj_kernel,
        grid=(rows // tm,),
        in_specs=[pl.BlockSpec((tm, D_MODEL), lambda i: (i, 0)),
                  pl.BlockSpec((tm, k), lambda i: (i, 0)),
                  _resident((k, D_MODEL)),
                  _mod_spec(mod, tm, tiles_per_seq)],
        out_specs=pl.BlockSpec((tm, D_MODEL), lambda i: (i, 0)),
        out_shape=jax.ShapeDtypeStruct((rows, D_MODEL), F32),
        compiler_params=_params("arbitrary"),
        name="out_proj",
    )(x, o, w, mod)


def _diag_fold(acc):
    rows = acc.shape[0]
    t = rows // (GROUP * KV_HEADS)
    row_g = (lax.broadcasted_iota(jnp.int32, (rows, 1), 0) // t) % KV_HEADS
    out = jnp.zeros((rows, HEAD_DIM), F32)
    for g in range(KV_HEADS):
        out = out + jnp.where(row_g == g, acc[:, g * HEAD_DIM:(g + 1) * HEAD_DIM], 0.0)
    return out


def _dec_cmp_select_kernel(q_ref, kc_ref, vc_ref, imp_ref, o_ref, bias_ref, *, t, nc, n_sel, past):
    rows = q_ref.shape[1]
    qt = KV_HEADS * t
    qpos = past + lax.broadcasted_iota(jnp.int32, (rows, 1), 0) % t
    cidx = lax.broadcasted_iota(jnp.int32, (1, kc_ref.shape[1]), 1)
    valid = (cidx * CMP_STRIDE + CMP_BLOCK - 1 <= qpos) & (cidx < nc)
    s = jnp.where(valid, _dot_nt(q_ref[0], kc_ref[0]), NEG)
    e = jnp.exp(s - jnp.max(s, axis=-1, keepdims=True))
    p = jnp.where(valid, e / jnp.sum(e, axis=-1, keepdims=True), 0.0)
    o_ref[0] = _diag_fold(_dot(p.astype(BF16), vc_ref[0]))
    p_grp = p[0:qt]
    for r in range(1, GROUP):
        p_grp = p_grp + p[r * qt:(r + 1) * qt]
    imp = _dot(p_grp.astype(BF16), imp_ref[...])
    sel = _topk_mask(_block_scores(imp, qpos[0:qt], n_sel, 1), min(SEL_TOPK, n_sel), 1)
    bias = jnp.where(sel > 0.0, 0.0, NEG).astype(BF16)
    for r in range(GROUP):
        bias_ref[0, r * qt:(r + 1) * qt, :] = bias


def _dec_cmp_select(qbd, kc, vc, t, past):
    b, rows, lanes = qbd.shape
    nseg = kc.shape[1]
    n_sel = -(-(past + t) // SEL_BLOCK)
    n_sel_pad = -(-n_sel // LANE) * LANE
    imp_m = _importance_matrix(nseg, nseg - 1, n_sel_pad, n_sel)
    return pl.pallas_call(
        functools.partial(_dec_cmp_select_kernel, t=t, nc=nseg - 1, n_sel=n_sel, past=past),
        grid=(b,),
        in_specs=[pl.BlockSpec((1, rows, lanes), lambda i: (i, 0, 0)),
                  pl.BlockSpec((1, nseg, lanes), lambda i: (i, 0, 0)),
                  pl.BlockSpec((1, nseg, lanes), lambda i: (i, 0, 0)),
                  pl.BlockSpec((nseg, n_sel_pad), lambda i: (0, 0))],
        out_specs=[pl.BlockSpec((1, rows, HEAD_DIM), lambda i: (i, 0, 0)),
                   pl.BlockSpec((1, rows, n_sel_pad), lambda i: (i, 0, 0))],
        out_shape=[jax.ShapeDtypeStruct((b, rows, HEAD_DIM), F32),
                   jax.ShapeDtypeStruct((b, rows, n_sel_pad), BF16)],
        compiler_params=_params("arbitrary"),
        name="dec_cmp_select",
    )(qbd, kc, vc, imp_m)


def _dec_attend_kernel(pt_ref, *refs, t, n_steps, past):
    del pt_ref
    n_page_refs = DEC_SEQS_PER_STEP * PAGES_PER_STEP
    pages = refs[:n_page_refs]
    (q_ref, bias_ref, e_ref, tail_ref, wbuf_ref, wtail_ref,
     osel_ref, owin_ref, m_ref, l_ref, acc_ref) = refs[n_page_refs:]
    step = pl.program_id(1)
    half = KV_LANES // 2
    rows = q_ref.shape[1]
    tq = lax.broadcasted_iota(jnp.int32, (rows, 1), 0) % t

    @pl.when(step == 0)
    def _():
        m_ref[...] = jnp.full(m_ref.shape, -jnp.inf, F32)
        l_ref[...] = jnp.zeros(l_ref.shape, F32)
        acc_ref[...] = jnp.zeros(acc_ref.shape, F32)

    def update(sq, s, weighted_values):
        m = m_ref[sq]
        m_new = jnp.maximum(m, jnp.max(s, axis=-1, keepdims=True))
        a = jnp.exp(m - m_new)
        p = jnp.exp(s - m_new)
        l_ref[sq] = a * l_ref[sq] + jnp.sum(p, axis=-1, keepdims=True)
        acc_ref[sq] = a * acc_ref[sq] + weighted_values(p.astype(BF16))
        m_ref[sq] = m_new

    seqs = range(DEC_SEQS_PER_STEP)
    mine = lambda sq: pages[sq * PAGES_PER_STEP:(sq + 1) * PAGES_PER_STEP]
    transposed = lambda sq, s: jnp.concatenate([page[0, 0, s].reshape(half, PAGE_SIZE) for page in mine(sq)],
                                               axis=1).astype(BF16)
    scores = [_dot(q_ref[sq], transposed(sq, 0)) + _dot_nt(bias_ref[sq], e_ref[...]) for sq in seqs]
    for sq in seqs:
        v_t = transposed(sq, 1)
        update(sq, scores[sq], lambda p: _dot_nt(p, v_t))

    @pl.when(step == n_steps - 1)
    def _():
        for sq in seqs:
            q, bias = q_ref[sq], bias_ref[sq]
            tail = tail_ref[sq]
            tpos = lax.broadcasted_iota(jnp.int32, (1, tail.shape[0]), 1)
            blk = past // SEL_BLOCK
            lane = lax.broadcasted_iota(jnp.int32, bias.shape, 1)
            b_tail = jnp.sum(jnp.where(lane == blk, bias.astype(F32), 0.0), axis=-1, keepdims=True)
            s = _dot_nt(q, tail[:, :half].astype(BF16)) + b_tail
            update(sq, jnp.where(tpos <= tq, s, NEG), lambda p: _dot(p, tail[:, half:].astype(BF16)))
            osel_ref[sq] = _diag_fold(acc_ref[sq] / l_ref[sq])

            wb = wbuf_ref.shape[-1]
            wk_t = wbuf_ref[0, sq, 0].reshape(half, wb).astype(BF16)
            wv_t = wbuf_ref[0, sq, 1].reshape(half, wb).astype(BF16)
            wtail = wtail_ref[sq]
            ipos = lax.broadcasted_iota(jnp.int32, (1, wb), 1)
            d = (past + tq) - (past - wb + ipos)
            s1 = jnp.where((d >= 0) & (d < WINDOW), _dot(q, wk_t), NEG)
            s2 = jnp.where(tpos <= tq, _dot_nt(q, wtail[:, :half].astype(BF16)), NEG)
            m = jnp.maximum(jnp.max(s1, axis=-1, keepdims=True), jnp.max(s2, axis=-1, keepdims=True))
            p1, p2 = jnp.exp(s1 - m), jnp.exp(s2 - m)
            den = jnp.sum(p1, axis=-1, keepdims=True) + jnp.sum(p2, axis=-1, keepdims=True)
            o = _dot_nt(p1.astype(BF16), wv_t) + _dot(p2.astype(BF16), wtail[:, half:].astype(BF16))
            owin_ref[sq] = _diag_fold(o / den)


def _dec_attend(cache_t, win_t, layer, page_table, qbd, bias, onehot, sel_tail, win_tail, t, past):
    b, n_pages = page_table.shape
    n_steps = n_pages // PAGES_PER_STEP
    rows, lanes = qbd.shape[1:]
    n_sel_pad = bias.shape[2]
    keys_per_step = PAGES_PER_STEP * PAGE_SIZE
    ns = DEC_SEQS_PER_STEP
    assert b % ns == 0
    per_seq = lambda arr: pl.BlockSpec((ns,) + arr.shape[1:], lambda i, s, pt: (i,) + (0,) * (arr.ndim - 1))

    def page_spec(sq, k):
        return pl.BlockSpec((1, 1, 2, KV_HEADS, HEAD_DIM, PAGE_SIZE),
                            lambda i, s, pt: (layer, pt[i * ns + sq, s * PAGES_PER_STEP + k], 0, 0, 0, 0))

    n_page_refs = ns * PAGES_PER_STEP
    grid_spec = pltpu.PrefetchScalarGridSpec(
        num_scalar_prefetch=1,
        grid=(b // ns, n_steps),
        in_specs=[page_spec(sq, k) for sq in range(ns) for k in range(PAGES_PER_STEP)]
        + [per_seq(qbd), per_seq(bias),
           pl.BlockSpec((keys_per_step, n_sel_pad), lambda i, s, pt: (s, 0)),
           per_seq(sel_tail),
           pl.BlockSpec((1, ns) + win_t.shape[2:], lambda i, s, pt: (layer, i, 0, 0, 0, 0)),
           per_seq(win_tail)],
        out_specs=[pl.BlockSpec((ns, rows, HEAD_DIM), lambda i, s, pt: (i, 0, 0)),
                   pl.BlockSpec((ns, rows, HEAD_DIM), lambda i, s, pt: (i, 0, 0))],
        scratch_shapes=[pltpu.VMEM((ns, rows, 1), F32), pltpu.VMEM((ns, rows, 1), F32),
                        pltpu.VMEM((ns, rows, lanes), F32)],
    )
    return pl.pallas_call(
        functools.partial(_dec_attend_kernel, t=t, n_steps=n_steps, past=past),
        grid_spec=grid_spec,
        out_shape=[jax.ShapeDtypeStruct((b, rows, HEAD_DIM), F32)] * 2,
        compiler_params=_params("arbitrary", "arbitrary"),
        name="dec_attend",
    )(page_table, *([cache_t] * n_page_refs), qbd, bias, onehot, sel_tail, win_t, win_tail)


def _gated_out_proj_kernel(x_ref, oc_ref, os_ref, ow_ref, gate_ref, ex_ref, w_ref, mod_ref, y_ref):
    gates = gate_ref[...]
    o = jnp.zeros(oc_ref.shape, F32)
    for k, ref in enumerate((oc_ref, os_ref, ow_ref)):
        o = o + _dot_exact(gates, ex_ref[k]) * ref[...]
    y_ref[...] = x_ref[...] + (1.0 + mod_ref[2]) * _dot(o.astype(BF16), w_ref[...])


def _gate_expansion():
    ex = np.zeros((3, LANE, NSA_Q), np.float32)
    for k in range(3):
        for h in range(N_HEADS):
            ex[k, k * N_HEADS + h, h * HEAD_DIM:(h + 1) * HEAD_DIM] = 1.0
    return jnp.asarray(ex)


def _gated_out_proj(x, oc, osel, owin, gates, w, mod):
    rows = x.shape[0]
    full = lambda n: pl.BlockSpec((rows, n), lambda i: (0, 0))
    return pl.pallas_call(
        _gated_out_proj_kernel,
        grid=(1,),
        in_specs=[full(D_MODEL), full(NSA_Q), full(NSA_Q), full(NSA_Q), full(LANE),
                  pl.BlockSpec((3, LANE, NSA_Q), lambda i: (0, 0, 0)),
                  pl.BlockSpec((NSA_Q, D_MODEL), lambda i: (0, 0)),
                  pl.BlockSpec((3, rows, D_MODEL), lambda i: (0, 0, 0))],
        out_specs=full(D_MODEL),
        out_shape=jax.ShapeDtypeStruct((rows, D_MODEL), F32),
        compiler_params=_params("arbitrary"),
        name="gated_out_proj",
    )(x, oc, osel, owin, gates, _gate_expansion(), w, mod)


def _ml_proj_kernel(x_ref, g_ref, mod_ref, w_ref, bg_ref, xm_ref, z_ref, gl_ref):
    h = _modulated_norm(x_ref[...], g_ref[...], mod_ref[0], mod_ref[1]).astype(BF16)
    proj = _dot(h, w_ref[...])
    xm_ref[...] = proj[:, :ML_INNER]
    z_ref[...] = proj[:, ML_INNER:2 * ML_INNER]
    gl = proj[:, 2 * ML_INNER:] + bg_ref[...]
    lane = lax.broadcasted_iota(jnp.int32, gl.shape, 1)
    gl_ref[...] = jnp.where(lane < ML_HEADS, gl, jax.nn.log_sigmoid(gl))


def _ml_proj(x, g, mod, w, b_gate, tm, tiles_per_seq=1):
    rows = x.shape[0]
    row = lambda n: pl.BlockSpec((tm, n), lambda i: (i, 0))
    return pl.pallas_call(
        _ml_proj_kernel,
        grid=(rows // tm,),
        in_specs=[row(D_MODEL), _resident((1, D_MODEL)), _mod_spec(mod, tm, tiles_per_seq),
                  _resident((D_MODEL, ML_N)), _resident((1, LANE))],
        out_specs=[row(ML_INNER), row(ML_INNER), row(LANE)],
        out_shape=[jax.ShapeDtypeStruct((rows, n), F32) for n in (ML_INNER, ML_INNER, LANE)],
        compiler_params=_params("arbitrary"),
        name="ml_proj",
    )(x, g, mod, w, b_gate)


def _conv_qkv(taps, cw_ref, cb_ref, wqk_ref, wv_ref, xc_ref, q_ref, k_ref, v_ref):
    xc = cb_ref[...]
    for w in range(CONV_W):
        xc = xc + taps[w] * cw_ref[w:w + 1, :]
    xc = jax.nn.silu(xc)
    xc_ref[...] = xc
    xcb = xc.astype(BF16)
    xmb = taps[CONV_W - 1].astype(BF16)
    for j in range(ML_INNER // MXU_DIM):
        sl = slice(j * MXU_DIM, (j + 1) * MXU_DIM)
        qk = _dot(xcb[:, sl], wqk_ref[j])
        q_ref[:, sl] = qk[:, :MXU_DIM].astype(q_ref.dtype)
        k_ref[:, sl] = (qk[:, MXU_DIM:] * (ML_HEAD_DIM ** -0.5)).astype(k_ref.dtype)
        v_ref[:, sl] = _dot(xmb[:, sl], wv_ref[j]).astype(v_ref.dtype)


def _ml_proj_conv_kernel(x_ref, g_ref, mod_ref, w_ref, bg_ref, cw_ref, cb_ref, wqk_ref, wv_ref,
                         z_ref, gl_ref, xc_ref, q_ref, k_ref, v_ref, tail_ref, pad_ref, *, tm, tiles_per_seq):
    tile = pl.program_id(0) % tiles_per_seq
    h = _modulated_norm(x_ref[...], g_ref[...], mod_ref[0], mod_ref[1]).astype(BF16)
    proj = _dot(h, w_ref[...])
    xm = proj[:, :ML_INNER]
    z_ref[...] = proj[:, ML_INNER:2 * ML_INNER]
    gl = proj[:, 2 * ML_INNER:] + bg_ref[...]
    lane = lax.broadcasted_iota(jnp.int32, gl.shape, 1)
    gl_ref[...] = jnp.where(lane < ML_HEADS, gl, jax.nn.log_sigmoid(gl))

    @pl.when(tile == 0)
    def _():
        pad_ref[0:8, :] = jnp.zeros((8, ML_INNER), F32)

    pad_ref[8:, :] = xm
    taps = [pad_ref[pl.ds(8 - (CONV_W - 1) + w, tm), :] for w in range(CONV_W)]
    _conv_qkv(taps, cw_ref, cb_ref, wqk_ref, wv_ref, xc_ref, q_ref, k_ref, v_ref)
    pad_ref[0:8, :] = xm[tm - 8:, :]

    @pl.when(tile == tiles_per_seq - 1)
    def _():
        tail_ref[0] = xm[tm - 8:, :]


def _ml_conv_sample_kernel(t0_ref, t1_ref, t2_ref, t3_ref, cw_ref, cb_ref, wqk_ref, wv_ref,
                           xc_ref, q_ref, k_ref, v_ref):
    taps = [r[...] for r in (t0_ref, t1_ref, t2_ref, t3_ref)]
    _conv_qkv(taps, cw_ref, cb_ref, wqk_ref, wv_ref, xc_ref, q_ref, k_ref, v_ref)


def _headwise_weights(wq, wk, wv):
    def blockdiag(w):
        n_chunks = ML_INNER // MXU_DIM
        per = MXU_DIM // 4
        wc = w.reshape(n_chunks, per, 4, 4)
        eye = jnp.eye(per, dtype=w.dtype)
        return jnp.einsum('jncd,nm->jncmd', wc, eye).reshape(n_chunks, MXU_DIM, MXU_DIM)
    return jnp.concatenate([blockdiag(wq), blockdiag(wk)], axis=-1).astype(BF16), blockdiag(wv).astype(BF16)


def _ml_proj_conv(x, g, mod, w, tm, tiles_per_seq):
    rows = x.shape[0]
    row = lambda n: pl.BlockSpec((tm, n), lambda i: (i, 0))
    f32 = lambda n: jax.ShapeDtypeStruct((rows, n), F32)
    bf16 = jax.ShapeDtypeStruct((rows, ML_INNER), BF16)
    n_seq = rows // (tm * tiles_per_seq)
    consts = [w['w_in'], w['bg'], w['cw'], w['cb'], w['wqk'], w['wv']]
    return pl.pallas_call(
        functools.partial(_ml_proj_conv_kernel, tm=tm, tiles_per_seq=tiles_per_seq),
        grid=(rows // tm,),
        in_specs=[row(D_MODEL), _resident((1, D_MODEL)), _mod_spec(mod, tm, tiles_per_seq)]
        + [_resident(c.shape) for c in consts],
        out_specs=[row(ML_INNER), row(LANE), row(ML_INNER), row(ML_INNER), row(ML_INNER), row(ML_INNER),
                   pl.BlockSpec((1, 8, ML_INNER), lambda i: (i // tiles_per_seq, 0, 0))],
        out_shape=[f32(ML_INNER), f32(LANE), f32(ML_INNER), bf16, bf16, bf16,
                   jax.ShapeDtypeStruct((n_seq, 8, ML_INNER), F32)],
        scratch_shapes=[pltpu.VMEM((tm + 8, ML_INNER), F32)],
        compiler_params=_params("arbitrary"),
        name="ml_proj_conv",
    )(x, g, mod, *consts)


def _ml_conv_sample(taps, cw, cb, wqk, wv):
    rows = taps[0].shape[0]
    full = pl.BlockSpec((rows, ML_INNER), lambda i: (0, 0))
    const = lambda a: pl.BlockSpec(a.shape, lambda i: (0,) * a.ndim)
    out = jax.ShapeDtypeStruct((rows, ML_INNER), F32)
    return pl.pallas_call(
        _ml_conv_sample_kernel,
        grid=(1,),
        in_specs=[full] * 4 + [const(cw), const(cb), const(wqk), const(wv)],
        out_specs=[full] * 4,
        out_shape=[out] * 4,
        compiler_params=_params("arbitrary"),
        name="ml_conv_sample",
    )(*taps, cw, cb, wqk, wv)


def _ml_cell_kernel(*refs, chunk, gated_norm, aliased_stack):
    if aliased_stack:
        refs = refs[1:]
    if gated_norm:
        (q_ref, k_ref, v_ref, gl_ref, c0_ref, n0_ref, m0_ref, z_ref, xc_ref, gn_ref, skip_ref,
         h_ref, c_ref, n_ref, m_ref) = refs
    else:
        q_ref, k_ref, v_ref, gl_ref, c0_ref, n0_ref, m0_ref, h_ref, c_ref, n_ref, m_ref = refs
    step = pl.program_id(1)

    @pl.when(step == 0)
    def _():
        c_ref[...] = c0_ref[0]
        n_ref[...] = n0_ref[0]
        m_ref[...] = m0_ref[...]

    gl = gl_ref[0]
    ti = lax.broadcasted_iota(jnp.int32, (chunk, chunk), 0)
    si = lax.broadcasted_iota(jnp.int32, (chunk, chunk), 1)
    tri = si <= ti
    cum = _dot_exact(tri.astype(F32), gl)
    sel = (lax.broadcasted_iota(jnp.int32, (8, LANE), 0) == lax.broadcasted_iota(jnp.int32, (8, LANE), 1)).astype(F32)
    gl_rows = _dot_nt_exact(sel, gl)
    cum_rows = _dot_nt_exact(sel, cum)
    for h in range(ML_HEADS):
        hs = slice(h * ML_HEAD_DIM, (h + 1) * ML_HEAD_DIM)
        q = q_ref[0, :, hs].astype(BF16)
        k = k_ref[0, :, hs].astype(F32)
        v = v_ref[0, :, hs].astype(BF16)
        ig_col = gl[:, h:h + 1]
        ig_row = gl_rows[h:h + 1, :]
        b_col = cum[:, ML_HEADS + h:ML_HEADS + h + 1]
        b_row = cum_rows[ML_HEADS + h:ML_HEADS + h + 1, :]
        b_last = b_col[chunk - 1:chunk, :]
        m_prev = m_ref[0, 0:1, h:h + 1]
        c_prev = c_ref[0, h]
        n_prev = n_ref[0, h:h + 1, :]

        dmat = jnp.where(tri, b_col - b_row + ig_row, -jnp.inf)
        inter = b_col + m_prev
        m_t = jnp.maximum(inter, jnp.max(dmat, axis=-1, keepdims=True))
        w_int = jnp.exp(inter - m_t)
        s = _dot_nt(q, k.astype(BF16)) * jnp.exp(dmat - m_t)
        num = w_int * _dot(q, c_prev.astype(BF16)) + _dot(s.astype(BF16), v)
        den = w_int * jnp.sum(q.astype(F32) * n_prev, axis=-1, keepdims=True) + jnp.sum(s, axis=-1, keepdims=True)
        hcell = num / jnp.maximum(jnp.abs(den), jnp.exp(-m_t))
        if gated_norm:
            hg = hcell * jax.nn.sigmoid(z_ref[0, :, hs])
            hn = hg * lax.rsqrt(jnp.mean(hg * hg, axis=-1, keepdims=True) + NORM_EPS)
            hcell = hn * gn_ref[:, hs] + skip_ref[:, hs] * xc_ref[0, :, hs]
        h_ref[0, :, hs] = hcell.astype(h_ref.dtype)

        m_new = m_t[chunk - 1:chunk, :]
        w_dec = jnp.exp(b_last + m_prev - m_new)
        kw = k * jnp.exp(b_last - b_col + ig_col - m_new)
        c_ref[0, h] = w_dec * c_prev + _dot_tn(kw.astype(BF16), v)
        n_ref[0, h:h + 1, :] = w_dec * n_prev + jnp.sum(kw, axis=0, keepdims=True)
        m_ref[0, :, h:h + 1] = jnp.broadcast_to(m_new, (8, 1))


def _ml_cell(q, k, v, gl, c0, n0, layer, m0, chunk, c_stack, n_layers, layer_out, gated_norm=None):
    b, t, _ = q.shape
    n_chunks = t // chunk
    seq = lambda n: pl.BlockSpec((1, chunk, n), lambda i, c: (i, c, 0))
    c_spec = pl.BlockSpec((None, 1, ML_HEADS, ML_HEAD_DIM, ML_HEAD_DIM), lambda i, c: (layer_out, i, 0, 0, 0))
    n_spec = pl.BlockSpec((1, ML_HEADS, ML_HEAD_DIM), lambda i, c: (i, 0, 0))
    m_spec = pl.BlockSpec((1, 8, LANE), lambda i, c: (i, 0, 0))
    c0_spec = pl.BlockSpec((1, 1, ML_HEADS, ML_HEAD_DIM, ML_HEAD_DIM), lambda i, c: (layer, i, 0, 0, 0))
    n0_spec = pl.BlockSpec((1, 1, ML_HEADS, ML_HEAD_DIM), lambda i, c: (layer, i, 0, 0))
    in_specs = [seq(ML_INNER), seq(ML_INNER), seq(ML_INNER), seq(LANE), c0_spec, n0_spec, m_spec]
    args = [q, k, v, gl, c0, n0, m0]
    if gated_norm is not None:
        const = pl.BlockSpec((1, ML_INNER), lambda i, c: (0, 0))
        in_specs += [seq(ML_INNER), seq(ML_INNER), const, const]
        args += list(gated_norm)
    aliases = {}
    if c_stack is not None:
        in_specs = [pl.BlockSpec(memory_space=pl.ANY)] + in_specs
        args = [c_stack] + args
        aliases = {0: 1}
    return pl.pallas_call(
        functools.partial(_ml_cell_kernel, chunk=chunk, gated_norm=gated_norm is not None,
                          aliased_stack=c_stack is not None),
        grid=(b, n_chunks),
        in_specs=in_specs,
        out_specs=[seq(ML_INNER), c_spec, n_spec, m_spec],
        out_shape=[jax.ShapeDtypeStruct((b, t, ML_INNER), F32 if gated_norm is None else BF16),
                   jax.ShapeDtypeStruct((n_layers,) + c0.shape[1:], F32),
                   jax.ShapeDtypeStruct(n0.shape[1:], F32),
                   jax.ShapeDtypeStruct(m0.shape, F32)],
        input_output_aliases=aliases,
        compiler_params=_params("arbitrary", "arbitrary"),
        name="ml_cell",
    )(*args)


def _ml_out_kernel(x_ref, h_ref, z_ref, xc_ref, gn_ref, skip_ref, w_ref, mod_ref, y_ref):
    hg = h_ref[...] * jax.nn.sigmoid(z_ref[...])
    parts = []
    for h in range(ML_HEADS):
        hh = hg[:, h * ML_HEAD_DIM:(h + 1) * ML_HEAD_DIM]
        parts.append(hh * lax.rsqrt(jnp.mean(hh * hh, axis=-1, keepdims=True) + NORM_EPS))
    hn = jnp.concatenate(parts, axis=-1) * gn_ref[...] + skip_ref[...] * xc_ref[...]
    y_ref[...] = x_ref[...] + (1.0 + mod_ref[2]) * _dot(hn.astype(BF16), w_ref[...])


def _ml_out(x, hcell, z, xc, gn, skip, w, mod, tm, tiles_per_seq=1):
    rows = x.shape[0]
    row = lambda n: pl.BlockSpec((tm, n), lambda i: (i, 0))
    return pl.pallas_call(
        _ml_out_kernel,
        grid=(rows // tm,),
        in_specs=[row(D_MODEL), row(ML_INNER), row(ML_INNER), row(ML_INNER),
                  _resident((1, ML_INNER)), _resident((1, ML_INNER)), _resident((ML_INNER, D_MODEL)),
                  _mod_spec(mod, tm, tiles_per_seq)],
        out_specs=row(D_MODEL),
        out_shape=jax.ShapeDtypeStruct((rows, D_MODEL), F32),
        compiler_params=_params("arbitrary"),
        name="ml_out",
    )(x, hcell, z, xc, gn, skip, w, mod)


TM_PROMPT = 512
TM_ML_PROJ = 256
ML_CHUNK_PROMPT = 256
ML_CHUNK_SAMPLE = 8


def _pad_cols(w, n):
    return jnp.pad(w, ((0, 0), (0, n - w.shape[1])))


def _nsa_prompt_layer(x, g, mod, w_in, w_out, cmp_w, tables, b, t):
    tiles = t // TM_PROMPT
    q, cmp_kv, gates, cmp_t, sel_t, win_t, ks_aug, kw_aug, vs_t, vw_t = _nsa_proj(
        x, g, mod, w_in, tables, TM_PROMPT, tiles, prompt_layouts=True)
    pe, w1c, _, w2b = cmp_w
    cmp = _compress_prompt(cmp_kv.reshape(b, t // CMP_STRIDE, CMP_STRIDE * KV_LANES), pe, w1c, w2b)
    nseg = cmp.shape[2]
    cmp5 = cmp.reshape(b, 2, nseg, KV_HEADS, HEAD_DIM)
    o_cmp, q_aug = _cmp_select(q.reshape(b, t, NSA_Q), cmp5[:, 0].transpose(0, 2, 1, 3),
                               cmp5[:, 1].transpose(0, 2, 3, 1), gates, tq=256)
    per_head = lambda v_t: v_t.reshape(b, KV_HEADS, V_ROWS, t)
    o = _flash(q_aug, ks_aug, per_head(vs_t), kw_aug, per_head(vw_t), o_cmp, gates, tq=512, tk=256)
    x = _out_proj(x, o.reshape(b * t, NSA_Q), w_out, mod, TM_PROMPT, tiles)
    kv6 = lambda a_t: jnp.transpose(a_t.reshape(b, 2, KV_HEADS, HEAD_DIM, a_t.shape[-1]), (0, 4, 1, 2, 3))
    return x, kv6(cmp_t), kv6(sel_t), kv6(win_t[:, :, t - min(WINDOW, t):])


def _paged_view(cache):
    return jnp.transpose(cache, (0, 1, 3, 4, 5, 2))


def _nsa_sample_layer(x, g, mod, w_in, w_out, cmp_w, tables, cmp_t, sel_t, win_t, layer, win_buf, page_table, b, t):
    rows = b * t
    past = page_table.shape[1] * PAGE_SIZE
    q, cmp_kv, sel_kv, win_kv, gates = _nsa_proj(x, g, mod, w_in, tables, rows)
    cmp = _compress_paged(cmp_t, layer, page_table, *cmp_w)
    q5 = (q * SCALE).reshape(b, t, KV_HEADS, GROUP, HEAD_DIM).transpose(0, 3, 2, 1, 4)
    qbd = (q5[:, :, :, :, None, :] * jnp.eye(KV_HEADS, dtype=F32)[None, None, :, None, :, None])
    qbd = qbd.reshape(b, GROUP * KV_HEADS * t, KV_HEADS * HEAD_DIM).astype(BF16)
    o_cmp, bias = _dec_cmp_select(qbd, cmp[:, 0], cmp[:, 1], t, past)
    n_sel_pad = bias.shape[2]
    onehot = (jnp.arange(past)[:, None] // SEL_BLOCK == jnp.arange(n_sel_pad)[None, :]).astype(BF16)
    pad8 = lambda a: jnp.pad(a.reshape(b, t, KV_LANES), ((0, 0), (0, 8 - t), (0, 0)))
    wb = win_buf.shape[1]
    o_sel, o_win = _dec_attend(sel_t, win_t, layer, page_table, qbd, bias, onehot, pad8(sel_kv), pad8(win_kv), t, past)
    lane_dense = lambda o: o.reshape(b, GROUP, KV_HEADS, t, HEAD_DIM).transpose(0, 3, 2, 1, 4).reshape(rows, NSA_Q)
    x = _gated_out_proj(x, lane_dense(o_cmp), lane_dense(o_sel), lane_dense(o_win), gates, w_out, mod)
    kv6 = lambda a: a.reshape(b, t, 2, KV_HEADS, HEAD_DIM)
    wkv = jnp.concatenate([win_buf, kv6(win_kv)], axis=1)
    return x, kv6(cmp_kv), kv6(sel_kv), wkv[:, wb + t - min(WINDOW, wb + t):]


def _ml_weights(w_in, b_gate, conv_w, conv_b, wq, wk, wv, skip, gn_g, w_out):
    wqk, wvb = _headwise_weights(wq, wk, wv)
    bg = jnp.pad(b_gate.reshape(1, 2 * ML_HEADS), ((0, 0), (0, LANE - 2 * ML_HEADS)))
    return dict(w_in=_pad_cols(w_in, ML_N).astype(BF16), bg=bg, cw=conv_w, cb=conv_b.reshape(1, ML_INNER),
                wqk=wqk, wv=wvb, skip=skip.reshape(1, ML_INNER), gn=gn_g.reshape(1, ML_INNER),
                w_out=w_out.astype(BF16))


def _ml_prompt_layer(x, g, mod, w, b, t, c_stack, n_layers, layer_out):
    tiles = t // TM_PROMPT
    z, gl, xc, q, k, v, tail = _ml_proj_conv(x, g, mod, w, TM_ML_PROJ, t // TM_ML_PROJ)
    seq = lambda a: a.reshape(b, t, a.shape[-1])
    c0 = jnp.zeros((1, b, ML_HEADS, ML_HEAD_DIM, ML_HEAD_DIM), F32)
    n0 = jnp.zeros((1, b, ML_HEADS, ML_HEAD_DIM), F32)
    m0 = jnp.zeros((b, 8, LANE), F32)
    hn, c, n, m = _ml_cell(seq(q), seq(k), seq(v), seq(gl), c0, n0, 0, m0, ML_CHUNK_PROMPT, c_stack, n_layers,
                           layer_out, gated_norm=(seq(z), seq(xc), w['gn'], w['skip']))
    x = _out_proj(x, hn.reshape(b * t, ML_INNER), w['w_out'], mod, TM_PROMPT, tiles)
    return x, c, n, m[:, 0, :ML_HEADS], tail[:, 8 - (CONV_W - 1):]


def _ml_sample_layer(x, g, mod, w, conv_buf, c0, n0, layer, m0, b, t, c_stack):
    rows = b * t
    xm, z, gl = _ml_proj(x, g, mod, w['w_in'], w['bg'], rows)
    xpad = jnp.concatenate([conv_buf, xm.reshape(b, t, ML_INNER)], axis=1)
    taps = [xpad[:, k:k + t].reshape(rows, ML_INNER) for k in range(CONV_W)]
    xc, q, k, v = _ml_conv_sample(taps, w['cw'], w['cb'], w['wqk'], w['wv'])
    tp = ML_CHUNK_SAMPLE
    pad = lambda a: jnp.pad(a.reshape(b, t, a.shape[-1]), ((0, 0), (0, tp - t), (0, 0)))
    lane = jnp.arange(LANE)[None, None, :]
    step = jnp.arange(tp)[None, :, None]
    glp = jnp.where(step < t, pad(gl), jnp.where(lane < ML_HEADS, NEG, 0.0))
    m0p = jnp.broadcast_to(jnp.pad(m0, ((0, 0), (0, LANE - ML_HEADS)))[:, None, :], (b, 8, LANE))
    hcell, c, n, m = _ml_cell(pad(q), pad(k), pad(v), glp, c0, n0, layer, m0p, tp, c_stack, c0.shape[0], layer)
    x = _ml_out(x, hcell[:, :t].reshape(rows, ML_INNER), z, xc, w['gn'], w['skip'], w['w_out'], mod, rows)
    return x, c, n, m[:, 0, :ML_HEADS], xpad[:, t:]


def kernel(x_prompt, x_sample, cache_cmp, cache_sel, cache_win, state_C, state_n, state_m, state_conv, page_table,
           c_prompt, c_sample, ada_w, ada_b, norm_g, ffn_w_in, ffn_w_out, final_norm_g,
           attn_w_in, attn_w_out, cmp_pos, cmp_w1, cmp_w2,
           ml_w_in, ml_b_gate, ml_conv_w, ml_conv_b, ml_wq, ml_wk, ml_wv, ml_skip, ml_gn_g, ml_w_out):
    bp, tp, _ = x_prompt.shape
    bs, ts, _ = x_sample.shape
    depth = ada_w.shape[0]
    rows_s = bs * ts
    past = page_table.shape[1] * PAGE_SIZE

    n_c = bp + bs
    c_all = jnp.pad(jnp.concatenate([c_prompt, c_sample], axis=0), ((0, -n_c % 8), (0, 0)))
    mods = _adaln(c_all, ada_w, ada_b)

    xp = x_prompt.reshape(bp * tp, D_MODEL)
    xs = x_sample.reshape(rows_s, D_MODEL)
    tables_p = _rope_tables(jnp.arange(tp))
    tables_s = tuple(jnp.tile(tb, (bs, 1)) for tb in _rope_tables(past + jnp.arange(ts)))
    tiles = tp // TM_PROMPT
    cmp_t, sel_t, win_t = _paged_view(cache_cmp), _paged_view(cache_sel), _paged_view(cache_win)

    outs_p = {k: [] for k in ('cmp', 'sel', 'win', 'n', 'm', 'conv')}
    outs_s = {k: [] for k in ('cmp', 'sel', 'win', 'n', 'm', 'conv')}
    c_all_p = c_all_s = None
    for i in range(depth):
        mod_i = mods[i, :n_c].reshape(n_c, 9, D_MODEL)
        mp = [mod_i[:bp, 3 * s:3 * s + 3].reshape(bp, 3, 1, D_MODEL) for s in range(3)]
        ms = [jnp.repeat(mod_i[bp:, 3 * s:3 * s + 3].transpose(1, 0, 2), ts, axis=1) for s in range(3)]
        g = norm_g[i].reshape(3, 1, D_MODEL)
        w_in = ffn_w_in[i].astype(BF16)
        w_out = ffn_w_out[i].astype(BF16)

        xp = _ffn(xp, g[0], mp[0], w_in[0], w_out[0], TM_PROMPT, tiles)
        xs = _ffn(xs, g[0], ms[0], w_in[0], w_out[0], rows_s)
        j = i // 2
        if i % 2 == 0:
            a_in = _pad_cols(attn_w_in[j], NSA_N).astype(BF16)
            a_out = attn_w_out[j].astype(BF16)
            cmp_w = _compress_weights(cmp_pos[j], cmp_w1[j], cmp_w2[j])
            xp, ckv, skv, wkv = _nsa_prompt_layer(xp, g[1], mp[1], a_in, a_out, cmp_w, tables_p, bp, tp)
            xs, ckv_s, skv_s, wkv_s = _nsa_sample_layer(xs, g[1], ms[1], a_in, a_out, cmp_w, tables_s,
                                                        cmp_t, sel_t, win_t, j, cache_win[j], page_table, bs, ts)
            for d, vals in ((outs_p, (ckv, skv, wkv)), (outs_s, (ckv_s, skv_s, wkv_s))):
                for key, val in zip(('cmp', 'sel', 'win'), vals):
                    d[key].append(val)
        else:
            w = _ml_weights(ml_w_in[j], ml_b_gate[j], ml_conv_w[j], ml_conv_b[j], ml_wq[j], ml_wk[j], ml_wv[j],
                            ml_skip[j], ml_gn_g[j], ml_w_out[j])
            xp, c_all_p, n_p, m_p, buf_p = _ml_prompt_layer(xp, g[1], mp[1], w, bp, tp, c_all_p, state_C.shape[0], j)
            xs, c_all_s, n_s, m_s, buf_s = _ml_sample_layer(xs, g[1], ms[1], w, state_conv[j], state_C, state_n, j,
                                                            state_m[j], bs, ts, c_all_s)
            for d, vals in ((outs_p, (n_p, m_p, buf_p)), (outs_s, (n_s, m_s, buf_s))):
                for key, val in zip(('n', 'm', 'conv'), vals):
                    d[key].append(val)
        xp = _ffn(xp, g[2], mp[2], w_in[1], w_out[1], TM_PROMPT, tiles)
        xs = _ffn(xs, g[2], ms[2], w_in[1], w_out[1], rows_s)

    fg = final_norm_g.reshape(1, D_MODEL)
    y_prompt = _final_norm(xp, fg, TM_PROMPT).reshape(bp, tp, D_MODEL)
    y_sample = _final_norm(xs, fg, rows_s).reshape(bs, ts, D_MODEL)
    st = lambda d, key: jnp.stack(d[key])
    return (y_prompt, y_sample,
            st(outs_p, 'cmp'), st(outs_p, 'sel'), st(outs_p, 'win'),
            c_all_p, st(outs_p, 'n'), st(outs_p, 'm'), st(outs_p, 'conv'),
            st(outs_s, 'cmp'), st(outs_s, 'sel'), st(outs_s, 'win'),
            c_all_s, st(outs_s, 'n'), st(outs_s, 'm'), st(outs_s, 'conv'))
```

```python
import functools
import math
from typing import NamedTuple

import numpy as np
import jax
import jax.numpy as jnp
from jax import lax
from jax.experimental import pallas as pl
from jax.experimental.pallas import tpu as pltpu

D_MODEL = 1024
D_FF = 2816
N_HEADS = 16
HEAD_DIM = 64
KV_HEADS = 4
GROUP = 4
ROT_DIM = 16
ROPE_THETA = 500000.0
CMP_BLOCK = 32
CMP_STRIDE = 16
CMP_HIDDEN = 128
SEL_BLOCK = 64
SEL_TOPK = 16
WINDOW = 512
PAGE_SIZE = 128
KV_LANES = 2 * KV_HEADS * HEAD_DIM
NSA_Q = N_HEADS * HEAD_DIM
NSA_N = 2688
ML_INNER = 2048
ML_HEADS = 4
ML_HEAD_DIM = 512
ML_N = 4224
CONV_W = 4
NORM_EPS = 1e-6
NEG = -1e30
BIG = 1e30
SCALE = HEAD_DIM ** -0.5
LOG2E = 1.4426950408889634
V_ROWS = HEAD_DIM + 16

F32 = jnp.float32
BF16 = jnp.bfloat16
LANE = 128
VMEM_LIMIT_BYTES = 56 * 1024 * 1024


def _params(*sem):
    return pltpu.CompilerParams(dimension_semantics=sem, vmem_limit_bytes=VMEM_LIMIT_BYTES)


def _resident(shape):
    nd = len(shape)
    return pl.BlockSpec(shape, lambda *_: (0,) * nd, pipeline_mode=pl.Buffered(1))


class _At(NamedTuple):
    arr: jax.Array
    lead: tuple = ()


def _arr(x):
    return x.arr if isinstance(x, _At) else x


def _param_spec(x):
    if not isinstance(x, _At):
        return _resident(x.shape)
    tail = x.arr.shape[len(x.lead):]
    return pl.BlockSpec((None,) * len(x.lead) + tail, lambda *_: x.lead + (0,) * len(tail),
                        pipeline_mode=pl.Buffered(1))


def _dot(a, b):
    return jnp.dot(a, b, preferred_element_type=F32)


def _dot_nt(a, b):
    return lax.dot_general(a, b, (((1,), (1,)), ((), ())), preferred_element_type=F32)


def _dot_tn(a, b):
    return lax.dot_general(a, b, (((0,), (0,)), ((), ())), preferred_element_type=F32)


def _dot_exact(a, b):
    return jnp.dot(a, b, preferred_element_type=F32, precision=lax.Precision.HIGHEST)


def _dot_nt_exact(a, b):
    return lax.dot_general(a, b, (((1,), (1,)), ((), ())), preferred_element_type=F32,
                           precision=lax.Precision.HIGHEST)


def _modulated_norm(x, g, shift, scale):
    y = x * lax.rsqrt(jnp.mean(x * x, axis=-1, keepdims=True) + NORM_EPS)
    return (y * g) * (1.0 + scale) + shift


def _mod_spec(mod, tm, tiles_per_seq):
    lead = mod.lead if isinstance(mod, _At) else ()
    none = (None,) * len(lead)
    if _arr(mod).ndim - len(lead) == 4:
        return pl.BlockSpec(none + (None, 3, 1, D_MODEL), lambda i: lead + (i // tiles_per_seq, 0, 0, 0))
    return pl.BlockSpec(none + (3, tm, D_MODEL), lambda i: lead + (0, i, 0))


def _adaln_kernel(c_ref, w_ref, b_ref, o_ref):
    a = jax.nn.silu(c_ref[...]).astype(BF16)
    o_ref[0] = _dot(a, w_ref[0].astype(BF16)) + b_ref[0]


def _adaln(c_all, ada_w, ada_b):
    depth, _, n = ada_w.shape
    rows = c_all.shape[0]
    tn = 2304
    return pl.pallas_call(
        _adaln_kernel,
        grid=(depth, n // tn),
        in_specs=[pl.BlockSpec((rows, D_MODEL), lambda l, j: (0, 0)),
                  pl.BlockSpec((1, D_MODEL, tn), lambda l, j: (l, 0, j)),
                  pl.BlockSpec((1, 1, tn), lambda l, j: (l, 0, j))],
        out_specs=pl.BlockSpec((1, rows, tn), lambda l, j: (l, 0, j)),
        out_shape=jax.ShapeDtypeStruct((depth, rows, n), F32),
        compiler_params=_params("arbitrary", "arbitrary"),
        name="adaln",
    )(c_all, ada_w, ada_b.reshape(depth, 1, n))


MXU_DIM = 256
FF_CHUNKS = (6 * MXU_DIM, 5 * MXU_DIM)
assert sum(FF_CHUNKS) == D_FF


def _ffn_kernel(x_ref, g_ref, mod_ref, win_ref, wout_ref, o_ref):
    x = x_ref[...]
    h = _modulated_norm(x, g_ref[...], mod_ref[0], mod_ref[1]).astype(BF16)
    acc = jnp.zeros(x.shape, F32)
    lo = 0
    for width in FF_CHUNKS:
        gate = _dot(h, win_ref[:, lo:lo + width])
        up = _dot(h, win_ref[:, D_FF + lo:D_FF + lo + width])
        act = (jax.nn.silu(gate) * up).astype(BF16)
        acc = acc + _dot(act, wout_ref[lo:lo + width, :])
        lo += width
    o_ref[...] = x + 0.5 * (1.0 + mod_ref[2]) * acc


def _ffn(x, g, mod, w_in, w_out, tm, tiles_per_seq=1):
    rows = x.shape[0]
    return pl.pallas_call(
        _ffn_kernel,
        grid=(rows // tm,),
        in_specs=[pl.BlockSpec((tm, D_MODEL), lambda i: (i, 0)),
                  _param_spec(g),
                  _mod_spec(mod, tm, tiles_per_seq),
                  _param_spec(w_in),
                  _param_spec(w_out)],
        out_specs=pl.BlockSpec((tm, D_MODEL), lambda i: (i, 0)),
        out_shape=jax.ShapeDtypeStruct((rows, D_MODEL), F32),
        compiler_params=_params("arbitrary"),
        name="ffn_half",
    )(x, _arr(g), _arr(mod), _arr(w_in), _arr(w_out))


def _final_norm_kernel(x_ref, g_ref, o_ref):
    x = x_ref[...]
    o_ref[...] = x * lax.rsqrt(jnp.mean(x * x, axis=-1, keepdims=True) + NORM_EPS) * g_ref[...]


def _final_norm(x, g, tm):
    rows = x.shape[0]
    return pl.pallas_call(
        _final_norm_kernel,
        grid=(rows // tm,),
        in_specs=[pl.BlockSpec((tm, D_MODEL), lambda i: (i, 0)), _resident((1, D_MODEL))],
        out_specs=pl.BlockSpec((tm, D_MODEL), lambda i: (i, 0)),
        out_shape=jax.ShapeDtypeStruct((rows, D_MODEL), F32),
        compiler_params=_params("arbitrary"),
        name="final_norm",
    )(x, g)


def _rope_tables(pos):
    half = ROT_DIM // 2
    inv = ROPE_THETA ** (-jnp.arange(0, ROT_DIM, 2, dtype=F32) / ROT_DIM)
    ang = pos.astype(F32)[:, None] * inv[None, :]
    cos, sin = jnp.cos(ang), jnp.sin(ang)
    n = pos.shape[0]
    pad = jnp.zeros((n, HEAD_DIM - ROT_DIM), F32)
    zeros = jnp.zeros((n, half), F32)
    c = jnp.concatenate([cos, cos, pad + 1.0], axis=1)
    s1 = jnp.concatenate([-sin, zeros, pad], axis=1)
    s2 = jnp.concatenate([zeros, sin, pad], axis=1)
    return tuple(jnp.tile(t, (1, LANE // HEAD_DIM)) for t in (c, s1, s2))


def _nsa_proj_kernel(x_ref, g_ref, mod_ref, w_ref, c_ref, s1_ref, s2_ref, *out_refs, prompt_layouts, tiles_per_seq):
    h = _modulated_norm(x_ref[...], g_ref[...], mod_ref[0], mod_ref[1]).astype(BF16)
    proj = _dot(h, w_ref[...])
    c, s1, s2 = c_ref[...], s1_ref[...], s2_ref[...]
    tm = proj.shape[0]
    half = KV_LANES // 2

    def rope(lo, n):
        chunks = []
        for j in range(n // LANE):
            v = proj[:, lo + j * LANE:lo + (j + 1) * LANE]
            chunks.append(v * c + pltpu.roll(v, LANE - ROT_DIM // 2, 1) * s1 + pltpu.roll(v, ROT_DIM // 2, 1) * s2)
        return jnp.concatenate(chunks, axis=1)

    q = rope(0, NSA_Q)
    branches = []
    for br in range(3):
        base = NSA_Q + br * KV_LANES
        branches.append(jnp.concatenate([rope(base, half), proj[:, base + half:base + KV_LANES]], axis=1))
    gates = jax.nn.sigmoid(proj[:, NSA_Q + 3 * KV_LANES:])

    if not prompt_layouts:
        q_ref, cmp_ref, sel_ref, win_ref, gate_ref = out_refs
        q_ref[...] = q
        gate_ref[...] = gates
        for ref, kv in zip((cmp_ref, sel_ref, win_ref), branches):
            ref[...] = kv
        return

    q_ref, cmp_ref, gate_ref, cmp_t_ref, sel_t_ref, win_t_ref, ks_ref, kw_ref, vst_ref, vwt_ref = out_refs
    q_ref[...] = q
    gate_ref[...] = gates
    cmp_ref[...] = branches[0]
    transposed = [kv.T for kv in branches]
    for ref, kv_t in zip((cmp_t_ref, sel_t_ref, win_t_ref), transposed):
        ref[0] = kv_t
    ones_pad = (lax.broadcasted_iota(jnp.int32, (V_ROWS - HEAD_DIM, tm), 0) == 0).astype(F32)
    for ref, kv_t in ((vst_ref, transposed[1]), (vwt_ref, transposed[2])):
        rows = []
        for g in range(KV_HEADS):
            rows += [kv_t[half + g * HEAD_DIM:half + (g + 1) * HEAD_DIM], ones_pad]
        ref[0] = jnp.concatenate(rows, axis=0).astype(BF16)
    pos = (pl.program_id(0) % tiles_per_seq) * tm + lax.broadcasted_iota(jnp.int32, (tm, HEAD_DIM), 0)
    onehot = (pos // SEL_BLOCK == lax.broadcasted_iota(jnp.int32, (tm, HEAD_DIM), 1)).astype(BF16)
    zeros = jnp.zeros((tm, HEAD_DIM), BF16)
    for g in range(KV_HEADS):
        sl = slice(g * HEAD_DIM, (g + 1) * HEAD_DIM)
        ks_ref[0, g, :, 0:HEAD_DIM] = branches[1][:, sl].astype(BF16)
        ks_ref[0, g, :, HEAD_DIM:] = onehot
        kw_ref[0, g, :, 0:HEAD_DIM] = branches[2][:, sl].astype(BF16)
        kw_ref[0, g, :, HEAD_DIM:] = zeros


def _nsa_proj(x, g, mod, w, tables, tm, tiles_per_seq=1, prompt_layouts=False):
    rows = x.shape[0]
    trows = tables[0].shape[0]
    tspec = pl.BlockSpec((tm, LANE), lambda i: (i % (trows // tm), 0))
    row = lambda n: pl.BlockSpec((tm, n), lambda i: (i, 0))
    f32 = lambda *shape: jax.ShapeDtypeStruct(shape, F32)
    bf16 = lambda *shape: jax.ShapeDtypeStruct(shape, BF16)
    if prompt_layouts:
        b, t = rows // (tiles_per_seq * tm), tiles_per_seq * tm
        seq_t = lambda n: pl.BlockSpec((1, n, tm), lambda i: (i // tiles_per_seq, 0, i % tiles_per_seq))
        heads = pl.BlockSpec((1, KV_HEADS, tm, 2 * HEAD_DIM), lambda i: (i // tiles_per_seq, 0, i % tiles_per_seq, 0))
        out_specs = [row(NSA_Q), row(KV_LANES), row(LANE), seq_t(KV_LANES), seq_t(KV_LANES), seq_t(KV_LANES),
                     heads, heads, seq_t(KV_HEADS * V_ROWS), seq_t(KV_HEADS * V_ROWS)]
        out_shape = [f32(rows, NSA_Q), f32(rows, KV_LANES), f32(rows, LANE),
                     f32(b, KV_LANES, t), f32(b, KV_LANES, t), f32(b, KV_LANES, t),
                     bf16(b, KV_HEADS, t, 2 * HEAD_DIM), bf16(b, KV_HEADS, t, 2 * HEAD_DIM),
                     bf16(b, KV_HEADS * V_ROWS, t), bf16(b, KV_HEADS * V_ROWS, t)]
    else:
        out_specs = [row(NSA_Q), row(KV_LANES), row(KV_LANES), row(KV_LANES), row(LANE)]
        out_shape = [f32(rows, n) for n in (NSA_Q, KV_LANES, KV_LANES, KV_LANES, LANE)]
    return pl.pallas_call(
        functools.partial(_nsa_proj_kernel, prompt_layouts=prompt_layouts, tiles_per_seq=tiles_per_seq),
        grid=(rows // tm,),
        in_specs=[row(D_MODEL), _param_spec(g), _mod_spec(mod, tm, tiles_per_seq),
                  _param_spec(w), tspec, tspec, tspec],
        out_specs=out_specs,
        out_shape=out_shape,
        compiler_params=_params("arbitrary"),
        name="nsa_proj",
    )(x, _arr(g), _arr(mod), _arr(w), *tables)


def _compress_finish(seg_proj, pe_ref, w1_ref, w2_ref, out_ref, nseg):
    row = lax.broadcasted_iota(jnp.int32, (nseg, 1), 0)
    for s in range(2):
        bias = jnp.zeros((8, 2 * CMP_HIDDEN), F32)
        lane = lax.broadcasted_iota(jnp.int32, (8, 2 * CMP_HIDDEN), 1)
        for l in range(CMP_STRIDE):
            pa = jnp.broadcast_to(pe_ref[s, l:l + 1, :], (8, HEAD_DIM)).astype(BF16)
            pb = jnp.broadcast_to(pe_ref[s, CMP_STRIDE + l:CMP_STRIDE + l + 1, :], (8, HEAD_DIM)).astype(BF16)
            bias = bias + jnp.where(lane < CMP_HIDDEN, _dot(pa, w1_ref[s, l]), _dot(pb, w1_ref[s, l]))
        bias = bias[0:1]
        for g in range(KV_HEADS):
            a = seg_proj(s, g) + bias
            first, second = a[:, :CMP_HIDDEN], a[:, CMP_HIDDEN:]
            hid = jax.nn.gelu(first + pltpu.roll(second, nseg - 1, 0))
            o = _dot(hid.astype(BF16), w2_ref[s])
            o = jnp.where(row < nseg - 1, o, 0.0)
            out_ref[s, :, g * HEAD_DIM:(g + 1) * HEAD_DIM] = o.astype(out_ref.dtype)


def _compress_prompt_kernel(x_ref, pe_ref, w1_ref, w2_ref, out_ref, acc_ref, *, nseg):
    for l in range(CMP_STRIDE):
        xl = x_ref[0, :, l * KV_LANES:(l + 1) * KV_LANES].astype(BF16)
        for sg in range(2 * KV_HEADS):
            d = _dot(xl[:, sg * HEAD_DIM:(sg + 1) * HEAD_DIM], w1_ref[sg // KV_HEADS, l])
            if l == 0:
                acc_ref[sg] = d
            else:
                acc_ref[sg] += d
    _compress_finish(lambda s, g: acc_ref[s * KV_HEADS + g], pe_ref, w1_ref, w2_ref, out_ref.at[0], nseg)


def _compress_weights(pos_emb, w1, w2):
    pe = jnp.transpose(pos_emb, (1, 0, 2))
    w1c = jnp.concatenate([w1[:, :CMP_STRIDE], w1[:, CMP_STRIDE:]], axis=-1).astype(BF16)
    zero = jnp.zeros_like(w1c)
    w1bd = jnp.concatenate([jnp.concatenate([w1c, zero], axis=-1), jnp.concatenate([zero, w1c], axis=-1)], axis=-2)
    w1bd = w1bd.reshape(2, CMP_STRIDE // 2, 4 * HEAD_DIM, 4 * CMP_HIDDEN)
    return pe, w1c, w1bd, w2.astype(BF16)


def _compress_prompt(cmp_kv, pe, w1c, w2b):
    b, nseg, seg_lanes = cmp_kv.shape
    return pl.pallas_call(
        functools.partial(_compress_prompt_kernel, nseg=nseg),
        grid=(b,),
        in_specs=[pl.BlockSpec((1, nseg, seg_lanes), lambda i: (i, 0, 0)),
                  _resident(pe.shape), _resident(w1c.shape), _resident(w2b.shape)],
        out_specs=pl.BlockSpec((1, 2, nseg, KV_LANES // 2), lambda i: (i, 0, 0, 0)),
        out_shape=jax.ShapeDtypeStruct((b, 2, nseg, KV_LANES // 2), BF16),
        scratch_shapes=[pltpu.VMEM((2 * KV_HEADS, nseg, 2 * CMP_HIDDEN), F32)],
        compiler_params=_params("arbitrary"),
        name="compress_prompt",
    )(cmp_kv, pe, w1c, w2b)


PAGES_PER_STEP = 16
DEC_SEQS_PER_STEP = 2
SEGS_PER_PAGE = PAGE_SIZE // CMP_STRIDE


def _page_spec(layer, k):
    return pl.BlockSpec((1, 1, 2, KV_HEADS, HEAD_DIM, PAGE_SIZE),
                        lambda i, s, pt: (layer, pt[i, s * PAGES_PER_STEP + k], 0, 0, 0, 0))


def _compress_paged_kernel(pt_ref, *refs, nseg, n_steps):
    del pt_ref
    pages = refs[:PAGES_PER_STEP]
    pe_ref, w1_ref, w1bd_ref, w2_ref, out_ref, rows_ref, acc_ref = refs[PAGES_PER_STEP:]
    step = pl.program_id(1)
    pair = 2 * HEAD_DIM
    for k, page in enumerate(pages):
        for s in range(2):
            for gp in range(KV_HEADS // 2):
                tile = page[0, 0, s, 2 * gp:2 * gp + 2].reshape(pair, PAGE_SIZE)
                rows_ref[s * 2 + gp, k * PAGE_SIZE:(k + 1) * PAGE_SIZE, :] = tile.T
    segs = PAGES_PER_STEP * SEGS_PER_PAGE
    seg0 = pl.multiple_of(step * segs, segs)
    for sgp in range(KV_HEADS):
        d = None
        for l in range(0, CMP_STRIDE, 2):
            xl = jnp.concatenate([rows_ref[sgp, pl.ds(l + i, segs, stride=CMP_STRIDE), :] for i in range(2)],
                                 axis=1).astype(BF16)
            dl = _dot(xl, w1bd_ref[sgp // 2, l // 2])
            d = dl if d is None else d + dl
        acc_ref[sgp, pl.ds(seg0, segs), :] = d

    @pl.when(step == n_steps - 1)
    def _():
        width = 2 * CMP_HIDDEN

        def seg_proj(s, g):
            return acc_ref[s * 2 + g // 2, :, (g % 2) * width:(g % 2 + 1) * width]

        _compress_finish(seg_proj, pe_ref, w1_ref, w2_ref, out_ref.at[0], nseg)


def _compress_paged(cache_t, layer, page_table, pe, w1c, w1bd, w2b):
    b, n_pages = page_table.shape
    n_steps = n_pages // PAGES_PER_STEP
    nseg = n_pages * SEGS_PER_PAGE
    const = lambda a: pl.BlockSpec(a.shape, lambda i, s, pt: (0,) * a.ndim)
    grid_spec = pltpu.PrefetchScalarGridSpec(
        num_scalar_prefetch=1,
        grid=(b, n_steps),
        in_specs=[_page_spec(layer, k) for k in range(PAGES_PER_STEP)] + [const(pe), const(w1c), const(w1bd), const(w2b)],
        out_specs=pl.BlockSpec((1, 2, nseg, KV_LANES // 2), lambda i, s, pt: (i, 0, 0, 0)),
        scratch_shapes=[pltpu.VMEM((KV_HEADS, PAGES_PER_STEP * PAGE_SIZE, 2 * HEAD_DIM), F32),
                        pltpu.VMEM((KV_HEADS, nseg, 4 * CMP_HIDDEN), F32)],
    )
    return pl.pallas_call(
        functools.partial(_compress_paged_kernel, nseg=nseg, n_steps=n_steps),
        grid_spec=grid_spec,
        out_shape=jax.ShapeDtypeStruct((b, 2, nseg, KV_LANES // 2), BF16),
        compiler_params=_params("arbitrary", "arbitrary"),
        name="compress_paged",
    )(page_table, *([cache_t] * PAGES_PER_STEP), pe, w1c, w1bd, w2b)


def _importance_matrix(nc_pad, nc, n_sel_pad, n_sel):
    i = np.arange(nc_pad)[:, None]
    j = np.arange(n_sel_pad)[None, :]
    overlap = np.maximum(np.minimum(i * CMP_STRIDE + CMP_BLOCK, (j + 1) * SEL_BLOCK)
                         - np.maximum(i * CMP_STRIDE, j * SEL_BLOCK), 0)
    m = overlap / CMP_STRIDE * (i < nc) * (j < n_sel)
    return jnp.asarray(m, dtype=BF16)


def _topk_mask(score, k, axis):
    n = score.shape[axis]

    def scan(sc, unrolled):
        idx = lax.broadcasted_iota(jnp.int32, sc.shape, axis).astype(F32)

        def body(_, carry):
            sc, sel = carry
            m = jnp.max(sc, axis=axis, keepdims=True)
            first = jnp.min(jnp.where(sc == m, idx, float(n)), axis=axis, keepdims=True)
            hit = idx == first
            return jnp.where(hit, -jnp.inf, sc), jnp.where(hit, 1.0, sel)

        carry = (sc, jnp.zeros(sc.shape, F32))
        if unrolled:
            for i in range(k):
                carry = body(i, carry)
            return carry[1]
        return lax.fori_loop(0, k, body, carry)[1]

    if axis == 0 and score.shape[1] % LANE == 0:
        return jnp.concatenate([scan(score[:, c:c + LANE], True) for c in range(0, score.shape[1], LANE)], axis=1)
    return scan(score, False)


def _block_scores(imp, qpos, n_valid, axis):
    jj = lax.broadcasted_iota(jnp.int32, imp.shape, axis)
    cur = qpos // SEL_BLOCK
    forced = (jj == 0) | (jj == cur) | (jj == cur - 1)
    score = jnp.where(forced, BIG, jnp.where(jj <= cur, imp, NEG))
    return jnp.where(jj < n_valid, score, -jnp.inf)


def _cmp_select_kernel(q_ref, kc_ref, vct_ref, impt_ref, gate_ref, o_ref, qa_ref, *, tq, nc, n_sel):
    q0 = pl.program_id(1) * tq
    rows = GROUP * tq
    qpos = q0 + lax.broadcasted_iota(jnp.int32, (1, rows), 1) % tq
    cidx = lax.broadcasted_iota(jnp.int32, (kc_ref.shape[2], 1), 0)
    valid = (cidx * CMP_STRIDE + CMP_BLOCK - 1 <= qpos) & (cidx < nc)
    tpos = q0 + lax.broadcasted_iota(jnp.int32, (1, tq), 1)
    gates_t = gate_ref[...].T
    q2 = (q_ref[0] * (SCALE * LOG2E)).astype(BF16)
    heads = [q2[:, h * HEAD_DIM:(h + 1) * HEAD_DIM] for h in range(N_HEADS)]
    scores = [_dot_nt(kc_ref[0, g], jnp.concatenate(heads[g * GROUP:(g + 1) * GROUP], axis=0))
              for g in range(KV_HEADS)]
    o_t, imp = [], []
    for g in range(KV_HEADS):
        s = jnp.where(valid, scores[g], NEG)
        e = jnp.exp2(s - jnp.max(s, axis=0, keepdims=True))
        p = jnp.where(valid, e / jnp.sum(e, axis=0, keepdims=True), 0.0)
        o = _dot(vct_ref[0, g], p.astype(BF16))
        p_grp = p[:, 0:tq]
        for r in range(1, GROUP):
            p_grp = p_grp + p[:, r * tq:(r + 1) * tq]
        imp.append(_dot(impt_ref[...], p_grp.astype(BF16)))
        for r in range(GROUP):
            h = g * GROUP + r
            o_t.append(o[:, r * tq:(r + 1) * tq] * gates_t[h:h + 1, :])
    o_ref[0] = jnp.concatenate(o_t, axis=0).T
    imp = jnp.concatenate(imp, axis=1)
    tpos = jnp.concatenate([tpos] * KV_HEADS, axis=1)
    sel = _topk_mask(_block_scores(imp, tpos, n_sel, 0), min(SEL_TOPK, n_sel), 0)
    bias_t = jnp.where(sel > 0.0, 0.0, NEG)
    bias = jnp.concatenate([bias_t[:, g * tq:(g + 1) * tq] for g in range(KV_HEADS)], axis=0).T.astype(BF16)
    for h in range(N_HEADS):
        g = h // GROUP
        qa_ref[0, h, :, 0:HEAD_DIM] = heads[h]
        qa_ref[0, h, :, HEAD_DIM:] = bias[:, g * HEAD_DIM:(g + 1) * HEAD_DIM]


def _cmp_select(q, kch, vct, gates, tq):
    b, t, _ = q.shape
    nseg = kch.shape[2]
    n_sel = t // SEL_BLOCK
    assert n_sel <= HEAD_DIM
    imp_t = _importance_matrix(nseg, nseg - 1, HEAD_DIM, n_sel).T
    return pl.pallas_call(
        functools.partial(_cmp_select_kernel, tq=tq, nc=nseg - 1, n_sel=n_sel),
        grid=(b, t // tq),
        in_specs=[pl.BlockSpec((1, tq, NSA_Q), lambda i, j: (i, j, 0)),
                  pl.BlockSpec((1, KV_HEADS, nseg, HEAD_DIM), lambda i, j: (i, 0, 0, 0)),
                  pl.BlockSpec((1, KV_HEADS, HEAD_DIM, nseg), lambda i, j: (i, 0, 0, 0)),
                  pl.BlockSpec((HEAD_DIM, nseg), lambda i, j: (0, 0)),
                  pl.BlockSpec((tq, LANE), lambda i, j: (i * (t // tq) + j, 0))],
        out_specs=[pl.BlockSpec((1, tq, NSA_Q), lambda i, j: (i, j, 0)),
                   pl.BlockSpec((1, N_HEADS, tq, 2 * HEAD_DIM), lambda i, j: (i, 0, j, 0))],
        out_shape=[jax.ShapeDtypeStruct((b, t, NSA_Q), F32),
                   jax.ShapeDtypeStruct((b, N_HEADS, t, 2 * HEAD_DIM), BF16)],
        compiler_params=_params("arbitrary", "arbitrary"),
        name="cmp_select",
    )(q, kch, vct, imp_t, gates)


def _flash_kernel(qa_ref, ks_ref, vst_ref, kw_ref, vwt_ref, oc_ref, gate_ref, o_ref, *, tq, tk):
    g = pl.program_id(1)
    q0 = pl.program_id(2) * tq
    heads = [qa_ref[0, r] for r in range(GROUP)]

    def softmax_step(m, acc, s, v_t):
        m_new = jnp.maximum(m, jnp.max(s, axis=0, keepdims=True))
        a = jnp.exp2(m - m_new)
        p = jnp.exp2(s - m_new)
        return m_new, a * acc + _dot(v_t, p.astype(BF16))

    def tile_pair(j, carry):
        carry = list(carry)
        k0 = pl.multiple_of(j * (2 * tk), 2 * tk)
        starts = [k0, k0 + tk]
        scores = [[_dot_nt(ks_ref[0, 0, pl.ds(st, tk), :], heads[r]) for r in range(GROUP)] for st in starts]
        for st, s_tile in zip(starts, scores):
            v_t = vst_ref[0, 0, :, pl.ds(st, tk)]
            for r in range(GROUP):
                carry[r] = softmax_step(*carry[r], s_tile[r], v_t)
        return tuple(carry)

    init = tuple((jnp.full((1, tq), -jnp.inf, F32), jnp.zeros((V_ROWS, tq), F32)) for _ in range(GROUP))
    carry = lax.fori_loop(0, q0 // tq, tile_pair, init)

    k_loc = lax.broadcasted_iota(jnp.int32, (tk, 1), 0)
    q_loc = lax.broadcasted_iota(jnp.int32, (1, tk), 1)
    causal, older = k_loc <= q_loc, k_loc > q_loc
    plan = []
    for r in range(GROUP):
        for half in range(2):
            qs = q0 + half * tk
            plan += [(r, half, 0, q0 + i * tk, 'causal' if i == half else None) for i in range(half + 1)]
            plan += [(r, half, 1, qs - 2 * tk, 'older'), (r, half, 1, qs - tk, 'valid'), (r, half, 1, qs, 'causal')]
    q_half = [[heads[r][0:tk], heads[r][tk:tq]] for r in range(GROUP)]
    k_refs, v_refs = (ks_ref, kw_ref), (vst_ref, vwt_ref)
    clamp = lambda st: pl.multiple_of(jnp.maximum(st, 0), tk)
    scores = [_dot_nt(k_refs[br][0, 0, pl.ds(clamp(st), tk), :], q_half[r][half]) for r, half, br, st, _ in plan]
    state = {}
    for r in range(GROUP):
        m, acc = carry[r]
        for half in range(2):
            cols = slice(half * tk, (half + 1) * tk)
            state[r, half, 0] = (m[:, cols], acc[:, cols])
            state[r, half, 1] = (jnp.full((1, tk), -jnp.inf, F32), jnp.zeros((V_ROWS, tk), F32))
    for (r, half, br, st, kind), s in zip(plan, scores):
        if kind == 'causal':
            s = jnp.where(causal, s, NEG)
        elif kind == 'older':
            s = jnp.where(older & (st >= 0), s, NEG)
        elif kind == 'valid':
            s = jnp.where(st >= 0, s, NEG)
        state[r, half, br] = softmax_step(*state[r, half, br], s, v_refs[br][0, 0, :, pl.ds(clamp(st), tk)])

    gates_t = gate_ref[...].T
    sub = lax.broadcasted_iota(jnp.int32, gates_t.shape, 0)
    parts = []
    for r in range(GROUP):
        acc = jnp.concatenate([state[r, half, 0][1] for half in range(2)], axis=1)
        acc_w = jnp.concatenate([state[r, half, 1][1] for half in range(2)], axis=1)
        g_sel = jnp.sum(jnp.where(sub == N_HEADS + g * GROUP + r, gates_t, 0.0), axis=0, keepdims=True)
        g_win = jnp.sum(jnp.where(sub == 2 * N_HEADS + g * GROUP + r, gates_t, 0.0), axis=0, keepdims=True)
        d_sel, d_win = acc[HEAD_DIM:HEAD_DIM + 1], acc_w[HEAD_DIM:HEAD_DIM + 1]
        parts.append(acc[:HEAD_DIM] * (g_sel / d_sel) + acc_w[:HEAD_DIM] * (g_win / d_win))
    o = oc_ref[0] + jnp.concatenate(parts, axis=0).T
    o_ref[0] = o.astype(o_ref.dtype)


def _flash(qa, ks, vst, kw, vwt, oc, gates, tq, tk):
    assert tq == 2 * tk == WINDOW
    b, _, t, _ = qa.shape
    grp_lanes = GROUP * HEAD_DIM
    keys = pl.BlockSpec((1, 1, t, 2 * HEAD_DIM), lambda i, g, j: (i, g, 0, 0))
    vals = pl.BlockSpec((1, 1, V_ROWS, t), lambda i, g, j: (i, g, 0, 0))
    return pl.pallas_call(
        functools.partial(_flash_kernel, tq=tq, tk=tk),
        grid=(b, KV_HEADS, t // tq),
        in_specs=[pl.BlockSpec((1, GROUP, tq, 2 * HEAD_DIM), lambda i, g, j: (i, g, j, 0)),
                  keys, vals, keys, vals,
                  pl.BlockSpec((1, tq, grp_lanes), lambda i, g, j: (i, j, g)),
                  pl.BlockSpec((tq, LANE), lambda i, g, j: (i * (t // tq) + j, 0))],
        out_specs=pl.BlockSpec((1, tq, grp_lanes), lambda i, g, j: (i, j, g)),
        out_shape=jax.ShapeDtypeStruct((b, t, NSA_Q), BF16),
        compiler_params=_params("arbitrary", "arbitrary", "arbitrary"),
        name="nsa_flash",
    )(qa, ks, vst, kw, vwt, oc, gates)


def _out_proj_kernel(x_ref, o_ref, w_ref, mod_ref, y_ref):
    y = _dot(o_ref[...].astype(BF16), w_ref[...])
    y_ref[...] = x_ref[...] + (1.0 + mod_ref[2]) * y


def _out_proj(x, o, w, mod, tm, tiles_per_seq=1):
    rows, k = o.shape
    return pl.pallas_call(
        _out_proj_kernel,
        grid=(rows // tm,),
        in_specs=[pl.BlockSpec((tm, D_MODEL), lambda i: (i, 0)),
                  pl.BlockSpec((tm, k), lambda i: (i, 0)),
                  _param_spec(w),
                  _mod_spec(mod, tm, tiles_per_seq)],
        out_specs=pl.BlockSpec((tm, D_MODEL), lambda i: (i, 0)),
        out_shape=jax.ShapeDtypeStruct((rows, D_MODEL), F32),
        compiler_params=_params("arbitrary"),
        name="out_proj",
    )(x, o, _arr(w), _arr(mod))


def _diag_fold(acc):
    rows = acc.shape[0]
    t = rows // (GROUP * KV_HEADS)
    row_g = (lax.broadcasted_iota(jnp.int32, (rows, 1), 0) // t) % KV_HEADS
    out = jnp.zeros((rows, HEAD_DIM), F32)
    for g in range(KV_HEADS):
        out = out + jnp.where(row_g == g, acc[:, g * HEAD_DIM:(g + 1) * HEAD_DIM], 0.0)
    return out


def _dec_cmp_select_kernel(q_ref, kc_ref, vc_ref, imp_ref, o_ref, bias_ref, *, t, nc, n_sel, past):
    rows = q_ref.shape[1]
    qt = KV_HEADS * t
    qpos = past + lax.broadcasted_iota(jnp.int32, (rows, 1), 0) % t
    cidx = lax.broadcasted_iota(jnp.int32, (1, kc_ref.shape[1]), 1)
    valid = (cidx * CMP_STRIDE + CMP_BLOCK - 1 <= qpos) & (cidx < nc)
    s = jnp.where(valid, _dot_nt(q_ref[0], kc_ref[0]), NEG)
    e = jnp.exp(s - jnp.max(s, axis=-1, keepdims=True))
    p = jnp.where(valid, e / jnp.sum(e, axis=-1, keepdims=True), 0.0)
    o_ref[0] = _diag_fold(_dot(p.astype(BF16), vc_ref[0]))
    p_grp = p[0:qt]
    for r in range(1, GROUP):
        p_grp = p_grp + p[r * qt:(r + 1) * qt]
    imp = _dot(p_grp.astype(BF16), imp_ref[...])
    sel = _topk_mask(_block_scores(imp, qpos[0:qt], n_sel, 1), min(SEL_TOPK, n_sel), 1)
    bias = jnp.where(sel > 0.0, 0.0, NEG).astype(BF16)
    for r in range(GROUP):
        bias_ref[0, r * qt:(r + 1) * qt, :] = bias


def _dec_cmp_select(qbd, kc, vc, t, past):
    b, rows, lanes = qbd.shape
    nseg = kc.shape[1]
    n_sel = -(-(past + t) // SEL_BLOCK)
    n_sel_pad = -(-n_sel // LANE) * LANE
    imp_m = _importance_matrix(nseg, nseg - 1, n_sel_pad, n_sel)
    return pl.pallas_call(
        functools.partial(_dec_cmp_select_kernel, t=t, nc=nseg - 1, n_sel=n_sel, past=past),
        grid=(b,),
        in_specs=[pl.BlockSpec((1, rows, lanes), lambda i: (i, 0, 0)),
                  pl.BlockSpec((1, nseg, lanes), lambda i: (i, 0, 0)),
                  pl.BlockSpec((1, nseg, lanes), lambda i: (i, 0, 0)),
                  pl.BlockSpec((nseg, n_sel_pad), lambda i: (0, 0))],
        out_specs=[pl.BlockSpec((1, rows, HEAD_DIM), lambda i: (i, 0, 0)),
                   pl.BlockSpec((1, rows, n_sel_pad), lambda i: (i, 0, 0))],
        out_shape=[jax.ShapeDtypeStruct((b, rows, HEAD_DIM), F32),
                   jax.ShapeDtypeStruct((b, rows, n_sel_pad), BF16)],
        compiler_params=_params("arbitrary"),
        name="dec_cmp_select",
    )(qbd, kc, vc, imp_m)


def _dec_attend_kernel(pt_ref, *refs, t, n_steps, past):
    del pt_ref
    n_page_refs = DEC_SEQS_PER_STEP * PAGES_PER_STEP
    pages = refs[:n_page_refs]
    (q_ref, bias_ref, e_ref, tail_ref, wbuf_ref, wtail_ref,
     osel_ref, owin_ref, m_ref, l_ref, acc_ref) = refs[n_page_refs:]
    step = pl.program_id(1)
    half = KV_LANES // 2
    rows = q_ref.shape[1]
    tq = lax.broadcasted_iota(jnp.int32, (rows, 1), 0) % t

    @pl.when(step == 0)
    def _():
        m_ref[...] = jnp.full(m_ref.shape, -jnp.inf, F32)
        l_ref[...] = jnp.zeros(l_ref.shape, F32)
        acc_ref[...] = jnp.zeros(acc_ref.shape, F32)

    def update(sq, s, weighted_values):
        m = m_ref[sq]
        m_new = jnp.maximum(m, jnp.max(s, axis=-1, keepdims=True))
        a = jnp.exp(m - m_new)
        p = jnp.exp(s - m_new)
        l_ref[sq] = a * l_ref[sq] + jnp.sum(p, axis=-1, keepdims=True)
        acc_ref[sq] = a * acc_ref[sq] + weighted_values(p.astype(BF16))
        m_ref[sq] = m_new

    seqs = range(DEC_SEQS_PER_STEP)
    mine = lambda sq: pages[sq * PAGES_PER_STEP:(sq + 1) * PAGES_PER_STEP]
    transposed = lambda sq, s: jnp.concatenate([page[0, 0, s].reshape(half, PAGE_SIZE) for page in mine(sq)],
                                               axis=1).astype(BF16)
    scores = [_dot(q_ref[sq], transposed(sq, 0)) + _dot_nt(bias_ref[sq], e_ref[...]) for sq in seqs]
    for sq in seqs:
        v_t = transposed(sq, 1)
        update(sq, scores[sq], lambda p: _dot_nt(p, v_t))

    @pl.when(step == n_steps - 1)
    def _():
        for sq in seqs:
            q, bias = q_ref[sq], bias_ref[sq]
            tail = tail_ref[sq]
            tpos = lax.broadcasted_iota(jnp.int32, (1, tail.shape[0]), 1)
            blk = past // SEL_BLOCK
            lane = lax.broadcasted_iota(jnp.int32, bias.shape, 1)
            b_tail = jnp.sum(jnp.where(lane == blk, bias.astype(F32), 0.0), axis=-1, keepdims=True)
            s = _dot_nt(q, tail[:, :half].astype(BF16)) + b_tail
            update(sq, jnp.where(tpos <= tq, s, NEG), lambda p: _dot(p, tail[:, half:].astype(BF16)))
            osel_ref[sq] = _diag_fold(acc_ref[sq] / l_ref[sq])

            wb = wbuf_ref.shape[-1]
            wk_t = wbuf_ref[0, sq, 0].reshape(half, wb).astype(BF16)
            wv_t = wbuf_ref[0, sq, 1].reshape(half, wb).astype(BF16)
            wtail = wtail_ref[sq]
            ipos = lax.broadcasted_iota(jnp.int32, (1, wb), 1)
            d = (past + tq) - (past - wb + ipos)
            s1 = jnp.where((d >= 0) & (d < WINDOW), _dot(q, wk_t), NEG)
            s2 = jnp.where(tpos <= tq, _dot_nt(q, wtail[:, :half].astype(BF16)), NEG)
            m = jnp.maximum(jnp.max(s1, axis=-1, keepdims=True), jnp.max(s2, axis=-1, keepdims=True))
            p1, p2 = jnp.exp(s1 - m), jnp.exp(s2 - m)
            den = jnp.sum(p1, axis=-1, keepdims=True) + jnp.sum(p2, axis=-1, keepdims=True)
            o = _dot_nt(p1.astype(BF16), wv_t) + _dot(p2.astype(BF16), wtail[:, half:].astype(BF16))
            owin_ref[sq] = _diag_fold(o / den)


def _dec_attend(cache_t, win_t, layer, page_table, qbd, bias, onehot, sel_tail, win_tail, t, past):
    b, n_pages = page_table.shape
    n_steps = n_pages // PAGES_PER_STEP
    rows, lanes = qbd.shape[1:]
    n_sel_pad = bias.shape[2]
    keys_per_step = PAGES_PER_STEP * PAGE_SIZE
    ns = DEC_SEQS_PER_STEP
    assert b % ns == 0
    per_seq = lambda arr: pl.BlockSpec((ns,) + arr.shape[1:], lambda i, s, pt: (i,) + (0,) * (arr.ndim - 1))

    def page_spec(sq, k):
        return pl.BlockSpec((1, 1, 2, KV_HEADS, HEAD_DIM, PAGE_SIZE),
                            lambda i, s, pt: (layer, pt[i * ns + sq, s * PAGES_PER_STEP + k], 0, 0, 0, 0))

    n_page_refs = ns * PAGES_PER_STEP
    grid_spec = pltpu.PrefetchScalarGridSpec(
        num_scalar_prefetch=1,
        grid=(b // ns, n_steps),
        in_specs=[page_spec(sq, k) for sq in range(ns) for k in range(PAGES_PER_STEP)]
        + [per_seq(qbd), per_seq(bias),
           pl.BlockSpec((keys_per_step, n_sel_pad), lambda i, s, pt: (s, 0)),
           per_seq(sel_tail),
           pl.BlockSpec((1, ns) + win_t.shape[2:], lambda i, s, pt: (layer, i, 0, 0, 0, 0)),
           per_seq(win_tail)],
        out_specs=[pl.BlockSpec((ns, rows, HEAD_DIM), lambda i, s, pt: (i, 0, 0)),
                   pl.BlockSpec((ns, rows, HEAD_DIM), lambda i, s, pt: (i, 0, 0))],
        scratch_shapes=[pltpu.VMEM((ns, rows, 1), F32), pltpu.VMEM((ns, rows, 1), F32),
                        pltpu.VMEM((ns, rows, lanes), F32)],
    )
    return pl.pallas_call(
        functools.partial(_dec_attend_kernel, t=t, n_steps=n_steps, past=past),
        grid_spec=grid_spec,
        out_shape=[jax.ShapeDtypeStruct((b, rows, HEAD_DIM), F32)] * 2,
        compiler_params=_params("arbitrary", "arbitrary"),
        name="dec_attend",
    )(page_table, *([cache_t] * n_page_refs), qbd, bias, onehot, sel_tail, win_t, win_tail)


def _gated_out_proj_kernel(x_ref, oc_ref, os_ref, ow_ref, gate_ref, ex_ref, w_ref, mod_ref, y_ref):
    gates = gate_ref[...]
    o = jnp.zeros(oc_ref.shape, F32)
    for k, ref in enumerate((oc_ref, os_ref, ow_ref)):
        o = o + _dot_exact(gates, ex_ref[k]) * ref[...]
    y_ref[...] = x_ref[...] + (1.0 + mod_ref[2]) * _dot(o.astype(BF16), w_ref[...])


def _gate_expansion():
    ex = np.zeros((3, LANE, NSA_Q), np.float32)
    for k in range(3):
        for h in range(N_HEADS):
            ex[k, k * N_HEADS + h, h * HEAD_DIM:(h + 1) * HEAD_DIM] = 1.0
    return jnp.asarray(ex)


def _gated_out_proj(x, oc, osel, owin, gates, w, mod):
    rows = x.shape[0]
    full = lambda n: pl.BlockSpec((rows, n), lambda i: (0, 0))
    return pl.pallas_call(
        _gated_out_proj_kernel,
        grid=(1,),
        in_specs=[full(D_MODEL), full(NSA_Q), full(NSA_Q), full(NSA_Q), full(LANE),
                  pl.BlockSpec((3, LANE, NSA_Q), lambda i: (0, 0, 0)),
                  _param_spec(w),
                  _mod_spec(mod, rows, 1)],
        out_specs=full(D_MODEL),
        out_shape=jax.ShapeDtypeStruct((rows, D_MODEL), F32),
        compiler_params=_params("arbitrary"),
        name="gated_out_proj",
    )(x, oc, osel, owin, gates, _gate_expansion(), _arr(w), _arr(mod))


def _ml_proj_kernel(x_ref, g_ref, mod_ref, w_ref, bg_ref, xm_ref, z_ref, gl_ref):
    h = _modulated_norm(x_ref[...], g_ref[...], mod_ref[0], mod_ref[1]).astype(BF16)
    proj = _dot(h, w_ref[...])
    xm_ref[...] = proj[:, :ML_INNER]
    z_ref[...] = proj[:, ML_INNER:2 * ML_INNER]
    gl = proj[:, 2 * ML_INNER:] + bg_ref[...]
    lane = lax.broadcasted_iota(jnp.int32, gl.shape, 1)
    gl_ref[...] = jnp.where(lane < ML_HEADS, gl, jax.nn.log_sigmoid(gl))


def _ml_proj(x, g, mod, w, b_gate, tm, tiles_per_seq=1):
    rows = x.shape[0]
    row = lambda n: pl.BlockSpec((tm, n), lambda i: (i, 0))
    return pl.pallas_call(
        _ml_proj_kernel,
        grid=(rows // tm,),
        in_specs=[row(D_MODEL), _param_spec(g), _mod_spec(mod, tm, tiles_per_seq),
                  _param_spec(w), _resident((1, LANE))],
        out_specs=[row(ML_INNER), row(ML_INNER), row(LANE)],
        out_shape=[jax.ShapeDtypeStruct((rows, n), F32) for n in (ML_INNER, ML_INNER, LANE)],
        compiler_params=_params("arbitrary"),
        name="ml_proj",
    )(x, _arr(g), _arr(mod), _arr(w), b_gate)


def _conv_qkv(taps, cw_ref, cb_ref, wqk_ref, wv_ref, xc_ref, q_ref, k_ref, v_ref):
    xc = cb_ref[...]
    for w in range(CONV_W):
        xc = xc + taps[w] * cw_ref[w:w + 1, :]
    xc = jax.nn.silu(xc)
    xc_ref[...] = xc
    xcb = xc.astype(BF16)
    xmb = taps[CONV_W - 1].astype(BF16)
    for j in range(ML_INNER // MXU_DIM):
        sl = slice(j * MXU_DIM, (j + 1) * MXU_DIM)
        qk = _dot(xcb[:, sl], wqk_ref[j])
        q_ref[:, sl] = qk[:, :MXU_DIM].astype(q_ref.dtype)
        k_ref[:, sl] = (qk[:, MXU_DIM:] * (ML_HEAD_DIM ** -0.5)).astype(k_ref.dtype)
        v_ref[:, sl] = _dot(xmb[:, sl], wv_ref[j]).astype(v_ref.dtype)


def _ml_proj_conv_kernel(x_ref, g_ref, mod_ref, w_ref, bg_ref, cw_ref, cb_ref, wqk_ref, wv_ref,
                         z_ref, gl_ref, xc_ref, q_ref, k_ref, v_ref, tail_ref, pad_ref, *, tm, tiles_per_seq):
    tile = pl.program_id(0) % tiles_per_seq
    h = _modulated_norm(x_ref[...], g_ref[...], mod_ref[0], mod_ref[1]).astype(BF16)
    xm = _dot(h, w_ref[:, :ML_INNER])
    z_ref[...] = _dot(h, w_ref[:, ML_INNER:2 * ML_INNER])
    gl = _dot(h, w_ref[:, 2 * ML_INNER:]) + bg_ref[...]
    lane = lax.broadcasted_iota(jnp.int32, gl.shape, 1)
    gl_ref[...] = jnp.where(lane < ML_HEADS, gl, jax.nn.log_sigmoid(gl))

    @pl.when(tile == 0)
    def _():
        pad_ref[0:8, :] = jnp.zeros((8, ML_INNER), F32)

    pad_ref[8:, :] = xm
    taps = [pad_ref[pl.ds(8 - (CONV_W - 1) + w, tm), :] for w in range(CONV_W)]
    _conv_qkv(taps, cw_ref, cb_ref, wqk_ref, wv_ref, xc_ref, q_ref, k_ref, v_ref)
    pad_ref[0:8, :] = xm[tm - 8:, :]

    @pl.when(tile == tiles_per_seq - 1)
    def _():
        tail_ref[0] = xm[tm - 8:, :]


def _ml_conv_sample_kernel(t0_ref, t1_ref, t2_ref, t3_ref, cw_ref, cb_ref, wqk_ref, wv_ref,
                           xc_ref, q_ref, k_ref, v_ref):
    taps = [r[...] for r in (t0_ref, t1_ref, t2_ref, t3_ref)]
    _conv_qkv(taps, cw_ref, cb_ref, wqk_ref, wv_ref, xc_ref, q_ref, k_ref, v_ref)


def _headwise_weights(wq, wk, wv):
    def blockdiag(w):
        n_chunks = ML_INNER // MXU_DIM
        per = MXU_DIM // 4
        wc = w.reshape(n_chunks, per, 4, 4)
        eye = jnp.eye(per, dtype=w.dtype)
        return jnp.einsum('jncd,nm->jncmd', wc, eye).reshape(n_chunks, MXU_DIM, MXU_DIM)
    return jnp.concatenate([blockdiag(wq), blockdiag(wk)], axis=-1).astype(BF16), blockdiag(wv).astype(BF16)


def _ml_proj_conv(x, g, mod, w, tm, tiles_per_seq):
    rows = x.shape[0]
    row = lambda n: pl.BlockSpec((tm, n), lambda i: (i, 0))
    f32 = lambda n: jax.ShapeDtypeStruct((rows, n), F32)
    bf16 = jax.ShapeDtypeStruct((rows, ML_INNER), BF16)
    n_seq = rows // (tm * tiles_per_seq)
    consts = [w['w_in'], w['bg'], w['cw'], w['cb'], w['wqk'], w['wv']]
    return pl.pallas_call(
        functools.partial(_ml_proj_conv_kernel, tm=tm, tiles_per_seq=tiles_per_seq),
        grid=(rows // tm,),
        in_specs=[row(D_MODEL), _param_spec(g), _mod_spec(mod, tm, tiles_per_seq)]
        + [_param_spec(c) for c in consts],
        out_specs=[row(ML_INNER), row(LANE), row(ML_INNER), row(ML_INNER), row(ML_INNER), row(ML_INNER),
                   pl.BlockSpec((1, 8, ML_INNER), lambda i: (i // tiles_per_seq, 0, 0))],
        out_shape=[f32(ML_INNER), f32(LANE), f32(ML_INNER), bf16, bf16, bf16,
                   jax.ShapeDtypeStruct((n_seq, 8, ML_INNER), F32)],
        scratch_shapes=[pltpu.VMEM((tm + 8, ML_INNER), F32)],
        compiler_params=_params("arbitrary"),
        name="ml_proj_conv",
    )(x, _arr(g), _arr(mod), *[_arr(c) for c in consts])


def _ml_conv_sample(taps, cw, cb, wqk, wv):
    rows = taps[0].shape[0]
    full = pl.BlockSpec((rows, ML_INNER), lambda i: (0, 0))
    const = lambda a: pl.BlockSpec(a.shape, lambda i: (0,) * a.ndim)
    out = jax.ShapeDtypeStruct((rows, ML_INNER), F32)
    return pl.pallas_call(
        _ml_conv_sample_kernel,
        grid=(1,),
        in_specs=[full] * 4 + [const(cw), const(cb), const(wqk), const(wv)],
        out_specs=[full] * 4,
        out_shape=[out] * 4,
        compiler_params=_params("arbitrary"),
        name="ml_conv_sample",
    )(*taps, cw, cb, wqk, wv)


def _ml_cell_kernel(*refs, chunk, gated_norm, aliased_stack):
    if aliased_stack:
        refs = refs[1:]
    if gated_norm:
        (q_ref, k_ref, v_ref, gl_ref, c0_ref, n0_ref, m0_ref, z_ref, xc_ref, gn_ref, skip_ref,
         h_ref, c_ref, n_ref, m_ref) = refs
    else:
        q_ref, k_ref, v_ref, gl_ref, c0_ref, n0_ref, m0_ref, h_ref, c_ref, n_ref, m_ref = refs
    step = pl.program_id(1)

    @pl.when(step == 0)
    def _():
        c_ref[...] = c0_ref[0]
        n_ref[...] = n0_ref[0]
        m_ref[...] = m0_ref[...]

    gl = gl_ref[0]
    ti = lax.broadcasted_iota(jnp.int32, (chunk, chunk), 0)
    si = lax.broadcasted_iota(jnp.int32, (chunk, chunk), 1)
    tri = si <= ti
    cum = _dot_exact(tri.astype(F32), gl)
    sel = (lax.broadcasted_iota(jnp.int32, (8, LANE), 0) == lax.broadcasted_iota(jnp.int32, (8, LANE), 1)).astype(F32)
    gl_rows = _dot_nt_exact(sel, gl)
    cum_rows = _dot_nt_exact(sel, cum)
    for h in range(ML_HEADS):
        hs = slice(h * ML_HEAD_DIM, (h + 1) * ML_HEAD_DIM)
        q = q_ref[0, :, hs].astype(BF16)
        k = k_ref[0, :, hs].astype(F32)
        v = v_ref[0, :, hs].astype(BF16)
        ig_col = gl[:, h:h + 1]
        ig_row = gl_rows[h:h + 1, :]
        b_col = cum[:, ML_HEADS + h:ML_HEADS + h + 1]
        b_row = cum_rows[ML_HEADS + h:ML_HEADS + h + 1, :]
        b_last = b_col[chunk - 1:chunk, :]
        m_prev = m_ref[0, 0:1, h:h + 1]
        c_prev = c_ref[0, h]
        n_prev = n_ref[0, h:h + 1, :]

        dmat = jnp.where(tri, b_col - b_row + ig_row, -jnp.inf)
        inter = b_col + m_prev
        m_t = jnp.maximum(inter, jnp.max(dmat, axis=-1, keepdims=True))
        w_int = jnp.exp(inter - m_t)
        s = _dot_nt(q, k.astype(BF16)) * jnp.exp(dmat - m_t)
        num = w_int * _dot(q, c_prev.astype(BF16)) + _dot(s.astype(BF16), v)
        den = w_int * jnp.sum(q.astype(F32) * n_prev, axis=-1, keepdims=True) + jnp.sum(s, axis=-1, keepdims=True)
        hcell = num / jnp.maximum(jnp.abs(den), jnp.exp(-m_t))
        if gated_norm:
            hg = hcell * jax.nn.sigmoid(z_ref[0, :, hs])
            hn = hg * lax.rsqrt(jnp.mean(hg * hg, axis=-1, keepdims=True) + NORM_EPS)
            hcell = hn * gn_ref[:, hs] + skip_ref[:, hs] * xc_ref[0, :, hs]
        h_ref[0, :, hs] = hcell.astype(h_ref.dtype)

        m_new = m_t[chunk - 1:chunk, :]
        w_dec = jnp.exp(b_last + m_prev - m_new)
        kw = k * jnp.exp(b_last - b_col + ig_col - m_new)
        c_ref[0, h] = w_dec * c_prev + _dot_tn(kw.astype(BF16), v)
        n_ref[0, h:h + 1, :] = w_dec * n_prev + jnp.sum(kw, axis=0, keepdims=True)
        m_ref[0, :, h:h + 1] = jnp.broadcast_to(m_new, (8, 1))


def _ml_cell(q, k, v, gl, c0, n0, layer, m0, chunk, c_stack, n_layers, layer_out, gated_norm=None):
    b, t, _ = q.shape
    n_chunks = t // chunk
    seq = lambda n: pl.BlockSpec((1, chunk, n), lambda i, c: (i, c, 0))
    c_spec = pl.BlockSpec((None, 1, ML_HEADS, ML_HEAD_DIM, ML_HEAD_DIM), lambda i, c: (layer_out, i, 0, 0, 0))
    n_spec = pl.BlockSpec((1, ML_HEADS, ML_HEAD_DIM), lambda i, c: (i, 0, 0))
    m_spec = pl.BlockSpec((1, 8, LANE), lambda i, c: (i, 0, 0))
    c0_spec = pl.BlockSpec((1, 1, ML_HEADS, ML_HEAD_DIM, ML_HEAD_DIM), lambda i, c: (layer, i, 0, 0, 0))
    n0_spec = pl.BlockSpec((1, 1, ML_HEADS, ML_HEAD_DIM), lambda i, c: (layer, i, 0, 0))
    in_specs = [seq(ML_INNER), seq(ML_INNER), seq(ML_INNER), seq(LANE), c0_spec, n0_spec, m_spec]
    args = [q, k, v, gl, c0, n0, m0]
    if gated_norm is not None:
        const = pl.BlockSpec((1, ML_INNER), lambda i, c: (0, 0))
        in_specs += [seq(ML_INNER), seq(ML_INNER), const, const]
        args += list(gated_norm)
    aliases = {}
    if c_stack is not None:
        in_specs = [pl.BlockSpec(memory_space=pl.ANY)] + in_specs
        args = [c_stack] + args
        aliases = {0: 1}
    return pl.pallas_call(
        functools.partial(_ml_cell_kernel, chunk=chunk, gated_norm=gated_norm is not None,
                          aliased_stack=c_stack is not None),
        grid=(b, n_chunks),
        in_specs=in_specs,
        out_specs=[seq(ML_INNER), c_spec, n_spec, m_spec],
        out_shape=[jax.ShapeDtypeStruct((b, t, ML_INNER), F32 if gated_norm is None else BF16),
                   jax.ShapeDtypeStruct((n_layers,) + c0.shape[1:], F32),
                   jax.ShapeDtypeStruct(n0.shape[1:], F32),
                   jax.ShapeDtypeStruct(m0.shape, F32)],
        input_output_aliases=aliases,
        compiler_params=_params("arbitrary", "arbitrary"),
        name="ml_cell",
    )(*args)


def _ml_out_kernel(x_ref, h_ref, z_ref, xc_ref, gn_ref, skip_ref, w_ref, mod_ref, y_ref):
    hg = h_ref[...] * jax.nn.sigmoid(z_ref[...])
    parts = []
    for h in range(ML_HEADS):
        hh = hg[:, h * ML_HEAD_DIM:(h + 1) * ML_HEAD_DIM]
        parts.append(hh * lax.rsqrt(jnp.mean(hh * hh, axis=-1, keepdims=True) + NORM_EPS))
    hn = jnp.concatenate(parts, axis=-1) * gn_ref[...] + skip_ref[...] * xc_ref[...]
    y_ref[...] = x_ref[...] + (1.0 + mod_ref[2]) * _dot(hn.astype(BF16), w_ref[...])


def _ml_out(x, hcell, z, xc, gn, skip, w, mod, tm, tiles_per_seq=1):
    rows = x.shape[0]
    row = lambda n: pl.BlockSpec((tm, n), lambda i: (i, 0))
    return pl.pallas_call(
        _ml_out_kernel,
        grid=(rows // tm,),
        in_specs=[row(D_MODEL), row(ML_INNER), row(ML_INNER), row(ML_INNER),
                  _resident((1, ML_INNER)), _resident((1, ML_INNER)), _param_spec(w),
                  _mod_spec(mod, tm, tiles_per_seq)],
        out_specs=row(D_MODEL),
        out_shape=jax.ShapeDtypeStruct((rows, D_MODEL), F32),
        compiler_params=_params("arbitrary"),
        name="ml_out",
    )(x, hcell, z, xc, gn, skip, _arr(w), _arr(mod))


TM_PROMPT = 512
TM_ML_PROJ = 256
ML_CHUNK_PROMPT = 256
ML_CHUNK_SAMPLE = 8


def _pad_cols(w, n):
    return jnp.pad(w, ((0, 0), (0, n - w.shape[1])))


def _nsa_prompt_layer(x, g, mod, w_in, w_out, cmp_w, tables, b, t):
    tiles = t // TM_PROMPT
    q, cmp_kv, gates, cmp_t, sel_t, win_t, ks_aug, kw_aug, vs_t, vw_t = _nsa_proj(
        x, g, mod, w_in, tables, TM_PROMPT, tiles, prompt_layouts=True)
    pe, w1c, _, w2b = cmp_w
    cmp = _compress_prompt(cmp_kv.reshape(b, t // CMP_STRIDE, CMP_STRIDE * KV_LANES), pe, w1c, w2b)
    nseg = cmp.shape[2]
    cmp5 = cmp.reshape(b, 2, nseg, KV_HEADS, HEAD_DIM)
    o_cmp, q_aug = _cmp_select(q.reshape(b, t, NSA_Q), cmp5[:, 0].transpose(0, 2, 1, 3),
                               cmp5[:, 1].transpose(0, 2, 3, 1), gates, tq=256)
    per_head = lambda v_t: v_t.reshape(b, KV_HEADS, V_ROWS, t)
    o = _flash(q_aug, ks_aug, per_head(vs_t), kw_aug, per_head(vw_t), o_cmp, gates, tq=512, tk=256)
    x = _out_proj(x, o.reshape(b * t, NSA_Q), w_out, mod, TM_PROMPT, tiles)
    kv6 = lambda a_t: jnp.transpose(a_t.reshape(b, 2, KV_HEADS, HEAD_DIM, a_t.shape[-1]), (0, 4, 1, 2, 3))
    return x, kv6(cmp_t), kv6(sel_t), kv6(win_t[:, :, t - min(WINDOW, t):])


def _paged_view(cache):
    return jnp.transpose(cache, (0, 1, 3, 4, 5, 2))


def _nsa_sample_layer(x, g, mod, w_in, w_out, cmp_w, tables, cmp_t, sel_t, win_t, layer, win_buf, page_table, b, t):
    rows = b * t
    past = page_table.shape[1] * PAGE_SIZE
    q, cmp_kv, sel_kv, win_kv, gates = _nsa_proj(x, g, mod, w_in, tables, rows)
    cmp = _compress_paged(cmp_t, layer, page_table, *cmp_w)
    q5 = (q * SCALE).reshape(b, t, KV_HEADS, GROUP, HEAD_DIM).transpose(0, 3, 2, 1, 4)
    qbd = (q5[:, :, :, :, None, :] * jnp.eye(KV_HEADS, dtype=F32)[None, None, :, None, :, None])
    qbd = qbd.reshape(b, GROUP * KV_HEADS * t, KV_HEADS * HEAD_DIM).astype(BF16)
    o_cmp, bias = _dec_cmp_select(qbd, cmp[:, 0], cmp[:, 1], t, past)
    n_sel_pad = bias.shape[2]
    onehot = (jnp.arange(past)[:, None] // SEL_BLOCK == jnp.arange(n_sel_pad)[None, :]).astype(BF16)
    pad8 = lambda a: jnp.pad(a.reshape(b, t, KV_LANES), ((0, 0), (0, 8 - t), (0, 0)))
    wb = win_buf.shape[1]
    o_sel, o_win = _dec_attend(sel_t, win_t, layer, page_table, qbd, bias, onehot, pad8(sel_kv), pad8(win_kv), t, past)
    lane_dense = lambda o: o.reshape(b, GROUP, KV_HEADS, t, HEAD_DIM).transpose(0, 3, 2, 1, 4).reshape(rows, NSA_Q)
    x = _gated_out_proj(x, lane_dense(o_cmp), lane_dense(o_sel), lane_dense(o_win), gates, w_out, mod)
    kv6 = lambda a: a.reshape(b, t, 2, KV_HEADS, HEAD_DIM)
    wkv = jnp.concatenate([win_buf, kv6(win_kv)], axis=1)
    return x, kv6(cmp_kv), kv6(sel_kv), wkv[:, wb + t - min(WINDOW, wb + t):]


def _ml_weights(w_in, b_gate, conv_w, conv_b, wq, wk, wv, skip, gn_g, w_out):
    wqk, wvb = _headwise_weights(wq, wk, wv)
    bg = jnp.pad(b_gate.reshape(1, 2 * ML_HEADS), ((0, 0), (0, LANE - 2 * ML_HEADS)))
    return dict(w_in=w_in, bg=bg, cw=conv_w, cb=conv_b.reshape(1, ML_INNER),
                wqk=wqk, wv=wvb, skip=skip.reshape(1, ML_INNER), gn=gn_g.reshape(1, ML_INNER),
                w_out=w_out)


def _ml_prompt_layer(x, g, mod, w, b, t, c_stack, n_layers, layer_out):
    tiles = t // TM_PROMPT
    z, gl, xc, q, k, v, tail = _ml_proj_conv(x, g, mod, w, TM_ML_PROJ, t // TM_ML_PROJ)
    seq = lambda a: a.reshape(b, t, a.shape[-1])
    c0 = jnp.zeros((1, b, ML_HEADS, ML_HEAD_DIM, ML_HEAD_DIM), F32)
    n0 = jnp.zeros((1, b, ML_HEADS, ML_HEAD_DIM), F32)
    m0 = jnp.zeros((b, 8, LANE), F32)
    hn, c, n, m = _ml_cell(seq(q), seq(k), seq(v), seq(gl), c0, n0, 0, m0, ML_CHUNK_PROMPT, c_stack, n_layers,
                           layer_out, gated_norm=(seq(z), seq(xc), w['gn'], w['skip']))
    x = _out_proj(x, hn.reshape(b * t, ML_INNER), w['w_out'], mod, TM_PROMPT, tiles)
    return x, c, n, m[:, 0, :ML_HEADS], tail[:, 8 - (CONV_W - 1):]


def _ml_sample_layer(x, g, mod, w, conv_buf, c0, n0, layer, m0, b, t, c_stack):
    rows = b * t
    xm, z, gl = _ml_proj(x, g, mod, w['w_in'], w['bg'], rows)
    xpad = jnp.concatenate([conv_buf, xm.reshape(b, t, ML_INNER)], axis=1)
    taps = [xpad[:, k:k + t].reshape(rows, ML_INNER) for k in range(CONV_W)]
    xc, q, k, v = _ml_conv_sample(taps, w['cw'], w['cb'], w['wqk'], w['wv'])
    tp = ML_CHUNK_SAMPLE
    pad = lambda a: jnp.pad(a.reshape(b, t, a.shape[-1]), ((0, 0), (0, tp - t), (0, 0)))
    lane = jnp.arange(LANE)[None, None, :]
    step = jnp.arange(tp)[None, :, None]
    glp = jnp.where(step < t, pad(gl), jnp.where(lane < ML_HEADS, NEG, 0.0))
    m0p = jnp.broadcast_to(jnp.pad(m0, ((0, 0), (0, LANE - ML_HEADS)))[:, None, :], (b, 8, LANE))
    hcell, c, n, m = _ml_cell(pad(q), pad(k), pad(v), glp, c0, n0, layer, m0p, tp, c_stack, c0.shape[0], layer)
    x = _ml_out(x, hcell[:, :t].reshape(rows, ML_INNER), z, xc, w['gn'], w['skip'], w['w_out'], mod, rows)
    return x, c, n, m[:, 0, :ML_HEADS], xpad[:, t:]


def kernel(x_prompt, x_sample, cache_cmp, cache_sel, cache_win, state_C, state_n, state_m, state_conv, page_table,
           c_prompt, c_sample, ada_w, ada_b, norm_g, ffn_w_in, ffn_w_out, final_norm_g,
           attn_w_in, attn_w_out, cmp_pos, cmp_w1, cmp_w2,
           ml_w_in, ml_b_gate, ml_conv_w, ml_conv_b, ml_wq, ml_wk, ml_wv, ml_skip, ml_gn_g, ml_w_out):
    bp, tp, _ = x_prompt.shape
    bs, ts, _ = x_sample.shape
    depth = ada_w.shape[0]
    rows_s = bs * ts
    past = page_table.shape[1] * PAGE_SIZE

    n_c = bp + bs
    c_all = jnp.pad(jnp.concatenate([c_prompt, c_sample], axis=0), ((0, -n_c % 8), (0, 0)))
    mods = _adaln(c_all, ada_w, ada_b)

    xp = x_prompt.reshape(bp * tp, D_MODEL)
    xs = x_sample.reshape(rows_s, D_MODEL)
    tables_p = _rope_tables(jnp.arange(tp))
    tables_s = tuple(jnp.tile(tb, (bs, 1)) for tb in _rope_tables(past + jnp.arange(ts)))
    tiles = tp // TM_PROMPT
    cmp_t, sel_t, win_t = _paged_view(cache_cmp), _paged_view(cache_sel), _paged_view(cache_win)

    outs_p = {k: [] for k in ('cmp', 'sel', 'win', 'n', 'm', 'conv')}
    outs_s = {k: [] for k in ('cmp', 'sel', 'win', 'n', 'm', 'conv')}
    c_all_p = c_all_s = None
    mod5 = mods[:, :n_c].reshape(depth, n_c, 3, 3, D_MODEL)
    mods_p = mod5[:, :bp].transpose(0, 2, 1, 3, 4).reshape(depth, 3, bp, 3, 1, D_MODEL)
    mods_s = jnp.repeat(mod5[:, bp:].transpose(0, 2, 3, 1, 4), ts, axis=3)
    gains = norm_g.reshape(depth, 3, 1, D_MODEL)
    ffn_in, ffn_out = ffn_w_in.astype(BF16), ffn_w_out.astype(BF16)
    attn_in = jnp.pad(attn_w_in, ((0, 0), (0, 0), (0, NSA_N - attn_w_in.shape[2]))).astype(BF16)
    attn_out = attn_w_out.astype(BF16)
    ml_in = jnp.pad(ml_w_in, ((0, 0), (0, 0), (0, ML_N - ml_w_in.shape[2]))).astype(BF16)
    ml_out = ml_w_out.astype(BF16)
    for i in range(depth):
        mp = [_At(mods_p, (i, s)) for s in range(3)]
        ms = [_At(mods_s, (i, s)) for s in range(3)]
        g = [_At(gains, (i, s)) for s in range(3)]
        w_in = [_At(ffn_in, (i, k)) for k in range(2)]
        w_out = [_At(ffn_out, (i, k)) for k in range(2)]

        xp = _ffn(xp, g[0], mp[0], w_in[0], w_out[0], TM_PROMPT, tiles)
        xs = _ffn(xs, g[0], ms[0], w_in[0], w_out[0], rows_s)
        j = i // 2
        if i % 2 == 0:
            a_in = _At(attn_in, (j,))
            a_out = _At(attn_out, (j,))
            cmp_w = _compress_weights(cmp_pos[j], cmp_w1[j], cmp_w2[j])
            xp, ckv, skv, wkv = _nsa_prompt_layer(xp, g[1], mp[1], a_in, a_out, cmp_w, tables_p, bp, tp)
            xs, ckv_s, skv_s, wkv_s = _nsa_sample_layer(xs, g[1], ms[1], a_in, a_out, cmp_w, tables_s,
                                                        cmp_t, sel_t, win_t, j, cache_win[j], page_table, bs, ts)
            for d, vals in ((outs_p, (ckv, skv, wkv)), (outs_s, (ckv_s, skv_s, wkv_s))):
                for key, val in zip(('cmp', 'sel', 'win'), vals):
                    d[key].append(val)
        else:
            w = _ml_weights(_At(ml_in, (j,)), ml_b_gate[j], ml_conv_w[j], ml_conv_b[j], ml_wq[j], ml_wk[j], ml_wv[j],
                            ml_skip[j], ml_gn_g[j], _At(ml_out, (j,)))
            xp, c_all_p, n_p, m_p, buf_p = _ml_prompt_layer(xp, g[1], mp[1], w, bp, tp, c_all_p, state_C.shape[0], j)
            xs, c_all_s, n_s, m_s, buf_s = _ml_sample_layer(xs, g[1], ms[1], w, state_conv[j], state_C, state_n, j,
                                                            state_m[j], bs, ts, c_all_s)
            for d, vals in ((outs_p, (n_p, m_p, buf_p)), (outs_s, (n_s, m_s, buf_s))):
                for key, val in zip(('n', 'm', 'conv'), vals):
                    d[key].append(val)
        xp = _ffn(xp, g[2], mp[2], w_in[1], w_out[1], TM_PROMPT, tiles)
        xs = _ffn(xs, g[2], ms[2], w_in[1], w_out[1], rows_s)

    fg = final_norm_g.reshape(1, D_MODEL)
    y_prompt = _final_norm(xp, fg, TM_PROMPT).reshape(bp, tp, D_MODEL)
    y_sample = _final_norm(xs, fg, rows_s).reshape(bs, ts, D_MODEL)
    st = lambda d, key: jnp.stack(d[key])
    return (y_prompt, y_sample,
            st(outs_p, 'cmp'), st(outs_p, 'sel'), st(outs_p, 'win'),
            c_all_p, st(outs_p, 'n'), st(outs_p, 'm'), st(outs_p, 'conv'),
            st(outs_s, 'cmp'), st(outs_s, 'sel'), st(outs_s, 'win'),
            c_all_s, st(outs_s, 'n'), st(outs_s, 'm'), st(outs_s, 'conv'))
```
